```python
import jax, jax.numpy as jnp
from jax import lax
import numpy as np

D_MODEL = 2048
BATCH = 16
SEQ = 2048
DEPTH = 2

N_EVEN = (DEPTH + 1) // 2
N_ODD = DEPTH // 2
EPS = 1e-6
ROPE_THETA = 10000.0

MLA_HEADS = 8
MLA_NOPE = 128
MLA_ROPE = 64
MLA_V = 128
MLA_Q_RANK = 512
MLA_KV_RANK = 256
ATTN_BLOCK = 128

RET_HEADS = 8
RET_DK = 128
RET_DV = 128
RET_CHUNK = 128

EVEN_IN = MLA_Q_RANK + MLA_KV_RANK + MLA_ROPE + RET_HEADS * (2 * RET_DK + 2 * RET_DV)
EVEN_OUT = MLA_HEADS * MLA_V + RET_HEADS * RET_DV

LRU_WIDTH = D_MODEL
LRU_BLOCKS = 16
LRU_BLOCK = LRU_WIDTH // LRU_BLOCKS
CONV_WIDTH = 4
LRU_C = 8.0

MEM_TOKENS = 256
XA_HEADS = 4
XA_HEAD_DIM = 128
XA_WIDTH = XA_HEADS * XA_HEAD_DIM

PEER_HEADS = 8
PEER_NKEYS = 128
PEER_EXPERTS = PEER_NKEYS * PEER_NKEYS
PEER_QDIM = 256
PEER_HALF = PEER_QDIM // 2
PEER_TOPK = 16
PEER_TOKEN_BLOCK = 128

kernel_name = 'hybrid_mla_retention_rglru_peer'


def rms_norm(x, g):
    x32 = x.astype(jnp.float32)
    y = x32 * lax.rsqrt(jnp.mean(x32 * x32, axis=-1, keepdims=True) + EPS)
    return (y * g.astype(jnp.float32)).astype(x.dtype)


def rope_angles(positions, dim):
    inv = ROPE_THETA ** (-jnp.arange(0, dim, 2, dtype=jnp.float32) / dim)
    ang = positions.astype(jnp.float32)[..., None] * inv
    return jnp.cos(ang)[:, None], jnp.sin(ang)[:, None]


def apply_rope(x, cos, sin):
    x32 = x.astype(jnp.float32)
    x1, x2 = jnp.split(x32, 2, axis=-1)
    return jnp.concatenate([x1 * cos - x2 * sin, x1 * sin + x2 * cos], axis=-1).astype(x.dtype)


def causal_block_attention(q, k, v, scale):
    b, h, s, dq = q.shape
    nb = s // ATTN_BLOCK
    qb = q.reshape(b, h, nb, ATTN_BLOCK, dq).transpose(2, 0, 1, 3, 4)
    kpos = jnp.arange(s)
    neg = jnp.finfo(jnp.float32).min

    def one_block(args):
        qi, i = args
        sc = jnp.einsum('bhqd,bhkd->bhqk', qi, k).astype(jnp.float32) * scale
        qpos = i * ATTN_BLOCK + jnp.arange(ATTN_BLOCK)
        sc = jnp.where(kpos[None, :] <= qpos[:, None], sc, neg)
        p = jax.nn.softmax(sc, axis=-1).astype(v.dtype)
        return jnp.einsum('bhqk,bhkd->bhqd', p, v)

    out = lax.map(one_block, (qb, jnp.arange(nb)))
    return out.transpose(1, 2, 0, 3, 4).reshape(b, h, s, v.shape[-1])


def mla_group(c_q, c_kv, k_rope_in, q_norm_g, w_uq, kv_norm_g, w_ukv, positions):
    b, s, _ = c_q.shape
    q = jnp.einsum('bsr,rf->bsf', rms_norm(c_q, q_norm_g), w_uq)
    q = q.reshape(b, s, MLA_HEADS, MLA_NOPE + MLA_ROPE).transpose(0, 2, 1, 3)
    kv = jnp.einsum('bsr,rf->bsf', rms_norm(c_kv, kv_norm_g), w_ukv)
    kv = kv.reshape(b, s, MLA_HEADS, MLA_NOPE + MLA_V).transpose(0, 2, 1, 3)
    k_nope, v = kv[..., :MLA_NOPE], kv[..., MLA_NOPE:]
    cos, sin = rope_angles(positions, MLA_ROPE)
    q_rope = apply_rope(q[..., MLA_NOPE:], cos, sin)
    k_rope = apply_rope(k_rope_in[:, None], cos, sin)
    q_full = jnp.concatenate([q[..., :MLA_NOPE], q_rope], axis=-1)
    k_full = jnp.concatenate([k_nope, jnp.broadcast_to(k_rope, (b, MLA_HEADS, s, MLA_ROPE))], axis=-1)
    o = causal_block_attention(q_full, k_full, v, (MLA_NOPE + MLA_ROPE) ** -0.5)
    return o.transpose(0, 2, 1, 3).reshape(b, s, MLA_HEADS * MLA_V)


def retention_group(q, k, v, g, gn_g, positions):
    b, s, _ = q.shape
    nc = s // RET_CHUNK
    f32 = jnp.float32

    def heads(t, d):
        return t.reshape(b, s, RET_HEADS, d).transpose(0, 2, 1, 3).astype(f32)

    cos, sin = rope_angles(positions, RET_DK)
    qh = apply_rope(heads(q, RET_DK), cos, sin)
    kh = apply_rope(heads(k, RET_DK), cos, sin) * (RET_DK ** -0.5)
    vh = heads(v, RET_DV)
    log_g = jnp.log(1.0 - 2.0 ** (-5.0 - jnp.arange(RET_HEADS, dtype=f32)))
    idx = jnp.arange(RET_CHUNK, dtype=f32)
    diff = idx[:, None] - idx[None, :]
    decay = jnp.where(diff >= 0, jnp.exp(log_g[:, None, None] * jnp.maximum(diff, 0.0)), 0.0)
    xi = jnp.exp(log_g[:, None] * (idx + 1.0))
    zeta = jnp.exp(log_g[:, None] * (RET_CHUNK - 1.0 - idx))
    chunk_decay = jnp.exp(log_g * RET_CHUNK)
    qc = qh.reshape(b, RET_HEADS, nc, RET_CHUNK, RET_DK)
    kc = kh.reshape(b, RET_HEADS, nc, RET_CHUNK, RET_DK)
    vc = vh.reshape(b, RET_HEADS, nc, RET_CHUNK, RET_DV)
    att = jnp.einsum('bhnqd,bhnkd->bhnqk', qc, kc) * decay[None, :, None]
    inner = jnp.einsum('bhnqk,bhnkv->bhnqv', att, vc)
    dstate = jnp.einsum('bhnkd,bhnkv->bhndv', kc * zeta[None, :, None, :, None], vc)

    def step(state, ds):
        return state * chunk_decay[None, :, None, None] + ds, state

    _, prev = lax.scan(step, jnp.zeros((b, RET_HEADS, RET_DK, RET_DV), f32), jnp.moveaxis(dstate, 2, 0))
    prev = jnp.moveaxis(prev, 0, 2)
    cross = jnp.einsum('bhnqd,bhndv->bhnqv', qc * xi[None, :, None, :, None], prev)
    ret = (inner + cross).reshape(b, RET_HEADS, s, RET_DV)
    mu = jnp.mean(ret, axis=-1, keepdims=True)
    var = jnp.mean(jnp.square(ret - mu), axis=-1, keepdims=True)
    ret = (ret - mu) * lax.rsqrt(var + EPS) * gn_g.astype(f32)[None, :, None, :]
    ret = ret.transpose(0, 2, 1, 3).reshape(b, s, RET_HEADS * RET_DV)
    return (jax.nn.silu(g.astype(f32)) * ret).astype(q.dtype)


def even_mixer(h, positions, w_in, w_out, q_norm_g, w_uq, kv_norm_g, w_ukv, ret_gn_g):
    z = jnp.einsum('bsd,df->bsf', h, w_in)
    sizes = [MLA_Q_RANK, MLA_KV_RANK, MLA_ROPE, RET_HEADS * RET_DK, RET_HEADS * RET_DK, RET_HEADS * RET_DV]
    offsets = np.cumsum(sizes).tolist()
    c_q, c_kv, k_rope, r_q, r_k, r_v, r_g = jnp.split(z, offsets, axis=-1)
    a = mla_group(c_q, c_kv, k_rope, q_norm_g, w_uq, kv_norm_g, w_ukv, positions)
    r = retention_group(r_q, r_k, r_v, r_g, ret_gn_g, positions)
    return jnp.einsum('bsf,fd->bsd', jnp.concatenate([a, r], axis=-1), w_out)


def causal_depthwise_conv(x, w, bias):
    out = lax.conv_general_dilated(x, w[:, None, :], window_strides=(1,), padding=[(CONV_WIDTH - 1, 0)],
                                   dimension_numbers=('NWC', 'WIO', 'NWC'), feature_group_count=x.shape[-1])
    return out + bias


def rg_lru(x, wa, ba, wx, bx, lam):
    b, s, w = x.shape
    f32 = jnp.float32
    xb = x.reshape(b, s, LRU_BLOCKS, LRU_BLOCK)
    r = jax.nn.sigmoid(jnp.einsum('bsni,nij->bsnj', xb, wa).reshape(b, s, w).astype(f32) + ba.astype(f32))
    i = jax.nn.sigmoid(jnp.einsum('bsni,nij->bsnj', xb, wx).reshape(b, s, w).astype(f32) + bx.astype(f32))
    log_a = -LRU_C * r * jax.nn.softplus(-lam.astype(f32))
    a = jnp.exp(log_a)
    u = jnp.sqrt(-jnp.expm1(2.0 * log_a)) * (i * x.astype(f32))

    def step(hprev, inp):
        a_t, u_t = inp
        hn = a_t * hprev + u_t
        return hn, hn

    _, hs = lax.scan(step, jnp.zeros((b, w), f32), (jnp.moveaxis(a, 1, 0), jnp.moveaxis(u, 1, 0)))
    return jnp.moveaxis(hs, 0, 1).astype(x.dtype)


def odd_mixer(h, w_in, w_out, conv_w, conv_b, wa, ba, wx, bx, lam):
    z = jnp.einsum('bsd,df->bsf', h, w_in)
    gate, rec = jnp.split(z, 2, axis=-1)
    rec = rg_lru(causal_depthwise_conv(rec, conv_w, conv_b), wa, ba, wx, bx, lam)
    return jnp.einsum('bsf,fd->bsd', jax.nn.gelu(gate) * rec, w_out)


def memory_cross_attention(h, mem_n, wq, wk, wv, wo):
    b, s, _ = h.shape
    m = mem_n.shape[1]
    q = jnp.einsum('bsd,df->bsf', h, wq).reshape(b, s, XA_HEADS, XA_HEAD_DIM)
    k = jnp.einsum('bmd,df->bmf', mem_n, wk).reshape(b, m, XA_HEADS, XA_HEAD_DIM)
    v = jnp.einsum('bmd,df->bmf', mem_n, wv).reshape(b, m, XA_HEADS, XA_HEAD_DIM)
    sc = jnp.einsum('bshd,bmhd->bhsm', q, k).astype(jnp.float32) * (XA_HEAD_DIM ** -0.5)
    p = jax.nn.softmax(sc, axis=-1).astype(v.dtype)
    o = jnp.einsum('bhsm,bmhd->bshd', p, v).reshape(b, s, XA_WIDTH)
    return jnp.einsum('bsf,fd->bsd', o, wo)


def peer_ffn(h, wq, sub_keys, u, v):
    b, s, d = h.shape
    q = jnp.einsum('bsd,df->bsf', h, wq).reshape(b, s, PEER_HEADS, 2, PEER_HALF)
    scores = jnp.einsum('bshpc,hpkc->bshpk', q, sub_keys).astype(jnp.float32)
    top_s, top_i = lax.top_k(scores, PEER_TOPK)
    cand_s = top_s[..., 0, :, None] + top_s[..., 1, None, :]
    cand_i = top_i[..., 0, :, None] * PEER_NKEYS + top_i[..., 1, None, :]
    nk2 = PEER_TOPK * PEER_TOPK
    best_s, best_j = lax.top_k(cand_s.reshape(b, s, PEER_HEADS, nk2), PEER_TOPK)
    expert_idx = jnp.take_along_axis(cand_i.reshape(b, s, PEER_HEADS, nk2), best_j, axis=-1)
    gates = jax.nn.softmax(best_s, axis=-1).astype(h.dtype)
    n_sel = PEER_HEADS * PEER_TOPK
    nb = (b * s) // PEER_TOKEN_BLOCK
    hb = h.reshape(nb, PEER_TOKEN_BLOCK, d)
    eb = expert_idx.reshape(nb, PEER_TOKEN_BLOCK, n_sel)
    gb = gates.reshape(nb, PEER_TOKEN_BLOCK, n_sel)

    def block(args):
        x_t, e_t, g_t = args
        act = jax.nn.gelu(jnp.einsum('td,ted->te', x_t, jnp.take(u, e_t, axis=0)), approximate=False)
        return jnp.einsum('te,ted->td', g_t * act, jnp.take(v, e_t, axis=0))

    y = lax.map(block, (hb, eb, gb))
    return y.reshape(b, s, d)


def setup_inputs(seed: int = 0) -> dict:
    key = jax.random.key(seed)
    ks = iter(jax.random.split(key, 48))
    f32 = jnp.float32

    def nrm(shape, scale):
        return jax.random.normal(next(ks), shape, f32) * scale

    def gain(shape):
        return 1.0 + 0.02 * jax.random.normal(next(ks), shape, f32)

    x = nrm((BATCH, SEQ, D_MODEL), 1.0)
    mem = nrm((BATCH, MEM_TOKENS, D_MODEL), 1.0)
    offsets = jax.random.randint(next(ks), (BATCH, 1), 0, 4096, dtype=jnp.int32)
    positions = offsets + jnp.arange(SEQ, dtype=jnp.int32)[None, :]
    lam_u = jax.random.uniform(next(ks), (N_ODD, LRU_WIDTH), f32, 0.9, 0.999)
    lam_p = lam_u ** (1.0 / LRU_C)
    return {
        'x': x,
        'mem': mem,
        'positions': positions,
        'mix_norm_g': gain((DEPTH, D_MODEL)),
        'xa_norm_g': gain((DEPTH, D_MODEL)),
        'ffn_norm_g': gain((DEPTH, D_MODEL)),
        'mem_norm_g': gain((DEPTH, D_MODEL)),
        'ev_w_in': nrm((N_EVEN, D_MODEL, EVEN_IN), D_MODEL ** -0.5),
        'ev_w_out': nrm((N_EVEN, EVEN_OUT, D_MODEL), EVEN_OUT ** -0.5),
        'mla_q_norm_g': gain((N_EVEN, MLA_Q_RANK)),
        'mla_w_uq': nrm((N_EVEN, MLA_Q_RANK, MLA_HEADS * (MLA_NOPE + MLA_ROPE)), MLA_Q_RANK ** -0.5),
        'mla_kv_norm_g': gain((N_EVEN, MLA_KV_RANK)),
        'mla_w_ukv': nrm((N_EVEN, MLA_KV_RANK, MLA_HEADS * (MLA_NOPE + MLA_V)), MLA_KV_RANK ** -0.5),
        'ret_gn_g': gain((N_EVEN, RET_HEADS, RET_DV)),
        'od_w_in': nrm((N_ODD, D_MODEL, 2 * LRU_WIDTH), D_MODEL ** -0.5),
        'od_w_out': nrm((N_ODD, LRU_WIDTH, D_MODEL), LRU_WIDTH ** -0.5),
        'conv_w': nrm((N_ODD, CONV_WIDTH, LRU_WIDTH), CONV_WIDTH ** -0.5),
        'conv_b': nrm((N_ODD, LRU_WIDTH), 0.02),
        'lru_wa': nrm((N_ODD, LRU_BLOCKS, LRU_BLOCK, LRU_BLOCK), LRU_BLOCK ** -0.5),
        'lru_ba': nrm((N_ODD, LRU_WIDTH), 0.02),
        'lru_wx': nrm((N_ODD, LRU_BLOCKS, LRU_BLOCK, LRU_BLOCK), LRU_BLOCK ** -0.5),
        'lru_bx': nrm((N_ODD, LRU_WIDTH), 0.02),
        'lru_lambda': jnp.log(lam_p) - jnp.log1p(-lam_p),
        'xa_wq': nrm((DEPTH, D_MODEL, XA_WIDTH), D_MODEL ** -0.5),
        'xa_wk': nrm((DEPTH, D_MODEL, XA_WIDTH), D_MODEL ** -0.5),
        'xa_wv': nrm((DEPTH, D_MODEL, XA_WIDTH), D_MODEL ** -0.5),
        'xa_wo': nrm((DEPTH, XA_WIDTH, D_MODEL), XA_WIDTH ** -0.5),
        'peer_wq': nrm((DEPTH, D_MODEL, PEER_HEADS * PEER_QDIM), D_MODEL ** -0.5),
        'peer_sub_keys': nrm((DEPTH, PEER_HEADS, 2, PEER_NKEYS, PEER_HALF), PEER_HALF ** -0.5),
        'peer_u': nrm((DEPTH, PEER_EXPERTS, D_MODEL), D_MODEL ** -0.5),
        'peer_v': nrm((DEPTH, PEER_EXPERTS, D_MODEL), PEER_HEADS ** -0.5),
        'final_norm_g': gain((D_MODEL,)),
    }


def reference(x, mem, positions, mix_norm_g, xa_norm_g, ffn_norm_g, mem_norm_g,
              ev_w_in, ev_w_out, mla_q_norm_g, mla_w_uq, mla_kv_norm_g, mla_w_ukv, ret_gn_g,
              od_w_in, od_w_out, conv_w, conv_b, lru_wa, lru_ba, lru_wx, lru_bx, lru_lambda,
              xa_wq, xa_wk, xa_wv, xa_wo, peer_wq, peer_sub_keys, peer_u, peer_v, final_norm_g):
    h = x
    for layer in range(DEPTH):
        hn = rms_norm(h, mix_norm_g[layer])
        j = layer // 2
        if layer % 2 == 0:
            h = h + even_mixer(hn, positions, ev_w_in[j], ev_w_out[j], mla_q_norm_g[j], mla_w_uq[j],
                               mla_kv_norm_g[j], mla_w_ukv[j], ret_gn_g[j])
        else:
            h = h + odd_mixer(hn, od_w_in[j], od_w_out[j], conv_w[j], conv_b[j], lru_wa[j], lru_ba[j],
                              lru_wx[j], lru_bx[j], lru_lambda[j])
        h = h + memory_cross_attention(rms_norm(h, xa_norm_g[layer]), rms_norm(mem, mem_norm_g[layer]),
                                       xa_wq[layer], xa_wk[layer], xa_wv[layer], xa_wo[layer])
        h = h + peer_ffn(rms_norm(h, ffn_norm_g[layer]), peer_wq[layer], peer_sub_keys[layer],
                         peer_u[layer], peer_v[layer])
    return rms_norm(h, final_norm_g)
```

```python
import functools
import math

import jax
import jax.numpy as jnp
from jax import lax
from jax.experimental import pallas as pl
from jax.experimental.pallas import tpu as pltpu

F32 = jnp.float32
BF16 = jnp.bfloat16
I32 = jnp.int32

EPS = 1e-6
ROPE_THETA = 10000.0

MLA_HEADS = 8
MLA_NOPE = 128
MLA_ROPE = 64
MLA_V = 128
MLA_Q_RANK = 512
MLA_KV_RANK = 256
MLA_QK = 256

RET_HEADS = 8
RET_DK = 128
RET_DV = 128
RET_BLOCK = 256

LRU_BLOCKS = 16
LRU_BLOCK = 128
CONV_WIDTH = 4
LRU_C = 8.0

XA_HEADS = 4
XA_HEAD_DIM = 128

PEER_HEADS = 8
PEER_NKEYS = 128
PEER_HALF = 128
PEER_TOPK = 16
PEER_SEL = PEER_HEADS * PEER_TOPK

LANES = 128
VMEM_LIMIT = 48 * 1024 * 1024

NEG_INF = float("-inf")


def _cparams(sem):
    return pltpu.CompilerParams(dimension_semantics=sem, vmem_limit_bytes=VMEM_LIMIT)


def _rms(x, g):
    ms = jnp.mean(x * x, axis=-1, keepdims=True)
    return x * lax.rsqrt(ms + EPS) * g


def _norm_matmul_kernel(x_ref, g_ref, w_ref, o_ref, xn_ref):
    @pl.when(pl.program_id(1) == 0)
    def _():
        xn_ref[...] = _rms(x_ref[...], g_ref[...]).astype(BF16)

    o_ref[...] = jnp.dot(xn_ref[...], w_ref[...], preferred_element_type=F32).astype(o_ref.dtype)


def norm_matmul(x, g, w, tm=512, tn=1024):
    t, d = x.shape
    n = w.shape[1]
    tm = min(tm, t)
    tn = min(tn, n)
    return pl.pallas_call(
        _norm_matmul_kernel,
        grid=(t // tm, n // tn),
        in_specs=[
            pl.BlockSpec((tm, d), lambda i, j: (i, 0)),
            pl.BlockSpec((1, d), lambda i, j: (0, 0)),
            pl.BlockSpec((d, tn), lambda i, j: (0, j)),
        ],
        out_specs=pl.BlockSpec((tm, tn), lambda i, j: (i, j)),
        out_shape=jax.ShapeDtypeStruct((t, n), BF16),
        scratch_shapes=[pltpu.VMEM((tm, d), BF16)],
        compiler_params=_cparams(("parallel", "arbitrary")),
        name="norm_matmul",
    )(x, g.reshape(1, d), w)


def _matmul_res_kernel(*refs, n_in):
    xs = refs[:n_in]
    ws = refs[n_in:2 * n_in]
    res_ref = refs[2 * n_in]
    o_ref = refs[2 * n_in + 1]
    acc = res_ref[...]
    for x_ref, w_ref in zip(xs, ws):
        acc = acc + jnp.dot(x_ref[...], w_ref[...], preferred_element_type=F32)
    o_ref[...] = acc


def matmul_res(xs, ws, res, tm=512, tn=1024):
    t, n = res.shape
    tm = min(tm, t)
    tn = min(tn, n)
    n_in = len(xs)
    in_specs = [pl.BlockSpec((tm, x.shape[1]), lambda i, j: (i, 0)) for x in xs]
    in_specs += [pl.BlockSpec((w.shape[0], tn), lambda i, j: (0, j)) for w in ws]
    in_specs += [pl.BlockSpec((tm, tn), lambda i, j: (i, j))]
    return pl.pallas_call(
        functools.partial(_matmul_res_kernel, n_in=n_in),
        grid=(t // tm, n // tn),
        in_specs=in_specs,
        out_specs=pl.BlockSpec((tm, tn), lambda i, j: (i, j)),
        out_shape=jax.ShapeDtypeStruct((t, n), F32),
        compiler_params=_cparams(("parallel", "parallel")),
        name="matmul_res",
    )(*xs, *ws, res)


def _mla_proj_kernel(z_ref, cos_ref, sin_ref, qg_ref, kvg_ref, wqa_ref, wqb_ref, wkv_ref,
                     q_ref, kv_ref, kr_ref):
    scale = (MLA_NOPE + MLA_ROPE) ** -0.5
    z = z_ref[...]
    cq = _rms(z[:, :MLA_Q_RANK].astype(F32), qg_ref[...]).astype(BF16)
    ckv = _rms(z[:, MLA_Q_RANK:MLA_Q_RANK + MLA_KV_RANK].astype(F32), kvg_ref[...]).astype(BF16)
    cos = cos_ref[...]
    sin = sin_ref[...]
    qa = jnp.dot(cq, wqa_ref[...], preferred_element_type=F32)
    qb = jnp.dot(cq, wqb_ref[...], preferred_element_type=F32)
    for h in range(MLA_HEADS):
        base = h * MLA_QK
        q_ref[:, base:base + LANES] = (qa[:, base:base + LANES] * scale).astype(BF16)
        rot = qa[:, base + LANES:base + 2 * LANES] * cos + qb[:, h * LANES:(h + 1) * LANES] * sin
        q_ref[:, base + LANES:base + 2 * LANES] = (rot * scale).astype(BF16)
    kv_ref[...] = jnp.dot(ckv, wkv_ref[...], preferred_element_type=F32).astype(BF16)
    off = MLA_Q_RANK + MLA_KV_RANK
    kr = z[:, off:off + LANES].astype(F32) * cos + z[:, off + LANES:off + 2 * LANES].astype(F32) * sin
    kr_ref[...] = kr.astype(BF16)


def mla_proj(z, cos_t, sin_t, qg, kvg, wqa, wqb, wkv, tm=512):
    t = z.shape[0]
    tm = min(tm, t)
    hq = MLA_HEADS * MLA_QK
    full = lambda a: pl.BlockSpec(a.shape, lambda i: (0, 0))
    return pl.pallas_call(
        _mla_proj_kernel,
        grid=(t // tm,),
        in_specs=[
            pl.BlockSpec((tm, 1024), lambda i: (i, 0)),
            pl.BlockSpec((tm, LANES), lambda i: (i, 0)),
            pl.BlockSpec((tm, LANES), lambda i: (i, 0)),
            full(qg), full(kvg), full(wqa), full(wqb), full(wkv),
        ],
        out_specs=[
            pl.BlockSpec((tm, hq), lambda i: (i, 0)),
            pl.BlockSpec((tm, hq), lambda i: (i, 0)),
            pl.BlockSpec((tm, LANES), lambda i: (i, 0)),
        ],
        out_shape=[
            jax.ShapeDtypeStruct((t, hq), BF16),
            jax.ShapeDtypeStruct((t, hq), BF16),
            jax.ShapeDtypeStruct((t, LANES), BF16),
        ],
        compiler_params=_cparams(("parallel",)),
        name="mla_proj",
    )(z, cos_t, sin_t, qg, kvg, wqa, wqb, wkv)


def _mla_attn_kernel(q_ref, kv_ref, kr_ref, o_ref, kcat_ref, *, seq, tq):
    kcat_ref[:, :LANES] = kv_ref[:, :LANES]
    kcat_ref[:, LANES:] = kr_ref[...]
    for qi in range(seq // tq):
        kend = (qi + 1) * tq
        q = q_ref[qi * tq:(qi + 1) * tq, :]
        s = lax.dot_general(q, kcat_ref[:kend, :], (((1,), (1,)), ((), ())),
                            preferred_element_type=F32)
        row = lax.broadcasted_iota(I32, (tq, kend), 0) + qi * tq
        col = lax.broadcasted_iota(I32, (tq, kend), 1)
        s = jnp.where(col <= row, s, NEG_INF)
        m = jnp.max(s, axis=-1, keepdims=True)
        p = jnp.exp(s - m)
        l = jnp.sum(p, axis=-1, keepdims=True)
        o = jnp.dot(p.astype(BF16), kv_ref[:kend, LANES:], preferred_element_type=F32)
        o_ref[qi * tq:(qi + 1) * tq, :] = (o / l).astype(BF16)


def mla_attention(qf, kvf, kr, batch, seq, tq=256):
    tq = min(tq, seq)
    t = batch * seq
    return pl.pallas_call(
        functools.partial(_mla_attn_kernel, seq=seq, tq=tq),
        grid=(batch, MLA_HEADS),
        in_specs=[
            pl.BlockSpec((seq, MLA_QK), lambda b, h: (b, h)),
            pl.BlockSpec((seq, MLA_QK), lambda b, h: (b, h)),
            pl.BlockSpec((seq, LANES), lambda b, h: (b, 0)),
        ],
        out_specs=pl.BlockSpec((seq, MLA_V), lambda b, h: (b, h)),
        out_shape=jax.ShapeDtypeStruct((t, MLA_HEADS * MLA_V), BF16),
        scratch_shapes=[pltpu.VMEM((seq, MLA_QK), BF16)],
        compiler_params=_cparams(("parallel", "parallel")),
        name="mla_attention",
    )(qf, kvf, kr)


def _retention_kernel(q_ref, k_ref, v_ref, g_ref, cos_ref, sin_ref, logg_ref, gn_ref, o_ref,
                      *, seq, blk):
    lg = logg_ref[...]
    cos = cos_ref[...]
    sin = sin_ref[...]
    half = RET_DK // 2

    def rope(x):
        return x * cos + pltpu.roll(x, half, 1) * sin

    q_all = rope(q_ref[...].astype(F32))
    k_all = rope(k_ref[...].astype(F32)) * (RET_DK ** -0.5)

    ri = lax.broadcasted_iota(I32, (blk, blk), 0)
    ci = lax.broadcasted_iota(I32, (blk, blk), 1)
    diff = (ri - ci).astype(F32)
    decay = jnp.where(diff >= 0, jnp.exp(lg[:, :1] * jnp.maximum(diff, 0.0)), 0.0)
    pos = lax.broadcasted_iota(I32, (blk, RET_DK), 0).astype(F32)
    xi = jnp.exp(lg * (pos + 1.0))
    zeta = jnp.exp(lg * (blk - 1.0 - pos))
    blk_decay = jnp.exp(lg * float(blk))

    state = jnp.zeros((RET_DK, RET_DV), F32)
    for n in range(seq // blk):
        sl = slice(n * blk, (n + 1) * blk)
        qc = q_all[sl]
        kc = k_all[sl]
        vc = v_ref[sl, :]
        att = lax.dot_general(qc.astype(BF16), kc.astype(BF16), (((1,), (1,)), ((), ())),
                              preferred_element_type=F32) * decay
        inner = jnp.dot(att.astype(BF16), vc, preferred_element_type=F32)
        cross = jnp.dot((qc * xi).astype(BF16), state.astype(BF16), preferred_element_type=F32)
        ret = inner + cross
        kz_t = (kc * zeta).T.astype(BF16)
        dstate = jnp.dot(kz_t, vc, preferred_element_type=F32)
        state = state * blk_decay + dstate
        mu = jnp.mean(ret, axis=-1, keepdims=True)
        cen = ret - mu
        var = jnp.mean(cen * cen, axis=-1, keepdims=True)
        normed = cen * lax.rsqrt(var + EPS) * gn_ref[...]
        gate = g_ref[sl, :].astype(F32)
        o_ref[sl, :] = (gate * jax.nn.sigmoid(gate) * normed).astype(BF16)


def retention(z, cos_t, sin_t, logg, gn_g, batch, seq):
    blk = min(RET_BLOCK, seq)
    t = batch * seq
    first = 1024 // LANES
    col = lambda part: (lambda b, h: (b, first + part * RET_HEADS + h))
    return pl.pallas_call(
        functools.partial(_retention_kernel, seq=seq, blk=blk),
        grid=(batch, RET_HEADS),
        in_specs=[
            pl.BlockSpec((seq, LANES), col(0)),
            pl.BlockSpec((seq, LANES), col(1)),
            pl.BlockSpec((seq, LANES), col(2)),
            pl.BlockSpec((seq, LANES), col(3)),
            pl.BlockSpec((seq, LANES), lambda b, h: (b, 0)),
            pl.BlockSpec((seq, LANES), lambda b, h: (b, 0)),
            pl.BlockSpec((None, 1, LANES), lambda b, h: (h, 0, 0)),
            pl.BlockSpec((None, 1, LANES), lambda b, h: (h, 0, 0)),
        ],
        out_specs=pl.BlockSpec((seq, RET_DV), lambda b, h: (b, h)),
        out_shape=jax.ShapeDtypeStruct((t, RET_HEADS * RET_DV), BF16),
        compiler_params=_cparams(("parallel", "parallel")),
        name="retention",
    )(z, z, z, z, cos_t, sin_t, logg, gn_g)


def _xattn_kernel(h_ref, g_ref, wq_ref, kv_ref, wo_ref, o_ref):
    h = h_ref[...]
    hn = _rms(h, g_ref[...]).astype(BF16)
    q = jnp.dot(hn, wq_ref[...], preferred_element_type=F32) * (XA_HEAD_DIM ** -0.5)
    q = q.astype(BF16)
    width = XA_HEADS * XA_HEAD_DIM
    outs = []
    for hd in range(XA_HEADS):
        sl = slice(hd * XA_HEAD_DIM, (hd + 1) * XA_HEAD_DIM)
        k = kv_ref[:, sl]
        v = kv_ref[:, width + hd * XA_HEAD_DIM:width + (hd + 1) * XA_HEAD_DIM]
        s = lax.dot_general(q[:, sl], k, (((1,), (1,)), ((), ())), preferred_element_type=F32)
        m = jnp.max(s, axis=-1, keepdims=True)
        p = jnp.exp(s - m)
        l = jnp.sum(p, axis=-1, keepdims=True)
        o = jnp.dot(p.astype(BF16), v, preferred_element_type=F32) / l
        outs.append(o.astype(BF16))
    o_all = jnp.concatenate(outs, axis=-1)
    o_ref[...] = h + jnp.dot(o_all, wo_ref[...], preferred_element_type=F32)


def cross_attention(h, g, wq, memkv, wo, seq, tm=512):
    t, d = h.shape
    tm = min(tm, seq)
    per_b = seq // tm
    m = memkv.shape[0] // (t // seq)
    full = lambda a: pl.BlockSpec(a.shape, lambda i: (0, 0))
    return pl.pallas_call(
        _xattn_kernel,
        grid=(t // tm,),
        in_specs=[
            pl.BlockSpec((tm, d), lambda i: (i, 0)),
            pl.BlockSpec((1, d), lambda i: (0, 0)),
            full(wq),
            pl.BlockSpec((m, memkv.shape[1]), lambda i: (i // per_b, 0)),
            full(wo),
        ],
        out_specs=pl.BlockSpec((tm, d), lambda i: (i, 0)),
        out_shape=jax.ShapeDtypeStruct((t, d), F32),
        compiler_params=_cparams(("parallel",)),
        name="cross_attention",
    )(h, g.reshape(1, d), wq, memkv, wo)


def _top16(s, nrows):
    lanes = s.shape[1]
    rowi = lax.broadcasted_iota(I32, (nrows, lanes), 0).astype(F32)
    slot = lax.broadcasted_iota(I32, (PEER_TOPK, lanes), 0)
    vals = jnp.zeros((PEER_TOPK, lanes), F32)
    idxs = jnp.zeros((PEER_TOPK, lanes), F32)
    for r in range(PEER_TOPK):
        m = jnp.max(s, axis=0, keepdims=True)
        pos = jnp.min(jnp.where(s == m, rowi, float(nrows)), axis=0, keepdims=True)
        vals = jnp.where(slot == r, m, vals)
        idxs = jnp.where(slot == r, pos, idxs)
        s = jnp.where(rowi == pos, NEG_INF, s)
    return vals, idxs


def _pick_rows(table, sel):
    out = jnp.zeros(sel.shape, table.dtype)
    for i in range(PEER_TOPK):
        out = jnp.where(sel == float(i), table[i:i + 1, :], out)
    return out


def _peer_route_kernel(h_ref, g_ref, wqt_ref, keys_ref, hn_ref, e_ref, gate_ref,
                       sc_ref, esel_ref, gsel_ref, *, tm):
    hn = _rms(h_ref[...], g_ref[...]).astype(BF16)
    hn_ref[...] = hn
    q_t = lax.dot_general(wqt_ref[...], hn, (((1,), (1,)), ((), ())),
                          preferred_element_type=F32).astype(BF16)
    for grp in range(2 * PEER_HEADS):
        sc_ref[grp] = jnp.dot(keys_ref[grp], q_t[grp * PEER_HALF:(grp + 1) * PEER_HALF, :],
                              preferred_element_type=F32)

    n_col = tm // LANES

    def body(it, carry):
        head = it // n_col
        c0 = pl.multiple_of((it % n_col) * LANES, LANES)
        s1 = sc_ref[2 * head, :, pl.ds(c0, LANES)]
        s2 = sc_ref[2 * head + 1, :, pl.ds(c0, LANES)]
        v1, i1 = _top16(s1, PEER_NKEYS)
        v2, i2 = _top16(s2, PEER_NKEYS)
        cand = jnp.concatenate([v1[i:i + 1, :] + v2 for i in range(PEER_TOPK)], axis=0)
        best, flat = _top16(cand, PEER_TOPK * PEER_TOPK)
        hi = jnp.floor(flat * (1.0 / PEER_TOPK))
        a = _pick_rows(i1, hi)
        b = _pick_rows(i2, flat - hi * PEER_TOPK)
        p = jnp.exp(best - best[0:1, :])
        gates = p / jnp.sum(p, axis=0, keepdims=True)
        r0 = pl.multiple_of(head * PEER_TOPK, PEER_TOPK)
        esel_ref[pl.ds(r0, PEER_TOPK), pl.ds(c0, LANES)] = (a * PEER_NKEYS + b).astype(I32)
        gsel_ref[pl.ds(r0, PEER_TOPK), pl.ds(c0, LANES)] = gates
        return carry

    lax.fori_loop(0, PEER_HEADS * n_col, body, 0)

    for c in range(n_col):
        e_ref[c * LANES:(c + 1) * LANES, :] = esel_ref[:, c * LANES:(c + 1) * LANES].T
        gate_ref[c * LANES:(c + 1) * LANES, :] = gsel_ref[:, c * LANES:(c + 1) * LANES].T


def peer_route(h, g, wq_t, keys, tm=256):
    t, d = h.shape
    tm = min(tm, t)
    return pl.pallas_call(
        functools.partial(_peer_route_kernel, tm=tm),
        grid=(t // tm,),
        in_specs=[
            pl.BlockSpec((tm, d), lambda i: (i, 0)),
            pl.BlockSpec((1, d), lambda i: (0, 0)),
            pl.BlockSpec(wq_t.shape, lambda i: (0, 0)),
            pl.BlockSpec(keys.shape, lambda i: (0, 0, 0)),
        ],
        out_specs=[
            pl.BlockSpec((tm, d), lambda i: (i, 0)),
            pl.BlockSpec((tm, PEER_SEL), lambda i: (i, 0)),
            pl.BlockSpec((tm, PEER_SEL), lambda i: (i, 0)),
        ],
        out_shape=[
            jax.ShapeDtypeStruct((t, d), BF16),
            jax.ShapeDtypeStruct((t, PEER_SEL), I32),
            jax.ShapeDtypeStruct((t, PEER_SEL), F32),
        ],
        scratch_shapes=[
            pltpu.VMEM((2 * PEER_HEADS, PEER_NKEYS, tm), F32),
            pltpu.VMEM((PEER_SEL, tm), I32),
            pltpu.VMEM((PEER_SEL, tm), F32),
        ],
        compiler_params=_cparams(("parallel",)),
        name="peer_route",
    )(h, g.reshape(1, d), wq_t, keys)


W_GROUP = 16
W_PITCH = PEER_NKEYS + 8


def _peer_gates_kernel(e_ref, g_ref, w_ref, scr_ref, *, tw):
    sub = lax.broadcasted_iota(I32, (PEER_NKEYS, PEER_SEL), 0)

    def group(gi, carry):
        t0 = pl.multiple_of(gi * W_GROUP, W_GROUP)
        for i in range(W_GROUP):
            e_row = e_ref[pl.ds(t0 + i, 1), :]
            g_row = g_ref[pl.ds(t0 + i, 1), :]
            a_hot = jnp.where(sub == lax.shift_right_logical(e_row, 7), 1.0, 0.0).astype(BF16)
            b_hot = jnp.where(sub == (e_row & (PEER_NKEYS - 1)), g_row, 0.0).astype(BF16)
            w_tok = lax.dot_general(a_hot, b_hot, (((1,), (1,)), ((), ())),
                                    preferred_element_type=F32)
            scr_ref[i * W_PITCH:i * W_PITCH + PEER_NKEYS, :] = w_tok
        for c in range(PEER_NKEYS):
            rows = scr_ref[pl.ds(c, W_GROUP, stride=W_PITCH), :]
            w_ref[pl.ds(t0, W_GROUP), c * PEER_NKEYS:(c + 1) * PEER_NKEYS] = rows.astype(BF16)
        return carry

    lax.fori_loop(0, tw // W_GROUP, group, 0)


def peer_gates(e, g, tw=128):
    t = e.shape[0]
    tw = min(tw, t)
    n_exp = PEER_NKEYS * PEER_NKEYS
    return pl.pallas_call(
        functools.partial(_peer_gates_kernel, tw=tw),
        grid=(t // tw,),
        in_specs=[
            pl.BlockSpec((tw, PEER_SEL), lambda i: (i, 0)),
            pl.BlockSpec((tw, PEER_SEL), lambda i: (i, 0)),
        ],
        out_specs=pl.BlockSpec((tw, n_exp), lambda i: (i, 0)),
        out_shape=jax.ShapeDtypeStruct((t, n_exp), BF16),
        scratch_shapes=[pltpu.VMEM((W_GROUP * W_PITCH, PEER_NKEYS), F32)],
        compiler_params=_cparams(("parallel",)),
        name="peer_gates",
    )(e, g)


def _peer_dense_kernel(hn_ref, ut_ref, w_ref, v_ref, res_ref, fg_ref, o_ref, acc_ref, *, final_norm):
    j = pl.program_id(1)

    @pl.when(j == 0)
    def _():
        acc_ref[...] = jnp.zeros_like(acc_ref)

    pre = jnp.dot(hn_ref[...], ut_ref[...], preferred_element_type=F32)
    act = 0.5 * pre * (1.0 + lax.erf(pre * (2.0 ** -0.5)))
    weighted = (act * w_ref[...].astype(F32)).astype(BF16)
    acc_ref[...] += jnp.dot(weighted, v_ref[...], preferred_element_type=F32)

    @pl.when(j == pl.num_programs(1) - 1)
    def _():
        out = res_ref[...] + acc_ref[...]
        if final_norm:
            out = _rms(out, fg_ref[...])
        o_ref[...] = out


def peer_dense(hn, u_t, w, v, res, final_g, final_norm, tm=512, te=512):
    t, d = hn.shape
    n_exp = u_t.shape[1]
    tm = min(tm, t)
    return pl.pallas_call(
        functools.partial(_peer_dense_kernel, final_norm=final_norm),
        grid=(t // tm, n_exp // te),
        in_specs=[
            pl.BlockSpec((tm, d), lambda i, j: (i, 0)),
            pl.BlockSpec((d, te), lambda i, j: (0, j)),
            pl.BlockSpec((tm, te), lambda i, j: (i, j)),
            pl.BlockSpec((te, d), lambda i, j: (j, 0)),
            pl.BlockSpec((tm, d), lambda i, j: (i, 0)),
            pl.BlockSpec((1, d), lambda i, j: (0, 0)),
        ],
        out_specs=pl.BlockSpec((tm, d), lambda i, j: (i, 0)),
        out_shape=jax.ShapeDtypeStruct((t, d), F32),
        scratch_shapes=[pltpu.VMEM((tm, d), F32)],
        compiler_params=_cparams(("parallel", "arbitrary")),
        name="peer_dense",
    )(hn, u_t, w, v, res, final_g.reshape(1, d))


LRU_ROWS = 8


def _lru_kernel(gate_ref, rec_ref, cw_ref, cb_ref, wa_ref, ba_ref, wx_ref, bx_ref, lam_ref, o_ref,
                tail_ref, carry_ref, a_ref, u_ref, hs_ref, *, ts):
    @pl.when(pl.program_id(1) == 0)
    def _():
        tail_ref[...] = jnp.zeros_like(tail_ref)
        carry_ref[...] = jnp.zeros_like(carry_ref)

    x = rec_ref[...].astype(F32)
    xp = jnp.concatenate([tail_ref[...], x], axis=0)
    tail_ref[...] = x[ts - LRU_ROWS:, :]
    xc = cb_ref[...] + jnp.zeros_like(x)
    for j in range(CONV_WIDTH):
        off = LRU_ROWS - (CONV_WIDTH - 1) + j
        xc = xc + cw_ref[j:j + 1, :] * xp[off:off + ts, :]

    soft = jnp.log(1.0 + jnp.exp(-lam_ref[...]))
    for n in range(LRU_BLOCKS):
        sl = slice(n * LRU_BLOCK, (n + 1) * LRU_BLOCK)
        xb = xc[:, sl]
        xb16 = xb.astype(BF16)
        r = jax.nn.sigmoid(jnp.dot(xb16, wa_ref[n], preferred_element_type=F32) + ba_ref[:, sl])
        i = jax.nn.sigmoid(jnp.dot(xb16, wx_ref[n], preferred_element_type=F32) + bx_ref[:, sl])
        log_a = -LRU_C * r * soft[:, sl]
        a = jnp.exp(log_a)
        a_ref[:, sl] = a
        u_ref[:, sl] = jnp.sqrt(1.0 - a * a) * (i * xb)

    def step(c, h):
        r0 = pl.multiple_of(c * LRU_ROWS, LRU_ROWS)
        a8 = a_ref[pl.ds(r0, LRU_ROWS), :]
        u8 = u_ref[pl.ds(r0, LRU_ROWS), :]
        rows = []
        for r in range(LRU_ROWS):
            h = a8[r:r + 1, :] * h + u8[r:r + 1, :]
            rows.append(h)
        hs_ref[pl.ds(r0, LRU_ROWS), :] = jnp.concatenate(rows, axis=0)
        return h

    carry_ref[...] = lax.fori_loop(0, ts // LRU_ROWS, step, carry_ref[...])
    o_ref[...] = (jax.nn.gelu(gate_ref[...].astype(F32)) * hs_ref[...]).astype(BF16)


def lru_branch(z, conv_w, conv_b, wa, ba, wx, bx, lam, batch, seq, ts=256):
    t = z.shape[0]
    w = z.shape[1] // 2
    ts = min(ts, seq)
    per_b = seq // ts
    row = lambda a: pl.BlockSpec(a.shape, lambda b, s: (0, 0))
    cube = lambda a: pl.BlockSpec(a.shape, lambda b, s: (0, 0, 0))
    return pl.pallas_call(
        functools.partial(_lru_kernel, ts=ts),
        grid=(batch, per_b),
        in_specs=[
            pl.BlockSpec((ts, w), lambda b, s: (b * per_b + s, 0)),
            pl.BlockSpec((ts, w), lambda b, s: (b * per_b + s, 1)),
            row(conv_w), row(conv_b), cube(wa), row(ba), cube(wx), row(bx), row(lam),
        ],
        out_specs=pl.BlockSpec((ts, w), lambda b, s: (b * per_b + s, 0)),
        out_shape=jax.ShapeDtypeStruct((t, w), BF16),
        scratch_shapes=[
            pltpu.VMEM((LRU_ROWS, w), F32),
            pltpu.VMEM((1, w), F32),
            pltpu.VMEM((ts, w), F32),
            pltpu.VMEM((ts, w), F32),
            pltpu.VMEM((ts, w), F32),
        ],
        compiler_params=_cparams(("parallel", "arbitrary")),
        name="lru_branch",
    )(z, z, conv_w, conv_b, wa, ba, wx, bx, lam)


def _rope_tables(positions, dim, width):
    inv = ROPE_THETA ** (-jnp.arange(0, dim, 2, dtype=F32) / dim)
    ang = positions.astype(F32).reshape(-1, 1) * inv
    pad = jnp.zeros((ang.shape[0], width - dim), F32)
    cos = jnp.concatenate([jnp.cos(ang), jnp.cos(ang), pad], axis=-1)
    sin = jnp.concatenate([jnp.sin(ang), jnp.sin(ang), pad], axis=-1)
    return cos, sin


def _rotate_half_cols(w):
    half = w.shape[-1] // 2
    return jnp.concatenate([-w[..., half:], w[..., :half]], axis=-1)


def _even_weights(w_in, w_uq):
    d = w_in.shape[0]
    o = [0, 512, 768, 832]
    z64 = jnp.zeros((d, 64), F32)
    k_rope = w_in[:, o[2]:o[3]]
    w_in_p = jnp.concatenate(
        [w_in[:, :o[2]], k_rope, z64, _rotate_half_cols(k_rope), z64, w_in[:, o[3]:]], axis=-1)
    wq3 = w_uq.reshape(MLA_Q_RANK, MLA_HEADS, MLA_NOPE + MLA_ROPE)
    nope, rope = wq3[..., :MLA_NOPE], wq3[..., MLA_NOPE:]
    zq = jnp.zeros((MLA_Q_RANK, MLA_HEADS, 64), F32)
    wqa = jnp.concatenate([nope, rope, zq], axis=-1).reshape(MLA_Q_RANK, MLA_HEADS * MLA_QK)
    wqb = jnp.concatenate([_rotate_half_cols(rope), zq], axis=-1).reshape(MLA_Q_RANK, MLA_HEADS * LANES)
    return w_in_p.astype(BF16), wqa.astype(BF16), wqb.astype(BF16)


def _memory_and_peer(h, mem2, layer, seq, xa_norm_g, mem_norm_g, xa_wq, xa_wk, xa_wv, xa_wo,
                     ffn_norm_g, peer_wq, peer_sub_keys, peer_u, peer_v, final_norm_g, last):
    w_kv = jnp.concatenate([xa_wk[layer], xa_wv[layer]], axis=-1).astype(BF16)
    memkv = norm_matmul(mem2, mem_norm_g[layer], w_kv)
    h = cross_attention(h, xa_norm_g[layer], xa_wq[layer].astype(BF16), memkv,
                        xa_wo[layer].astype(BF16), seq)
    keys = peer_sub_keys[layer].reshape(2 * PEER_HEADS, PEER_NKEYS, PEER_HALF).astype(BF16)
    hn, e_sel, g_sel = peer_route(h, ffn_norm_g[layer], peer_wq[layer].T.astype(BF16), keys)
    w = peer_gates(e_sel, g_sel)
    return peer_dense(hn, peer_u[layer].T.astype(BF16), w, peer_v[layer].astype(BF16), h,
                      final_norm_g, last)


def kernel(x, mem, positions, mix_norm_g, xa_norm_g, ffn_norm_g, mem_norm_g, ev_w_in, ev_w_out, mla_q_norm_g, mla_w_uq, mla_kv_norm_g, mla_w_ukv, ret_gn_g, od_w_in, od_w_out, conv_w, conv_b, lru_wa, lru_ba, lru_wx, lru_bx, lru_lambda, xa_wq, xa_wk, xa_wv, xa_wo, peer_wq, peer_sub_keys, peer_u, peer_v, final_norm_g):
    batch, seq, d = x.shape
    t = batch * seq
    h = x.reshape(t, d)
    mem2 = mem.reshape(-1, d)
    tail = (xa_norm_g, mem_norm_g, xa_wq, xa_wk, xa_wv, xa_wo, ffn_norm_g, peer_wq, peer_sub_keys,
            peer_u, peer_v, final_norm_g)

    w_in_p, wqa, wqb = _even_weights(ev_w_in[0], mla_w_uq[0])
    z = norm_matmul(h, mix_norm_g[0], w_in_p)
    cos_m, sin_m = _rope_tables(positions, MLA_ROPE, LANES)
    qf, kvf, kr = mla_proj(z, cos_m, sin_m, mla_q_norm_g[0].reshape(1, -1),
                           mla_kv_norm_g[0].reshape(1, -1), wqa, wqb, mla_w_ukv[0].astype(BF16))
    attn = mla_attention(qf, kvf, kr, batch, seq)
    cos_r, sin_r = _rope_tables(positions, RET_DK, RET_DK)
    sign = jnp.concatenate([-jnp.ones((RET_DK // 2,), F32), jnp.ones((RET_DK // 2,), F32)])
    log_g = jnp.log(1.0 - 2.0 ** (-5.0 - jnp.arange(RET_HEADS, dtype=F32)))
    logg = jnp.broadcast_to(log_g[:, None, None], (RET_HEADS, 1, LANES))
    ret = retention(z, cos_r, sin_r * sign, logg, ret_gn_g[0].reshape(RET_HEADS, 1, RET_DV), batch, seq)
    w_out = ev_w_out[0].astype(BF16)
    n_a = MLA_HEADS * MLA_V
    h = matmul_res([attn, ret], [w_out[:n_a], w_out[n_a:]], h)
    h = _memory_and_peer(h, mem2, 0, seq, *tail, last=False)

    z = norm_matmul(h, mix_norm_g[1], od_w_in[0].astype(BF16))
    row = lambda a: a.reshape(1, -1)
    cw = jnp.concatenate([conv_w[0], jnp.zeros((8 - CONV_WIDTH, conv_w.shape[-1]), F32)], axis=0)
    y = lru_branch(z, cw, row(conv_b[0]), lru_wa[0].astype(BF16), row(lru_ba[0]),
                   lru_wx[0].astype(BF16), row(lru_bx[0]), row(lru_lambda[0]), batch, seq)
    h = matmul_res([y], [od_w_out[0].astype(BF16)], h)
    h = _memory_and_peer(h, mem2, 1, seq, *tail, last=True)
    return h.reshape(batch, seq, d)
```

```python
import functools
import math

import jax
import jax.numpy as jnp
from jax import lax
from jax.experimental import pallas as pl
from jax.experimental.pallas import tpu as pltpu

F32 = jnp.float32
BF16 = jnp.bfloat16
I32 = jnp.int32

EPS = 1e-6
ROPE_THETA = 10000.0

MLA_HEADS = 8
MLA_NOPE = 128
MLA_ROPE = 64
MLA_V = 128
MLA_Q_RANK = 512
MLA_KV_RANK = 256
MLA_QK = 256

RET_HEADS = 8
RET_DK = 128
RET_DV = 128
RET_BLOCK = 256

LRU_BLOCKS = 16
LRU_BLOCK = 128
CONV_WIDTH = 4
LRU_C = 8.0

XA_HEADS = 4
XA_HEAD_DIM = 128

PEER_HEADS = 8
PEER_NKEYS = 128
PEER_HALF = 128
PEER_TOPK = 16
PEER_SEL = PEER_HEADS * PEER_TOPK

LANES = 128
VMEM_LIMIT = 48 * 1024 * 1024

NEG_INF = float("-inf")
ID_SENTINEL = 1e9


def _cparams(sem, vmem=VMEM_LIMIT):
    return pltpu.CompilerParams(dimension_semantics=sem, vmem_limit_bytes=vmem)


def _rms(x, g):
    ms = jnp.mean(x * x, axis=-1, keepdims=True)
    return x * lax.rsqrt(ms + EPS) * g


def _norm_matmul_kernel(x_ref, g_ref, w_ref, o_ref, xn_ref):
    @pl.when(pl.program_id(1) == 0)
    def _():
        xn_ref[...] = _rms(x_ref[...], g_ref[...]).astype(BF16)

    o_ref[...] = jnp.dot(xn_ref[...], w_ref[...], preferred_element_type=F32).astype(o_ref.dtype)


def norm_matmul(x, g, w, tm=512, tn=1024):
    t, d = x.shape
    n = w.shape[1]
    tm = min(tm, t)
    tn = min(tn, n)
    return pl.pallas_call(
        _norm_matmul_kernel,
        grid=(t // tm, n // tn),
        in_specs=[
            pl.BlockSpec((tm, d), lambda i, j: (i, 0)),
            pl.BlockSpec((1, d), lambda i, j: (0, 0)),
            pl.BlockSpec((d, tn), lambda i, j: (0, j)),
        ],
        out_specs=pl.BlockSpec((tm, tn), lambda i, j: (i, j)),
        out_shape=jax.ShapeDtypeStruct((t, n), BF16),
        scratch_shapes=[pltpu.VMEM((tm, d), BF16)],
        compiler_params=_cparams(("parallel", "arbitrary")),
        name="norm_matmul",
    )(x, g.reshape(1, d), w)


def _matmul_res_kernel(*refs, n_in):
    xs = refs[:n_in]
    ws = refs[n_in:2 * n_in]
    res_ref = refs[2 * n_in]
    o_ref = refs[2 * n_in + 1]
    acc = res_ref[...]
    for x_ref, w_ref in zip(xs, ws):
        acc = acc + jnp.dot(x_ref[...], w_ref[...], preferred_element_type=F32)
    o_ref[...] = acc


def matmul_res(xs, ws, res, tm=512, tn=1024):
    t, n = res.shape
    tm = min(tm, t)
    tn = min(tn, n)
    n_in = len(xs)
    in_specs = [pl.BlockSpec((tm, x.shape[1]), lambda i, j: (i, 0)) for x in xs]
    in_specs += [pl.BlockSpec((w.shape[0], tn), lambda i, j: (0, j)) for w in ws]
    in_specs += [pl.BlockSpec((tm, tn), lambda i, j: (i, j))]
    return pl.pallas_call(
        functools.partial(_matmul_res_kernel, n_in=n_in),
        grid=(t // tm, n // tn),
        in_specs=in_specs,
        out_specs=pl.BlockSpec((tm, tn), lambda i, j: (i, j)),
        out_shape=jax.ShapeDtypeStruct((t, n), F32),
        compiler_params=_cparams(("parallel", "parallel")),
        name="matmul_res",
    )(*xs, *ws, res)


def _mla_proj_kernel(z_ref, cos_ref, sin_ref, qg_ref, kvg_ref, wqa_ref, wqb_ref, wkv_ref,
                     q_ref, kv_ref, kr_ref):
    scale = (MLA_NOPE + MLA_ROPE) ** -0.5
    z = z_ref[...]
    cq = _rms(z[:, :MLA_Q_RANK].astype(F32), qg_ref[...]).astype(BF16)
    ckv = _rms(z[:, MLA_Q_RANK:MLA_Q_RANK + MLA_KV_RANK].astype(F32), kvg_ref[...]).astype(BF16)
    cos = cos_ref[...]
    sin = sin_ref[...]
    qa = jnp.dot(cq, wqa_ref[...], preferred_element_type=F32)
    qb = jnp.dot(cq, wqb_ref[...], preferred_element_type=F32)
    for h in range(MLA_HEADS):
        base = h * MLA_QK
        q_ref[:, base:base + LANES] = (qa[:, base:base + LANES] * scale).astype(BF16)
        rot = qa[:, base + LANES:base + 2 * LANES] * cos + qb[:, h * LANES:(h + 1) * LANES] * sin
        q_ref[:, base + LANES:base + 2 * LANES] = (rot * scale).astype(BF16)
    kv_ref[...] = jnp.dot(ckv, wkv_ref[...], preferred_element_type=F32).astype(BF16)
    off = MLA_Q_RANK + MLA_KV_RANK
    kr = z[:, off:off + LANES].astype(F32) * cos + z[:, off + LANES:off + 2 * LANES].astype(F32) * sin
    kr_ref[...] = kr.astype(BF16)


def mla_proj(z, cos_t, sin_t, qg, kvg, wqa, wqb, wkv, tm=512):
    t = z.shape[0]
    tm = min(tm, t)
    hq = MLA_HEADS * MLA_QK
    full = lambda a: pl.BlockSpec(a.shape, lambda i: (0, 0))
    return pl.pallas_call(
        _mla_proj_kernel,
        grid=(t // tm,),
        in_specs=[
            pl.BlockSpec((tm, 1024), lambda i: (i, 0)),
            pl.BlockSpec((tm, LANES), lambda i: (i, 0)),
            pl.BlockSpec((tm, LANES), lambda i: (i, 0)),
            full(qg), full(kvg), full(wqa), full(wqb), full(wkv),
        ],
        out_specs=[
            pl.BlockSpec((tm, hq), lambda i: (i, 0)),
            pl.BlockSpec((tm, hq), lambda i: (i, 0)),
            pl.BlockSpec((tm, LANES), lambda i: (i, 0)),
        ],
        out_shape=[
            jax.ShapeDtypeStruct((t, hq), BF16),
            jax.ShapeDtypeStruct((t, hq), BF16),
            jax.ShapeDtypeStruct((t, LANES), BF16),
        ],
        compiler_params=_cparams(("parallel",)),
        name="mla_proj",
    )(z, cos_t, sin_t, qg, kvg, wqa, wqb, wkv)


def _mla_attn_kernel(q_ref, kv_ref, kr_ref, o_ref, kcat_ref, *, seq, tq):
    kcat_ref[:, :LANES] = kv_ref[:, :LANES]
    kcat_ref[:, LANES:] = kr_ref[...]
    for qi in range(seq // tq):
        kend = (qi + 1) * tq
        q = q_ref[qi * tq:(qi + 1) * tq, :]
        s = lax.dot_general(q, kcat_ref[:kend, :], (((1,), (1,)), ((), ())),
                            preferred_element_type=F32)
        row = lax.broadcasted_iota(I32, (tq, kend), 0) + qi * tq
        col = lax.broadcasted_iota(I32, (tq, kend), 1)
        s = jnp.where(col <= row, s, NEG_INF)
        m = jnp.max(s, axis=-1, keepdims=True)
        p = jnp.exp(s - m)
        l = jnp.sum(p, axis=-1, keepdims=True)
        o = jnp.dot(p.astype(BF16), kv_ref[:kend, LANES:], preferred_element_type=F32)
        o_ref[qi * tq:(qi + 1) * tq, :] = (o / l).astype(BF16)


def mla_attention(qf, kvf, kr, batch, seq, tq=256):
    tq = min(tq, seq)
    t = batch * seq
    return pl.pallas_call(
        functools.partial(_mla_attn_kernel, seq=seq, tq=tq),
        grid=(batch, MLA_HEADS),
        in_specs=[
            pl.BlockSpec((seq, MLA_QK), lambda b, h: (b, h)),
            pl.BlockSpec((seq, MLA_QK), lambda b, h: (b, h)),
            pl.BlockSpec((seq, LANES), lambda b, h: (b, 0)),
        ],
        out_specs=pl.BlockSpec((seq, MLA_V), lambda b, h: (b, h)),
        out_shape=jax.ShapeDtypeStruct((t, MLA_HEADS * MLA_V), BF16),
        scratch_shapes=[pltpu.VMEM((seq, MLA_QK), BF16)],
        compiler_params=_cparams(("parallel", "parallel")),
        name="mla_attention",
    )(qf, kvf, kr)


def _retention_kernel(q_ref, k_ref, v_ref, g_ref, cos_ref, sin_ref, logg_ref, gn_ref, o_ref,
                      *, seq, blk):
    lg = logg_ref[...]
    cos = cos_ref[...]
    sin = sin_ref[...]
    half = RET_DK // 2

    def rope(x):
        return x * cos + pltpu.roll(x, half, 1) * sin

    q_all = rope(q_ref[...].astype(F32))
    k_all = rope(k_ref[...].astype(F32)) * (RET_DK ** -0.5)

    ri = lax.broadcasted_iota(I32, (blk, blk), 0)
    ci = lax.broadcasted_iota(I32, (blk, blk), 1)
    diff = (ri - ci).astype(F32)
    decay = jnp.where(diff >= 0, jnp.exp(lg[:, :1] * jnp.maximum(diff, 0.0)), 0.0)
    pos = lax.broadcasted_iota(I32, (blk, RET_DK), 0).astype(F32)
    xi = jnp.exp(lg * (pos + 1.0))
    zeta = jnp.exp(lg * (blk - 1.0 - pos))
    blk_decay = jnp.exp(lg * float(blk))

    state = jnp.zeros((RET_DK, RET_DV), F32)
    for n in range(seq // blk):
        sl = slice(n * blk, (n + 1) * blk)
        qc = q_all[sl]
        kc = k_all[sl]
        vc = v_ref[sl, :]
        att = lax.dot_general(qc.astype(BF16), kc.astype(BF16), (((1,), (1,)), ((), ())),
                              preferred_element_type=F32) * decay
        inner = jnp.dot(att.astype(BF16), vc, preferred_element_type=F32)
        cross = jnp.dot((qc * xi).astype(BF16), state.astype(BF16), preferred_element_type=F32)
        ret = inner + cross
        kz_t = (kc * zeta).T.astype(BF16)
        dstate = jnp.dot(kz_t, vc, preferred_element_type=F32)
        state = state * blk_decay + dstate
        mu = jnp.mean(ret, axis=-1, keepdims=True)
        cen = ret - mu
        var = jnp.mean(cen * cen, axis=-1, keepdims=True)
        normed = cen * lax.rsqrt(var + EPS) * gn_ref[...]
        gate = g_ref[sl, :].astype(F32)
        o_ref[sl, :] = (gate * jax.nn.sigmoid(gate) * normed).astype(BF16)


def retention(z, cos_t, sin_t, logg, gn_g, batch, seq):
    blk = min(RET_BLOCK, seq)
    t = batch * seq
    first = 1024 // LANES
    col = lambda part: (lambda b, h: (b, first + part * RET_HEADS + h))
    return pl.pallas_call(
        functools.partial(_retention_kernel, seq=seq, blk=blk),
        grid=(batch, RET_HEADS),
        in_specs=[
            pl.BlockSpec((seq, LANES), col(0)),
            pl.BlockSpec((seq, LANES), col(1)),
            pl.BlockSpec((seq, LANES), col(2)),
            pl.BlockSpec((seq, LANES), col(3)),
            pl.BlockSpec((seq, LANES), lambda b, h: (b, 0)),
            pl.BlockSpec((seq, LANES), lambda b, h: (b, 0)),
            pl.BlockSpec((None, 1, LANES), lambda b, h: (h, 0, 0)),
            pl.BlockSpec((None, 1, LANES), lambda b, h: (h, 0, 0)),
        ],
        out_specs=pl.BlockSpec((seq, RET_DV), lambda b, h: (b, h)),
        out_shape=jax.ShapeDtypeStruct((t, RET_HEADS * RET_DV), BF16),
        compiler_params=_cparams(("parallel", "parallel")),
        name="retention",
    )(z, z, z, z, cos_t, sin_t, logg, gn_g)


def _xattn_kernel(h_ref, g_ref, wq_ref, kv_ref, wo_ref, o_ref):
    h = h_ref[...]
    hn = _rms(h, g_ref[...]).astype(BF16)
    q = jnp.dot(hn, wq_ref[...], preferred_element_type=F32) * (XA_HEAD_DIM ** -0.5)
    q = q.astype(BF16)
    width = XA_HEADS * XA_HEAD_DIM
    outs = []
    for hd in range(XA_HEADS):
        sl = slice(hd * XA_HEAD_DIM, (hd + 1) * XA_HEAD_DIM)
        k = kv_ref[:, sl]
        v = kv_ref[:, width + hd * XA_HEAD_DIM:width + (hd + 1) * XA_HEAD_DIM]
        s = lax.dot_general(q[:, sl], k, (((1,), (1,)), ((), ())), preferred_element_type=F32)
        m = jnp.max(s, axis=-1, keepdims=True)
        p = jnp.exp(s - m)
        l = jnp.sum(p, axis=-1, keepdims=True)
        o = jnp.dot(p.astype(BF16), v, preferred_element_type=F32) / l
        outs.append(o.astype(BF16))
    o_all = jnp.concatenate(outs, axis=-1)
    o_ref[...] = h + jnp.dot(o_all, wo_ref[...], preferred_element_type=F32)


def cross_attention(h, g, wq, memkv, wo, seq, tm=512):
    t, d = h.shape
    tm = min(tm, seq)
    per_b = seq // tm
    m = memkv.shape[0] // (t // seq)
    full = lambda a: pl.BlockSpec(a.shape, lambda i: (0, 0))
    return pl.pallas_call(
        _xattn_kernel,
        grid=(t // tm,),
        in_specs=[
            pl.BlockSpec((tm, d), lambda i: (i, 0)),
            pl.BlockSpec((1, d), lambda i: (0, 0)),
            full(wq),
            pl.BlockSpec((m, memkv.shape[1]), lambda i: (i // per_b, 0)),
            full(wo),
        ],
        out_specs=pl.BlockSpec((tm, d), lambda i: (i, 0)),
        out_shape=jax.ShapeDtypeStruct((t, d), F32),
        compiler_params=_cparams(("parallel",)),
        name="cross_attention",
    )(h, g.reshape(1, d), wq, memkv, wo)


SC_PITCH = PEER_NKEYS + 8
SUBLANES = 8
TOPK_TOKENS = SUBLANES * LANES


def _peer_scores_kernel(h_ref, g_ref, wqt_ref, keys_ref, hn_ref, sc_ref, *, tm):
    hn = _rms(h_ref[...], g_ref[...]).astype(BF16)
    hn_ref[...] = hn
    q_t = lax.dot_general(wqt_ref[...], hn, (((1,), (1,)), ((), ())),
                          preferred_element_type=F32).astype(BF16)
    pad = jnp.zeros((SC_PITCH - PEER_NKEYS, LANES), F32)
    for grp in range(2 * PEER_HEADS):
        s = jnp.dot(keys_ref[grp], q_t[grp * PEER_HALF:(grp + 1) * PEER_HALF, :],
                    preferred_element_type=F32)
        for c in range(tm // LANES):
            sc_ref[grp, c * SC_PITCH:c * SC_PITCH + PEER_NKEYS, :] = s[:, c * LANES:(c + 1) * LANES]
            sc_ref[grp, c * SC_PITCH + PEER_NKEYS:(c + 1) * SC_PITCH, :] = pad


def peer_scores(h, g, wq_t, keys, tm=256):
    t, d = h.shape
    tm = min(tm, t)
    rows = (tm // LANES) * SC_PITCH
    return pl.pallas_call(
        functools.partial(_peer_scores_kernel, tm=tm),
        grid=(t // tm,),
        in_specs=[
            pl.BlockSpec((tm, d), lambda i: (i, 0)),
            pl.BlockSpec((1, d), lambda i: (0, 0)),
            pl.BlockSpec(wq_t.shape, lambda i: (0, 0)),
            pl.BlockSpec(keys.shape, lambda i: (0, 0, 0)),
        ],
        out_specs=[
            pl.BlockSpec((tm, d), lambda i: (i, 0)),
            pl.BlockSpec((2 * PEER_HEADS, rows, LANES), lambda i: (0, i, 0)),
        ],
        out_shape=[
            jax.ShapeDtypeStruct((t, d), BF16),
            jax.ShapeDtypeStruct((2 * PEER_HEADS, (t // LANES) * SC_PITCH, LANES), F32),
        ],
        compiler_params=_cparams(("parallel",)),
        name="peer_scores",
    )(h, g.reshape(1, d), wq_t, keys)


def _tree(op, xs):
    xs = list(xs)
    while len(xs) > 1:
        xs = [op(xs[i], xs[i + 1]) if i + 1 < len(xs) else xs[i] for i in range(0, len(xs), 2)]
    return xs[0]


def _top16(rows, ids):
    vals, idxs = [], []
    for _ in range(PEER_TOPK):
        m = _tree(jnp.maximum, rows)
        pos = _tree(jnp.minimum, [jnp.where(x == m, float(i), ID_SENTINEL) for x, i in zip(rows, ids)])
        rows = [jnp.where(pos == float(i), NEG_INF, x) for x, i in zip(rows, ids)]
        vals.append(m)
        idxs.append(pos)
    return vals, idxs


def _pick(table, sel):
    out = table[0]
    for i in range(1, PEER_TOPK):
        out = jnp.where(sel == float(i), table[i], out)
    return out


_PAIRS = [(i, j) for i in range(PEER_TOPK) for j in range(PEER_TOPK) if (i + 1) * (j + 1) <= PEER_TOPK]


def _route_head(s1, s2):
    keys = list(range(PEER_NKEYS))
    v1, i1 = _top16(s1, keys)
    v2, i2 = _top16(s2, keys)
    best, flat = _top16([v1[i] + v2[j] for i, j in _PAIRS], [i * PEER_TOPK + j for i, j in _PAIRS])
    probs = [jnp.exp(b - best[0]) for b in best]
    inv = 1.0 / _tree(jnp.add, probs)
    experts = []
    for r in range(PEER_TOPK):
        hi = jnp.floor(flat[r] * (1.0 / PEER_TOPK))
        experts.append(_pick(i1, hi) * PEER_NKEYS + _pick(i2, flat[r] - hi * PEER_TOPK))
    return experts, [p * inv for p in probs]


def _peer_topk_kernel(sc_ref, e_ref, gate_ref, esel_ref, gsel_ref):
    head = pl.program_id(1)

    def load(grp):
        return [sc_ref[grp, pl.ds(k, SUBLANES, stride=SC_PITCH), :] for k in range(PEER_NKEYS)]

    experts, gates = _route_head(load(0), load(1))
    for r in range(PEER_TOPK):
        row = pl.multiple_of((head * PEER_TOPK + r) * SUBLANES, SUBLANES)
        esel_ref[pl.ds(row, SUBLANES), :] = experts[r]
        gsel_ref[pl.ds(row, SUBLANES), :] = gates[r]

    @pl.when(head == PEER_HEADS - 1)
    def _():
        for s in range(SUBLANES):
            e_sel = esel_ref[pl.ds(s, PEER_SEL, stride=SUBLANES), :]
            g_sel = gsel_ref[pl.ds(s, PEER_SEL, stride=SUBLANES), :]
            e_ref[s * LANES:(s + 1) * LANES, :] = e_sel.T.astype(I32)
            gate_ref[s * LANES:(s + 1) * LANES, :] = g_sel.T


def peer_topk(scores):
    t = scores.shape[1] // SC_PITCH * LANES
    rows = SUBLANES * SC_PITCH
    return pl.pallas_call(
        _peer_topk_kernel,
        grid=(t // TOPK_TOKENS, PEER_HEADS),
        in_specs=[pl.BlockSpec((2, rows, LANES), lambda i, hd: (hd, i, 0))],
        out_specs=[
            pl.BlockSpec((TOPK_TOKENS, PEER_SEL), lambda i, hd: (i, 0)),
            pl.BlockSpec((TOPK_TOKENS, PEER_SEL), lambda i, hd: (i, 0)),
        ],
        out_shape=[
            jax.ShapeDtypeStruct((t, PEER_SEL), I32),
            jax.ShapeDtypeStruct((t, PEER_SEL), F32),
        ],
        scratch_shapes=[
            pltpu.VMEM((PEER_SEL * SUBLANES, LANES), F32),
            pltpu.VMEM((PEER_SEL * SUBLANES, LANES), F32),
        ],
        compiler_params=_cparams(("parallel", "arbitrary")),
        name="peer_topk",
    )(scores)


W_GROUP = 16
W_PITCH = PEER_NKEYS + 8


def _peer_gates_kernel(e_ref, g_ref, w_ref, scr_ref, *, tw):
    sub = lax.broadcasted_iota(I32, (PEER_NKEYS, PEER_SEL), 0)

    def group(gi, carry):
        t0 = pl.multiple_of(gi * W_GROUP, W_GROUP)
        for i in range(W_GROUP):
            e_row = e_ref[pl.ds(t0 + i, 1), :]
            g_row = g_ref[pl.ds(t0 + i, 1), :]
            a_hot = jnp.where(sub == lax.shift_right_logical(e_row, 7), 1.0, 0.0).astype(BF16)
            b_hot = jnp.where(sub == (e_row & (PEER_NKEYS - 1)), g_row, 0.0).astype(BF16)
            w_tok = lax.dot_general(a_hot, b_hot, (((1,), (1,)), ((), ())),
                                    preferred_element_type=F32)
            scr_ref[i * W_PITCH:i * W_PITCH + PEER_NKEYS, :] = w_tok
        for c in range(PEER_NKEYS):
            rows = scr_ref[pl.ds(c, W_GROUP, stride=W_PITCH), :]
            w_ref[pl.ds(t0, W_GROUP), c * PEER_NKEYS:(c + 1) * PEER_NKEYS] = rows.astype(BF16)
        return carry

    lax.fori_loop(0, tw // W_GROUP, group, 0)


def peer_gates(e, g, tw=128):
    t = e.shape[0]
    tw = min(tw, t)
    n_exp = PEER_NKEYS * PEER_NKEYS
    return pl.pallas_call(
        functools.partial(_peer_gates_kernel, tw=tw),
        grid=(t // tw,),
        in_specs=[
            pl.BlockSpec((tw, PEER_SEL), lambda i: (i, 0)),
            pl.BlockSpec((tw, PEER_SEL), lambda i: (i, 0)),
        ],
        out_specs=pl.BlockSpec((tw, n_exp), lambda i: (i, 0)),
        out_shape=jax.ShapeDtypeStruct((t, n_exp), BF16),
        scratch_shapes=[pltpu.VMEM((W_GROUP * W_PITCH, PEER_NKEYS), F32)],
        compiler_params=_cparams(("parallel",)),
        name="peer_gates",
    )(e, g)


def _peer_dense_kernel(hn_ref, ut_ref, w_ref, v_ref, res_ref, fg_ref, o_ref, acc_ref, *, final_norm):
    j = pl.program_id(1)

    @pl.when(j == 0)
    def _():
        acc_ref[...] = jnp.zeros_like(acc_ref)

    hn = hn_ref[...]
    te = w_ref.shape[1]
    parts = []
    for c in range(te // DENSE_SUB):
        sl = slice(c * DENSE_SUB, (c + 1) * DENSE_SUB)
        pre = jnp.dot(hn, ut_ref[:, sl], preferred_element_type=F32)
        act = 0.5 * pre * (1.0 + lax.erf(pre * (2.0 ** -0.5)))
        parts.append((act * w_ref[:, sl].astype(F32)).astype(BF16))
    weighted = jnp.concatenate(parts, axis=1)
    acc_ref[...] += jnp.dot(weighted, v_ref[...], preferred_element_type=F32)

    @pl.when(j == pl.num_programs(1) - 1)
    def _():
        out = res_ref[...] + acc_ref[...]
        if final_norm:
            out = _rms(out, fg_ref[...])
        o_ref[...] = out


DENSE_SUB = 512
DENSE_VMEM_LIMIT = 58 * 1024 * 1024


def peer_dense(hn, u_t, w, v, res, final_g, final_norm, tm=512, te=1024):
    t, d = hn.shape
    n_exp = u_t.shape[1]
    tm = min(tm, t)
    return pl.pallas_call(
        functools.partial(_peer_dense_kernel, final_norm=final_norm),
        grid=(t // tm, n_exp // te),
        in_specs=[
            pl.BlockSpec((tm, d), lambda i, j: (i, 0)),
            pl.BlockSpec((d, te), lambda i, j: (0, j)),
            pl.BlockSpec((tm, te), lambda i, j: (i, j)),
            pl.BlockSpec((te, d), lambda i, j: (j, 0)),
            pl.BlockSpec((tm, d), lambda i, j: (i, 0)),
            pl.BlockSpec((1, d), lambda i, j: (0, 0)),
        ],
        out_specs=pl.BlockSpec((tm, d), lambda i, j: (i, 0)),
        out_shape=jax.ShapeDtypeStruct((t, d), F32),
        scratch_shapes=[pltpu.VMEM((tm, d), F32)],
        compiler_params=_cparams(("parallel", "arbitrary"), DENSE_VMEM_LIMIT),
        name="peer_dense",
    )(hn, u_t, w, v, res, final_g.reshape(1, d))


LRU_ROWS = 8


def _lru_kernel(gate_ref, rec_ref, cw_ref, cb_ref, wa_ref, ba_ref, wx_ref, bx_ref, lam_ref, o_ref,
                tail_ref, carry_ref, a_ref, u_ref, hs_ref, *, ts):
    @pl.when(pl.program_id(1) == 0)
    def _():
        tail_ref[...] = jnp.zeros_like(tail_ref)
        carry_ref[...] = jnp.zeros_like(carry_ref)

    x = rec_ref[...].astype(F32)
    xp = jnp.concatenate([tail_ref[...], x], axis=0)
    tail_ref[...] = x[ts - LRU_ROWS:, :]
    xc = cb_ref[...] + jnp.zeros_like(x)
    for j in range(CONV_WIDTH):
        off = LRU_ROWS - (CONV_WIDTH - 1) + j
        xc = xc + cw_ref[j:j + 1, :] * xp[off:off + ts, :]

    soft = jnp.log(1.0 + jnp.exp(-lam_ref[...]))
    for n in range(LRU_BLOCKS):
        sl = slice(n * LRU_BLOCK, (n + 1) * LRU_BLOCK)
        xb = xc[:, sl]
        xb16 = xb.astype(BF16)
        r = jax.nn.sigmoid(jnp.dot(xb16, wa_ref[n], preferred_element_type=F32) + ba_ref[:, sl])
        i = jax.nn.sigmoid(jnp.dot(xb16, wx_ref[n], preferred_element_type=F32) + bx_ref[:, sl])
        log_a = -LRU_C * r * soft[:, sl]
        a = jnp.exp(log_a)
        a_ref[:, sl] = a
        u_ref[:, sl] = jnp.sqrt(1.0 - a * a) * (i * xb)

    def step(c, h):
        r0 = pl.multiple_of(c * LRU_ROWS, LRU_ROWS)
        a8 = a_ref[pl.ds(r0, LRU_ROWS), :]
        u8 = u_ref[pl.ds(r0, LRU_ROWS), :]
        rows = []
        for r in range(LRU_ROWS):
            h = a8[r:r + 1, :] * h + u8[r:r + 1, :]
            rows.append(h)
        hs_ref[pl.ds(r0, LRU_ROWS), :] = jnp.concatenate(rows, axis=0)
        return h

    carry_ref[...] = lax.fori_loop(0, ts // LRU_ROWS, step, carry_ref[...])
    o_ref[...] = (jax.nn.gelu(gate_ref[...].astype(F32)) * hs_ref[...]).astype(BF16)


def lru_branch(z, conv_w, conv_b, wa, ba, wx, bx, lam, batch, seq, ts=256):
    t = z.shape[0]
    w = z.shape[1] // 2
    ts = min(ts, seq)
    per_b = seq // ts
    row = lambda a: pl.BlockSpec(a.shape, lambda b, s: (0, 0))
    cube = lambda a: pl.BlockSpec(a.shape, lambda b, s: (0, 0, 0))
    return pl.pallas_call(
        functools.partial(_lru_kernel, ts=ts),
        grid=(batch, per_b),
        in_specs=[
            pl.BlockSpec((ts, w), lambda b, s: (b * per_b + s, 0)),
            pl.BlockSpec((ts, w), lambda b, s: (b * per_b + s, 1)),
            row(conv_w), row(conv_b), cube(wa), row(ba), cube(wx), row(bx), row(lam),
        ],
        out_specs=pl.BlockSpec((ts, w), lambda b, s: (b * per_b + s, 0)),
        out_shape=jax.ShapeDtypeStruct((t, w), BF16),
        scratch_shapes=[
            pltpu.VMEM((LRU_ROWS, w), F32),
            pltpu.VMEM((1, w), F32),
            pltpu.VMEM((ts, w), F32),
            pltpu.VMEM((ts, w), F32),
            pltpu.VMEM((ts, w), F32),
        ],
        compiler_params=_cparams(("parallel", "arbitrary")),
        name="lru_branch",
    )(z, z, conv_w, conv_b, wa, ba, wx, bx, lam)


def _rope_tables(positions, dim, width):
    inv = ROPE_THETA ** (-jnp.arange(0, dim, 2, dtype=F32) / dim)
    ang = positions.astype(F32).reshape(-1, 1) * inv
    pad = jnp.zeros((ang.shape[0], width - dim), F32)
    cos = jnp.concatenate([jnp.cos(ang), jnp.cos(ang), pad], axis=-1)
    sin = jnp.concatenate([jnp.sin(ang), jnp.sin(ang), pad], axis=-1)
    return cos, sin


def _rotate_half_cols(w):
    half = w.shape[-1] // 2
    return jnp.concatenate([-w[..., half:], w[..., :half]], axis=-1)


def _even_weights(w_in, w_uq):
    d = w_in.shape[0]
    o = [0, 512, 768, 832]
    z64 = jnp.zeros((d, 64), F32)
    k_rope = w_in[:, o[2]:o[3]]
    w_in_p = jnp.concatenate(
        [w_in[:, :o[2]], k_rope, z64, _rotate_half_cols(k_rope), z64, w_in[:, o[3]:]], axis=-1)
    wq3 = w_uq.reshape(MLA_Q_RANK, MLA_HEADS, MLA_NOPE + MLA_ROPE)
    nope, rope = wq3[..., :MLA_NOPE], wq3[..., MLA_NOPE:]
    zq = jnp.zeros((MLA_Q_RANK, MLA_HEADS, 64), F32)
    wqa = jnp.concatenate([nope, rope, zq], axis=-1).reshape(MLA_Q_RANK, MLA_HEADS * MLA_QK)
    wqb = jnp.concatenate([_rotate_half_cols(rope), zq], axis=-1).reshape(MLA_Q_RANK, MLA_HEADS * LANES)
    return w_in_p.astype(BF16), wqa.astype(BF16), wqb.astype(BF16)


def _memory_and_peer(h, mem2, layer, seq, xa_norm_g, mem_norm_g, xa_wq, xa_wk, xa_wv, xa_wo,
                     ffn_norm_g, peer_wq, peer_sub_keys, peer_u, peer_v, final_norm_g, last):
    w_kv = jnp.concatenate([xa_wk[layer], xa_wv[layer]], axis=-1).astype(BF16)
    memkv = norm_matmul(mem2, mem_norm_g[layer], w_kv)
    h = cross_attention(h, xa_norm_g[layer], xa_wq[layer].astype(BF16), memkv,
                        xa_wo[layer].astype(BF16), seq)
    keys = peer_sub_keys[layer].reshape(2 * PEER_HEADS, PEER_NKEYS, PEER_HALF).astype(BF16)
    hn, scores = peer_scores(h, ffn_norm_g[layer], peer_wq[layer].T.astype(BF16), keys)
    e_sel, g_sel = peer_topk(scores)
    w = peer_gates(e_sel, g_sel)
    return peer_dense(hn, peer_u[layer].T.astype(BF16), w, peer_v[layer].astype(BF16), h,
                      final_norm_g, last)


def kernel(x, mem, positions, mix_norm_g, xa_norm_g, ffn_norm_g, mem_norm_g, ev_w_in, ev_w_out, mla_q_norm_g, mla_w_uq, mla_kv_norm_g, mla_w_ukv, ret_gn_g, od_w_in, od_w_out, conv_w, conv_b, lru_wa, lru_ba, lru_wx, lru_bx, lru_lambda, xa_wq, xa_wk, xa_wv, xa_wo, peer_wq, peer_sub_keys, peer_u, peer_v, final_norm_g):
    batch, seq, d = x.shape
    t = batch * seq
    h = x.reshape(t, d)
    mem2 = mem.reshape(-1, d)
    tail = (xa_norm_g, mem_norm_g, xa_wq, xa_wk, xa_wv, xa_wo, ffn_norm_g, peer_wq, peer_sub_keys,
            peer_u, peer_v, final_norm_g)

    w_in_p, wqa, wqb = _even_weights(ev_w_in[0], mla_w_uq[0])
    z = norm_matmul(h, mix_norm_g[0], w_in_p)
    cos_m, sin_m = _rope_tables(positions, MLA_ROPE, LANES)
    qf, kvf, kr = mla_proj(z, cos_m, sin_m, mla_q_norm_g[0].reshape(1, -1),
                           mla_kv_norm_g[0].reshape(1, -1), wqa, wqb, mla_w_ukv[0].astype(BF16))
    attn = mla_attention(qf, kvf, kr, batch, seq)
    cos_r, sin_r = _rope_tables(positions, RET_DK, RET_DK)
    sign = jnp.concatenate([-jnp.ones((RET_DK // 2,), F32), jnp.ones((RET_DK // 2,), F32)])
    log_g = jnp.log(1.0 - 2.0 ** (-5.0 - jnp.arange(RET_HEADS, dtype=F32)))
    logg = jnp.broadcast_to(log_g[:, None, None], (RET_HEADS, 1, LANES))
    ret = retention(z, cos_r, sin_r * sign, logg, ret_gn_g[0].reshape(RET_HEADS, 1, RET_DV), batch, seq)
    w_out = ev_w_out[0].astype(BF16)
    n_a = MLA_HEADS * MLA_V
    h = matmul_res([attn, ret], [w_out[:n_a], w_out[n_a:]], h)
    h = _memory_and_peer(h, mem2, 0, seq, *tail, last=False)

    z = norm_matmul(h, mix_norm_g[1], od_w_in[0].astype(BF16))
    row = lambda a: a.reshape(1, -1)
    cw = jnp.concatenate([conv_w[0], jnp.zeros((8 - CONV_WIDTH, conv_w.shape[-1]), F32)], axis=0)
    y = lru_branch(z, cw, row(conv_b[0]), lru_wa[0].astype(BF16), row(lru_ba[0]),
                   lru_wx[0].astype(BF16), row(lru_bx[0]), row(lru_lambda[0]), batch, seq)
    h = matmul_res([y], [od_w_out[0].astype(BF16)], h)
    h = _memory_and_peer(h, mem2, 1, seq, *tail, last=True)
    return h.reshape(batch, seq, d)
```

```python
import functools
import math

import jax
import jax.numpy as jnp
import numpy as np
from jax import lax
from jax.experimental import pallas as pl
from jax.experimental.pallas import tpu as pltpu

F32 = jnp.float32
BF16 = jnp.bfloat16
I32 = jnp.int32

EPS = 1e-6
ROPE_THETA = 10000.0

MLA_HEADS = 8
MLA_NOPE = 128
MLA_ROPE = 64
MLA_V = 128
MLA_Q_RANK = 512
MLA_KV_RANK = 256
MLA_QK = 256

RET_HEADS = 8
RET_DK = 128
RET_DV = 128
RET_BLOCK = 256

LRU_BLOCKS = 16
LRU_BLOCK = 128
CONV_WIDTH = 4
LRU_C = 8.0

XA_HEADS = 4
XA_HEAD_DIM = 128

PEER_HEADS = 8
PEER_NKEYS = 128
PEER_HALF = 128
PEER_TOPK = 16
PEER_SEL = PEER_HEADS * PEER_TOPK

LANES = 128
VMEM_LIMIT = 48 * 1024 * 1024

NEG_INF = float("-inf")


def _cparams(sem, vmem=VMEM_LIMIT):
    return pltpu.CompilerParams(dimension_semantics=sem, vmem_limit_bytes=vmem)


def _rms(x, g):
    ms = jnp.mean(x * x, axis=-1, keepdims=True)
    return x * lax.rsqrt(ms + EPS) * g


def _norm_matmul_kernel(*refs, n_x):
    x_refs = refs[:n_x]
    g_ref, w_ref, o_ref, xn_ref = refs[n_x:]

    @pl.when(pl.program_id(1) == 0)
    def _():
        x = x_refs[0][...]
        for extra in x_refs[1:]:
            x = x + extra[...]
        xn_ref[...] = _rms(x, g_ref[...]).astype(BF16)

    o_ref[...] = jnp.dot(xn_ref[...], w_ref[...], preferred_element_type=F32).astype(o_ref.dtype)


NORM_MATMUL_VMEM_LIMIT = 56 * 1024 * 1024


def norm_matmul(xs, g, w, tm=1024, tn=1024):
    t, d = xs[0].shape
    n = w.shape[1]
    tm = min(tm, t)
    tn = min(tn, n)
    return pl.pallas_call(
        functools.partial(_norm_matmul_kernel, n_x=len(xs)),
        grid=(t // tm, n // tn),
        in_specs=[pl.BlockSpec((tm, d), lambda i, j: (i, 0)) for _ in xs] + [
            pl.BlockSpec((1, d), lambda i, j: (0, 0)),
            pl.BlockSpec((d, tn), lambda i, j: (0, j)),
        ],
        out_specs=pl.BlockSpec((tm, tn), lambda i, j: (i, j)),
        out_shape=jax.ShapeDtypeStruct((t, n), BF16),
        scratch_shapes=[pltpu.VMEM((tm, d), BF16)],
        compiler_params=_cparams(("parallel", "arbitrary"), NORM_MATMUL_VMEM_LIMIT),
        name="norm_matmul",
    )(*xs, g.reshape(1, d), w)


def _matmul_res_kernel(*refs, n_in):
    xs = refs[:n_in]
    ws = refs[n_in:2 * n_in]
    res_refs = refs[2 * n_in:-1]
    o_ref = refs[-1]
    acc = res_refs[0][...]
    for res_ref in res_refs[1:]:
        acc = acc + res_ref[...]
    for x_ref, w_ref in zip(xs, ws):
        acc = acc + jnp.dot(x_ref[...], w_ref[...], preferred_element_type=F32)
    o_ref[...] = acc


def matmul_res(xs, ws, ress, tm=512, tn=2048):
    t, n = ress[0].shape
    tm = min(tm, t)
    tn = min(tn, n)
    n_in = len(xs)
    in_specs = [pl.BlockSpec((tm, x.shape[1]), lambda i, j: (i, 0)) for x in xs]
    in_specs += [pl.BlockSpec((w.shape[0], tn), lambda i, j: (0, j)) for w in ws]
    in_specs += [pl.BlockSpec((tm, tn), lambda i, j: (i, j)) for _ in ress]
    return pl.pallas_call(
        functools.partial(_matmul_res_kernel, n_in=n_in),
        grid=(t // tm, n // tn),
        in_specs=in_specs,
        out_specs=pl.BlockSpec((tm, tn), lambda i, j: (i, j)),
        out_shape=jax.ShapeDtypeStruct((t, n), F32),
        compiler_params=_cparams(("parallel", "parallel")),
        name="matmul_res",
    )(*xs, *ws, *ress)


def _mla_proj_kernel(z_ref, cos_ref, sin_ref, qg_ref, kvg_ref, wqa_ref, wqb_ref, wkv_ref,
                     q_ref, kv_ref, kr_ref):
    scale = (MLA_NOPE + MLA_ROPE) ** -0.5
    z = z_ref[...]
    cq = _rms(z[:, :MLA_Q_RANK].astype(F32), qg_ref[...]).astype(BF16)
    ckv = _rms(z[:, MLA_Q_RANK:MLA_Q_RANK + MLA_KV_RANK].astype(F32), kvg_ref[...]).astype(BF16)
    cos = cos_ref[...]
    sin = sin_ref[...]
    qa = jnp.dot(cq, wqa_ref[...], preferred_element_type=F32)
    qb = jnp.dot(cq, wqb_ref[...], preferred_element_type=F32)
    for h in range(MLA_HEADS):
        base = h * MLA_QK
        q_ref[:, base:base + LANES] = (qa[:, base:base + LANES] * scale).astype(BF16)
        rot = qa[:, base + LANES:base + 2 * LANES] * cos + qb[:, h * LANES:(h + 1) * LANES] * sin
        q_ref[:, base + LANES:base + 2 * LANES] = (rot * scale).astype(BF16)
    kv_ref[...] = jnp.dot(ckv, wkv_ref[...], preferred_element_type=F32).astype(BF16)
    off = MLA_Q_RANK + MLA_KV_RANK
    kr = z[:, off:off + LANES].astype(F32) * cos + z[:, off + LANES:off + 2 * LANES].astype(F32) * sin
    kr_ref[...] = kr.astype(BF16)


def mla_proj(z, cos_t, sin_t, qg, kvg, wqa, wqb, wkv, tm=512):
    t = z.shape[0]
    tm = min(tm, t)
    hq = MLA_HEADS * MLA_QK
    full = lambda a: pl.BlockSpec(a.shape, lambda i: (0, 0))
    return pl.pallas_call(
        _mla_proj_kernel,
        grid=(t // tm,),
        in_specs=[
            pl.BlockSpec((tm, 1024), lambda i: (i, 0)),
            pl.BlockSpec((tm, LANES), lambda i: (i, 0)),
            pl.BlockSpec((tm, LANES), lambda i: (i, 0)),
            full(qg), full(kvg), full(wqa), full(wqb), full(wkv),
        ],
        out_specs=[
            pl.BlockSpec((tm, hq), lambda i: (i, 0)),
            pl.BlockSpec((tm, hq), lambda i: (i, 0)),
            pl.BlockSpec((tm, LANES), lambda i: (i, 0)),
        ],
        out_shape=[
            jax.ShapeDtypeStruct((t, hq), BF16),
            jax.ShapeDtypeStruct((t, hq), BF16),
            jax.ShapeDtypeStruct((t, LANES), BF16),
        ],
        compiler_params=_cparams(("parallel",)),
        name="mla_proj",
    )(z, cos_t, sin_t, qg, kvg, wqa, wqb, wkv)


def _mla_attn_kernel(q_ref, kv_ref, kr_ref, o_ref, kcat_ref, *, seq, tq):
    kcat_ref[:, :LANES] = kv_ref[:, :LANES]
    kcat_ref[:, LANES:] = kr_ref[...]
    for qi in range(seq // tq):
        kend = (qi + 1) * tq
        q = q_ref[qi * tq:(qi + 1) * tq, :]
        s = lax.dot_general(q, kcat_ref[:kend, :], (((1,), (1,)), ((), ())),
                            preferred_element_type=F32)
        row = lax.broadcasted_iota(I32, (tq, kend), 0) + qi * tq
        col = lax.broadcasted_iota(I32, (tq, kend), 1)
        s = jnp.where(col <= row, s, NEG_INF)
        m = jnp.max(s, axis=-1, keepdims=True)
        p = jnp.exp(s - m)
        l = jnp.sum(p, axis=-1, keepdims=True)
        o = jnp.dot(p.astype(BF16), kv_ref[:kend, LANES:], preferred_element_type=F32)
        o_ref[qi * tq:(qi + 1) * tq, :] = (o / l).astype(BF16)


def mla_attention(qf, kvf, kr, batch, seq, tq=256):
    tq = min(tq, seq)
    t = batch * seq
    return pl.pallas_call(
        functools.partial(_mla_attn_kernel, seq=seq, tq=tq),
        grid=(batch, MLA_HEADS),
        in_specs=[
            pl.BlockSpec((seq, MLA_QK), lambda b, h: (b, h)),
            pl.BlockSpec((seq, MLA_QK), lambda b, h: (b, h)),
            pl.BlockSpec((seq, LANES), lambda b, h: (b, 0)),
        ],
        out_specs=pl.BlockSpec((seq, MLA_V), lambda b, h: (b, h)),
        out_shape=jax.ShapeDtypeStruct((t, MLA_HEADS * MLA_V), BF16),
        scratch_shapes=[pltpu.VMEM((seq, MLA_QK), BF16)],
        compiler_params=_cparams(("parallel", "parallel")),
        name="mla_attention",
    )(qf, kvf, kr)


def _retention_kernel(q_ref, k_ref, v_ref, g_ref, cos_ref, sin_ref, logg_ref, gn_ref, o_ref,
                      *, seq, blk):
    lg = logg_ref[...]
    cos = cos_ref[...]
    sin = sin_ref[...]
    half = RET_DK // 2

    def rope(x):
        return x * cos + pltpu.roll(x, half, 1) * sin

    q_all = rope(q_ref[...].astype(F32))
    k_all = rope(k_ref[...].astype(F32)) * (RET_DK ** -0.5)

    ri = lax.broadcasted_iota(I32, (blk, blk), 0)
    ci = lax.broadcasted_iota(I32, (blk, blk), 1)
    diff = (ri - ci).astype(F32)
    decay = jnp.where(diff >= 0, jnp.exp(lg[:, :1] * jnp.maximum(diff, 0.0)), 0.0)
    pos = lax.broadcasted_iota(I32, (blk, RET_DK), 0).astype(F32)
    xi = jnp.exp(lg * (pos + 1.0))
    zeta = jnp.exp(lg * (blk - 1.0 - pos))
    blk_decay = jnp.exp(lg * float(blk))

    state = jnp.zeros((RET_DK, RET_DV), F32)
    for n in range(seq // blk):
        sl = slice(n * blk, (n + 1) * blk)
        qc = q_all[sl]
        kc = k_all[sl]
        vc = v_ref[sl, :]
        att = lax.dot_general(qc.astype(BF16), kc.astype(BF16), (((1,), (1,)), ((), ())),
                              preferred_element_type=F32) * decay
        inner = jnp.dot(att.astype(BF16), vc, preferred_element_type=F32)
        cross = jnp.dot((qc * xi).astype(BF16), state.astype(BF16), preferred_element_type=F32)
        ret = inner + cross
        kz_t = (kc * zeta).T.astype(BF16)
        dstate = jnp.dot(kz_t, vc, preferred_element_type=F32)
        state = state * blk_decay + dstate
        mu = jnp.mean(ret, axis=-1, keepdims=True)
        cen = ret - mu
        var = jnp.mean(cen * cen, axis=-1, keepdims=True)
        normed = cen * lax.rsqrt(var + EPS) * gn_ref[...]
        gate = g_ref[sl, :].astype(F32)
        o_ref[sl, :] = (gate * jax.nn.sigmoid(gate) * normed).astype(BF16)


def retention(z, cos_t, sin_t, logg, gn_g, batch, seq):
    blk = min(RET_BLOCK, seq)
    t = batch * seq
    first = 1024 // LANES
    col = lambda part: (lambda b, h: (b, first + part * RET_HEADS + h))
    return pl.pallas_call(
        functools.partial(_retention_kernel, seq=seq, blk=blk),
        grid=(batch, RET_HEADS),
        in_specs=[
            pl.BlockSpec((seq, LANES), col(0)),
            pl.BlockSpec((seq, LANES), col(1)),
            pl.BlockSpec((seq, LANES), col(2)),
            pl.BlockSpec((seq, LANES), col(3)),
            pl.BlockSpec((seq, LANES), lambda b, h: (b, 0)),
            pl.BlockSpec((seq, LANES), lambda b, h: (b, 0)),
            pl.BlockSpec((None, 1, LANES), lambda b, h: (h, 0, 0)),
            pl.BlockSpec((None, 1, LANES), lambda b, h: (h, 0, 0)),
        ],
        out_specs=pl.BlockSpec((seq, RET_DV), lambda b, h: (b, h)),
        out_shape=jax.ShapeDtypeStruct((t, RET_HEADS * RET_DV), BF16),
        compiler_params=_cparams(("parallel", "parallel")),
        name="retention",
    )(z, z, z, z, cos_t, sin_t, logg, gn_g)


def _xattn_kernel(h_ref, g_ref, wq_ref, kv_ref, wo_ref, o_ref):
    h = h_ref[...]
    hn = _rms(h, g_ref[...]).astype(BF16)
    q = jnp.dot(hn, wq_ref[...], preferred_element_type=F32) * (XA_HEAD_DIM ** -0.5)
    q = q.astype(BF16)
    width = XA_HEADS * XA_HEAD_DIM
    outs = []
    for hd in range(XA_HEADS):
        sl = slice(hd * XA_HEAD_DIM, (hd + 1) * XA_HEAD_DIM)
        k = kv_ref[:, sl]
        v = kv_ref[:, width + hd * XA_HEAD_DIM:width + (hd + 1) * XA_HEAD_DIM]
        s = lax.dot_general(q[:, sl], k, (((1,), (1,)), ((), ())), preferred_element_type=F32)
        m = jnp.max(s, axis=-1, keepdims=True)
        p = jnp.exp(s - m)
        l = jnp.sum(p, axis=-1, keepdims=True)
        o = jnp.dot(p.astype(BF16), v, preferred_element_type=F32) / l
        outs.append(o.astype(BF16))
    o_all = jnp.concatenate(outs, axis=-1)
    o_ref[...] = h + jnp.dot(o_all, wo_ref[...], preferred_element_type=F32)


def cross_attention(h, g, wq, memkv, wo, seq, tm=512):
    t, d = h.shape
    tm = min(tm, seq)
    per_b = seq // tm
    m = memkv.shape[0] // (t // seq)
    full = lambda a: pl.BlockSpec(a.shape, lambda i: (0, 0))
    return pl.pallas_call(
        _xattn_kernel,
        grid=(t // tm,),
        in_specs=[
            pl.BlockSpec((tm, d), lambda i: (i, 0)),
            pl.BlockSpec((1, d), lambda i: (0, 0)),
            full(wq),
            pl.BlockSpec((m, memkv.shape[1]), lambda i: (i // per_b, 0)),
            full(wo),
        ],
        out_specs=pl.BlockSpec((tm, d), lambda i: (i, 0)),
        out_shape=jax.ShapeDtypeStruct((t, d), F32),
        compiler_params=_cparams(("parallel",)),
        name="cross_attention",
    )(h, g.reshape(1, d), wq, memkv, wo)


SC_PITCH = PEER_NKEYS + 8
SUBLANES = 8
TOPK_TOKENS = SUBLANES * LANES


def _peer_scores_kernel(h_ref, g_ref, wqt_ref, keys_ref, hn_ref, sc_ref, *, tm):
    hn = _rms(h_ref[...], g_ref[...]).astype(BF16)
    hn_ref[...] = hn
    q_t = lax.dot_general(wqt_ref[...], hn, (((1,), (1,)), ((), ())),
                          preferred_element_type=F32).astype(BF16)
    pad = jnp.zeros((SC_PITCH - PEER_NKEYS, LANES), F32)
    for grp in range(2 * PEER_HEADS):
        s = jnp.dot(keys_ref[grp], q_t[grp * PEER_HALF:(grp + 1) * PEER_HALF, :],
                    preferred_element_type=F32)
        for c in range(tm // LANES):
            sc_ref[grp, c * SC_PITCH:c * SC_PITCH + PEER_NKEYS, :] = s[:, c * LANES:(c + 1) * LANES]
            sc_ref[grp, c * SC_PITCH + PEER_NKEYS:(c + 1) * SC_PITCH, :] = pad


def peer_scores(h, g, wq_t, keys, tm=256):
    t, d = h.shape
    tm = min(tm, t)
    rows = (tm // LANES) * SC_PITCH
    return pl.pallas_call(
        functools.partial(_peer_scores_kernel, tm=tm),
        grid=(t // tm,),
        in_specs=[
            pl.BlockSpec((tm, d), lambda i: (i, 0)),
            pl.BlockSpec((1, d), lambda i: (0, 0)),
            pl.BlockSpec(wq_t.shape, lambda i: (0, 0)),
            pl.BlockSpec(keys.shape, lambda i: (0, 0, 0)),
        ],
        out_specs=[
            pl.BlockSpec((tm, d), lambda i: (i, 0)),
            pl.BlockSpec((2 * PEER_HEADS, rows, LANES), lambda i: (0, i, 0)),
        ],
        out_shape=[
            jax.ShapeDtypeStruct((t, d), BF16),
            jax.ShapeDtypeStruct((2 * PEER_HEADS, (t // LANES) * SC_PITCH, LANES), F32),
        ],
        compiler_params=_cparams(("parallel",)),
        name="peer_scores",
    )(h, g.reshape(1, d), wq_t, keys)


def _tree(op, xs):
    xs = list(xs)
    while len(xs) > 1:
        xs = [op(xs[i], xs[i + 1]) if i + 1 < len(xs) else xs[i] for i in range(0, len(xs), 2)]
    return xs[0]


def _top16(rows, ids):
    def better(a, b):
        return jnp.maximum(a[0], b[0]), jnp.where(a[0] >= b[0], a[1], b[1])

    vals, idxs = [], []
    for _ in range(PEER_TOPK):
        m, pos = _tree(better, [(x, float(i)) for x, i in zip(rows, ids)])
        rows = [jnp.where(pos == float(i), NEG_INF, x) for x, i in zip(rows, ids)]
        vals.append(m)
        idxs.append(pos)
    return vals, idxs


def _pick(table, sel):
    out = table[0]
    for i in range(1, PEER_TOPK):
        out = jnp.where(sel == float(i), table[i], out)
    return out


_PAIRS = [(i, j) for i in range(PEER_TOPK) for j in range(PEER_TOPK) if (i + 1) * (j + 1) <= PEER_TOPK]


def _route_head(s1, s2):
    keys = list(range(PEER_NKEYS))
    v1, i1 = _top16(s1, keys)
    v2, i2 = _top16(s2, keys)
    best, flat = _top16([v1[i] + v2[j] for i, j in _PAIRS], [i * PEER_TOPK + j for i, j in _PAIRS])
    probs = [jnp.exp(b - best[0]) for b in best]
    inv = 1.0 / _tree(jnp.add, probs)
    experts = []
    for r in range(PEER_TOPK):
        hi = jnp.floor(flat[r] * (1.0 / PEER_TOPK))
        experts.append(_pick(i1, hi) * PEER_NKEYS + _pick(i2, flat[r] - hi * PEER_TOPK))
    return experts, [p * inv for p in probs]


def _peer_topk_kernel(sc_ref, e_ref, gate_ref, esel_ref, gsel_ref):
    head = pl.program_id(1)

    def load(grp):
        return [sc_ref[grp, pl.ds(k, SUBLANES, stride=SC_PITCH), :] for k in range(PEER_NKEYS)]

    experts, gates = _route_head(load(0), load(1))
    for r in range(PEER_TOPK):
        row = pl.multiple_of((head * PEER_TOPK + r) * SUBLANES, SUBLANES)
        esel_ref[pl.ds(row, SUBLANES), :] = experts[r]
        gsel_ref[pl.ds(row, SUBLANES), :] = gates[r]

    @pl.when(head == PEER_HEADS - 1)
    def _():
        for s in range(SUBLANES):
            e_sel = esel_ref[pl.ds(s, PEER_SEL, stride=SUBLANES), :]
            g_sel = gsel_ref[pl.ds(s, PEER_SEL, stride=SUBLANES), :]
            e_ref[s * LANES:(s + 1) * LANES, :] = e_sel.T.astype(I32)
            gate_ref[s * LANES:(s + 1) * LANES, :] = g_sel.T


def peer_topk(scores):
    t = scores.shape[1] // SC_PITCH * LANES
    rows = SUBLANES * SC_PITCH
    return pl.pallas_call(
        _peer_topk_kernel,
        grid=(t // TOPK_TOKENS, PEER_HEADS),
        in_specs=[pl.BlockSpec((2, rows, LANES), lambda i, hd: (hd, i, 0))],
        out_specs=[
            pl.BlockSpec((TOPK_TOKENS, PEER_SEL), lambda i, hd: (i, 0)),
            pl.BlockSpec((TOPK_TOKENS, PEER_SEL), lambda i, hd: (i, 0)),
        ],
        out_shape=[
            jax.ShapeDtypeStruct((t, PEER_SEL), I32),
            jax.ShapeDtypeStruct((t, PEER_SEL), F32),
        ],
        scratch_shapes=[
            pltpu.VMEM((PEER_SEL * SUBLANES, LANES), F32),
            pltpu.VMEM((PEER_SEL * SUBLANES, LANES), F32),
        ],
        compiler_params=_cparams(("parallel", "arbitrary")),
        name="peer_topk",
    )(scores)


W_UNROLL = 32
W_SLABS = PEER_NKEYS // SUBLANES
U32 = jnp.uint32
BF16_HI = np.uint32(0xFFFF0000)
HALF_WORD = np.uint32(16)


def _token_gates(e_row, g_row, sub):
    a_hot = jnp.where(sub == lax.shift_right_logical(e_row, 7), 1.0, 0.0).astype(BF16)
    b_hot = jnp.where(sub == (e_row & (PEER_NKEYS - 1)), g_row, 0.0).astype(BF16)
    return lax.dot_general(a_hot, b_hot, (((1,), (1,)), ((), ())), preferred_element_type=F32)


def _bf16_bits(x):
    return lax.bitcast_convert_type(x.astype(BF16).astype(F32), U32)


def _peer_gates_kernel(e_hi_ref, g_hi_ref, e_lo_ref, g_lo_ref, w_ref, *, tw):
    sub = lax.broadcasted_iota(I32, (PEER_NKEYS, PEER_SEL), 0)

    def group(gi, carry):
        p0 = pl.multiple_of(gi * W_UNROLL, W_UNROLL)
        for i in range(W_UNROLL):
            rows = pl.ds(p0 + i, 1)
            w_hi = _token_gates(e_hi_ref[rows, :], g_hi_ref[rows, :], sub)
            w_lo = _token_gates(e_lo_ref[rows, :], g_lo_ref[rows, :], sub)
            packed = _bf16_bits(w_hi) | lax.shift_right_logical(_bf16_bits(w_lo), HALF_WORD)
            row = pl.multiple_of((p0 + i) * SUBLANES, SUBLANES)
            for s in range(W_SLABS):
                w_ref[s, pl.ds(row, SUBLANES), :] = packed[s * SUBLANES:(s + 1) * SUBLANES, :]
        return carry

    lax.fori_loop(0, tw // W_UNROLL, group, 0)


def peer_gates(e, g, tile, tw=64):
    t = e.shape[0]
    half = tile // 2
    tw = min(tw, half)
    per_half = half // tw
    hi = lambda i, c: (i * 2 * per_half + c, 0)
    lo = lambda i, c: (i * 2 * per_half + per_half + c, 0)
    return pl.pallas_call(
        functools.partial(_peer_gates_kernel, tw=tw),
        grid=(t // tile, per_half),
        in_specs=[
            pl.BlockSpec((tw, PEER_SEL), hi),
            pl.BlockSpec((tw, PEER_SEL), hi),
            pl.BlockSpec((tw, PEER_SEL), lo),
            pl.BlockSpec((tw, PEER_SEL), lo),
        ],
        out_specs=pl.BlockSpec((W_SLABS, tw * SUBLANES, PEER_NKEYS), lambda i, c: (0, i * per_half + c, 0)),
        out_shape=jax.ShapeDtypeStruct((W_SLABS, (t // 2) * SUBLANES, PEER_NKEYS), U32),
        compiler_params=_cparams(("parallel", "parallel")),
        name="peer_gates",
    )(e, g, e, g)


DENSE_CHUNK = 512
DENSE_VMEM_LIMIT = 60 * 1024 * 1024


def _peer_dense_kernel(hn_ref, u_ref, w_ref, v_ref, o_ref):
    j = pl.program_id(1)

    @pl.when(j == 0)
    def _():
        o_ref[...] = jnp.zeros_like(o_ref)

    hn = hn_ref[...]
    tm = hn.shape[0]
    te = u_ref.shape[0]
    k1_per_chunk = DENSE_CHUNK // PEER_NKEYS
    steps_per_slab = (SUBLANES * PEER_NKEYS) // te
    k1_base = (j % steps_per_slab) * (te // PEER_NKEYS)
    parts = []
    for c in range(te // DENSE_CHUNK):
        rows = slice(c * DENSE_CHUNK, (c + 1) * DENSE_CHUNK)
        pre = lax.dot_general(hn, u_ref[rows, :], (((1,), (1,)), ((), ())),
                              preferred_element_type=F32)
        act = 0.5 * pre * (1.0 + lax.erf(pre * (2.0 ** -0.5)))
        words = [w_ref[pl.ds(k1_base + c * k1_per_chunk + k, tm // 2, stride=SUBLANES), :]
                 for k in range(k1_per_chunk)]
        first = jnp.concatenate([lax.bitcast_convert_type(x & BF16_HI, F32) for x in words], axis=1)
        second = jnp.concatenate(
            [lax.bitcast_convert_type(lax.shift_left(x, HALF_WORD), F32) for x in words], axis=1)
        gate = jnp.concatenate([first, second], axis=0)
        parts.append((act * gate).astype(BF16))
    weighted = parts[0] if len(parts) == 1 else jnp.concatenate(parts, axis=1)
    o_ref[...] += jnp.dot(weighted, v_ref[...], preferred_element_type=F32)


DENSE_TOKENS = 1024


def peer_dense(hn, u, w, v, layer, tm, te=1024):
    t, d = hn.shape
    n_exp = u.shape[1]
    slab = SUBLANES * PEER_NKEYS
    return pl.pallas_call(
        _peer_dense_kernel,
        grid=(t // tm, n_exp // te),
        in_specs=[
            pl.BlockSpec((tm, d), lambda i, j: (i, 0)),
            pl.BlockSpec((None, te, d), lambda i, j: (layer, j, 0)),
            pl.BlockSpec((None, (tm // 2) * SUBLANES, PEER_NKEYS), lambda i, j: (j * te // slab, i, 0)),
            pl.BlockSpec((None, te, d), lambda i, j: (layer, j, 0)),
        ],
        out_specs=pl.BlockSpec((tm, d), lambda i, j: (i, 0)),
        out_shape=jax.ShapeDtypeStruct((t, d), F32),
        compiler_params=_cparams(("parallel", "arbitrary"), DENSE_VMEM_LIMIT),
        name="peer_dense",
    )(hn, u, w, v)


def _add_rmsnorm_kernel(a_ref, b_ref, g_ref, o_ref):
    o_ref[...] = _rms(a_ref[...] + b_ref[...], g_ref[...])


def add_rmsnorm(a, b, g, tm=512):
    t, d = a.shape
    tm = min(tm, t)
    return pl.pallas_call(
        _add_rmsnorm_kernel,
        grid=(t // tm,),
        in_specs=[
            pl.BlockSpec((tm, d), lambda i: (i, 0)),
            pl.BlockSpec((tm, d), lambda i: (i, 0)),
            pl.BlockSpec((1, d), lambda i: (0, 0)),
        ],
        out_specs=pl.BlockSpec((tm, d), lambda i: (i, 0)),
        out_shape=jax.ShapeDtypeStruct((t, d), F32),
        compiler_params=_cparams(("parallel",)),
        name="add_rmsnorm",
    )(a, b, g.reshape(1, d))


LRU_ROWS = 8


def _lru_kernel(gate_ref, rec_ref, cw_ref, cb_ref, wa_ref, ba_ref, wx_ref, bx_ref, lam_ref, o_ref,
                tail_ref, carry_ref, a_ref, u_ref, hs_ref, *, ts):
    @pl.when(pl.program_id(1) == 0)
    def _():
        tail_ref[...] = jnp.zeros_like(tail_ref)
        carry_ref[...] = jnp.zeros_like(carry_ref)

    x = rec_ref[...].astype(F32)
    xp = jnp.concatenate([tail_ref[...], x], axis=0)
    tail_ref[...] = x[ts - LRU_ROWS:, :]
    xc = cb_ref[...] + jnp.zeros_like(x)
    for j in range(CONV_WIDTH):
        off = LRU_ROWS - (CONV_WIDTH - 1) + j
        xc = xc + cw_ref[j:j + 1, :] * xp[off:off + ts, :]

    soft = jnp.log(1.0 + jnp.exp(-lam_ref[...]))
    for n in range(LRU_BLOCKS):
        sl = slice(n * LRU_BLOCK, (n + 1) * LRU_BLOCK)
        xb = xc[:, sl]
        xb16 = xb.astype(BF16)
        r = jax.nn.sigmoid(jnp.dot(xb16, wa_ref[n], preferred_element_type=F32) + ba_ref[:, sl])
        i = jax.nn.sigmoid(jnp.dot(xb16, wx_ref[n], preferred_element_type=F32) + bx_ref[:, sl])
        log_a = -LRU_C * r * soft[:, sl]
        a = jnp.exp(log_a)
        a_ref[:, sl] = a
        u_ref[:, sl] = jnp.sqrt(1.0 - a * a) * (i * xb)

    def step(c, h):
        r0 = pl.multiple_of(c * LRU_ROWS, LRU_ROWS)
        a8 = a_ref[pl.ds(r0, LRU_ROWS), :]
        u8 = u_ref[pl.ds(r0, LRU_ROWS), :]
        rows = []
        for r in range(LRU_ROWS):
            h = a8[r:r + 1, :] * h + u8[r:r + 1, :]
            rows.append(h)
        hs_ref[pl.ds(r0, LRU_ROWS), :] = jnp.concatenate(rows, axis=0)
        return h

    carry_ref[...] = lax.fori_loop(0, ts // LRU_ROWS, step, carry_ref[...])
    o_ref[...] = (jax.nn.gelu(gate_ref[...].astype(F32)) * hs_ref[...]).astype(BF16)


def lru_branch(z, conv_w, conv_b, wa, ba, wx, bx, lam, batch, seq, ts=256):
    t = z.shape[0]
    w = z.shape[1] // 2
    ts = min(ts, seq)
    per_b = seq // ts
    row = lambda a: pl.BlockSpec(a.shape, lambda b, s: (0, 0))
    cube = lambda a: pl.BlockSpec(a.shape, lambda b, s: (0, 0, 0))
    return pl.pallas_call(
        functools.partial(_lru_kernel, ts=ts),
        grid=(batch, per_b),
        in_specs=[
            pl.BlockSpec((ts, w), lambda b, s: (b * per_b + s, 0)),
            pl.BlockSpec((ts, w), lambda b, s: (b * per_b + s, 1)),
            row(conv_w), row(conv_b), cube(wa), row(ba), cube(wx), row(bx), row(lam),
        ],
        out_specs=pl.BlockSpec((ts, w), lambda b, s: (b * per_b + s, 0)),
        out_shape=jax.ShapeDtypeStruct((t, w), BF16),
        scratch_shapes=[
            pltpu.VMEM((LRU_ROWS, w), F32),
            pltpu.VMEM((1, w), F32),
            pltpu.VMEM((ts, w), F32),
            pltpu.VMEM((ts, w), F32),
            pltpu.VMEM((ts, w), F32),
        ],
        compiler_params=_cparams(("parallel", "arbitrary")),
        name="lru_branch",
    )(z, z, conv_w, conv_b, wa, ba, wx, bx, lam)


def _rope_tables(positions, dim, width):
    inv = ROPE_THETA ** (-jnp.arange(0, dim, 2, dtype=F32) / dim)
    ang = positions.astype(F32).reshape(-1, 1) * inv
    pad = jnp.zeros((ang.shape[0], width - dim), F32)
    cos = jnp.concatenate([jnp.cos(ang), jnp.cos(ang), pad], axis=-1)
    sin = jnp.concatenate([jnp.sin(ang), jnp.sin(ang), pad], axis=-1)
    return cos, sin


def _rotate_half_cols(w):
    half = w.shape[-1] // 2
    return jnp.concatenate([-w[..., half:], w[..., :half]], axis=-1)


def _even_weights(w_in, w_uq):
    d = w_in.shape[0]
    o = [0, 512, 768, 832]
    z64 = jnp.zeros((d, 64), F32)
    k_rope = w_in[:, o[2]:o[3]]
    w_in_p = jnp.concatenate(
        [w_in[:, :o[2]], k_rope, z64, _rotate_half_cols(k_rope), z64, w_in[:, o[3]:]], axis=-1)
    wq3 = w_uq.reshape(MLA_Q_RANK, MLA_HEADS, MLA_NOPE + MLA_ROPE)
    nope, rope = wq3[..., :MLA_NOPE], wq3[..., MLA_NOPE:]
    zq = jnp.zeros((MLA_Q_RANK, MLA_HEADS, 64), F32)
    wqa = jnp.concatenate([nope, rope, zq], axis=-1).reshape(MLA_Q_RANK, MLA_HEADS * MLA_QK)
    wqb = jnp.concatenate([_rotate_half_cols(rope), zq], axis=-1).reshape(MLA_Q_RANK, MLA_HEADS * LANES)
    return w_in_p.astype(BF16), wqa.astype(BF16), wqb.astype(BF16)


def _memory_and_route(h, mem2, layer, seq, xa_norm_g, mem_norm_g, xa_wq, xa_wk, xa_wv, xa_wo,
                      ffn_norm_g, peer_wq, peer_sub_keys):
    w_kv = jnp.concatenate([xa_wk[layer], xa_wv[layer]], axis=-1).astype(BF16)
    memkv = norm_matmul([mem2], mem_norm_g[layer], w_kv)
    h = cross_attention(h, xa_norm_g[layer], xa_wq[layer].astype(BF16), memkv,
                        xa_wo[layer].astype(BF16), seq)
    keys = peer_sub_keys[layer].reshape(2 * PEER_HEADS, PEER_NKEYS, PEER_HALF).astype(BF16)
    hn, scores = peer_scores(h, ffn_norm_g[layer], peer_wq[layer].T.astype(BF16), keys)
    e_sel, g_sel = peer_topk(scores)
    return h, hn, peer_gates(e_sel, g_sel, min(DENSE_TOKENS, h.shape[0]))


def kernel(x, mem, positions, mix_norm_g, xa_norm_g, ffn_norm_g, mem_norm_g, ev_w_in, ev_w_out, mla_q_norm_g, mla_w_uq, mla_kv_norm_g, mla_w_ukv, ret_gn_g, od_w_in, od_w_out, conv_w, conv_b, lru_wa, lru_ba, lru_wx, lru_bx, lru_lambda, xa_wq, xa_wk, xa_wv, xa_wo, peer_wq, peer_sub_keys, peer_u, peer_v, final_norm_g):
    batch, seq, d = x.shape
    t = batch * seq
    h = x.reshape(t, d)
    mem2 = mem.reshape(-1, d)
    tail = (xa_norm_g, mem_norm_g, xa_wq, xa_wk, xa_wv, xa_wo, ffn_norm_g, peer_wq, peer_sub_keys)
    u16 = peer_u.astype(BF16)
    v16 = peer_v.astype(BF16)

    w_in_p, wqa, wqb = _even_weights(ev_w_in[0], mla_w_uq[0])
    z = norm_matmul([h], mix_norm_g[0], w_in_p)
    cos_m, sin_m = _rope_tables(positions, MLA_ROPE, LANES)
    qf, kvf, kr = mla_proj(z, cos_m, sin_m, mla_q_norm_g[0].reshape(1, -1),
                           mla_kv_norm_g[0].reshape(1, -1), wqa, wqb, mla_w_ukv[0].astype(BF16))
    attn = mla_attention(qf, kvf, kr, batch, seq)
    cos_r, sin_r = _rope_tables(positions, RET_DK, RET_DK)
    sign = jnp.concatenate([-jnp.ones((RET_DK // 2,), F32), jnp.ones((RET_DK // 2,), F32)])
    log_g = jnp.log(1.0 - 2.0 ** (-5.0 - jnp.arange(RET_HEADS, dtype=F32)))
    logg = jnp.broadcast_to(log_g[:, None, None], (RET_HEADS, 1, LANES))
    ret = retention(z, cos_r, sin_r * sign, logg, ret_gn_g[0].reshape(RET_HEADS, 1, RET_DV), batch, seq)
    w_out = ev_w_out[0].astype(BF16)
    n_a = MLA_HEADS * MLA_V
    h = matmul_res([attn, ret], [w_out[:n_a], w_out[n_a:]], [h])
    h, hn, w = _memory_and_route(h, mem2, 0, seq, *tail)
    tile = min(DENSE_TOKENS, t)
    ffn = peer_dense(hn, u16, w, v16, 0, tile)

    z = norm_matmul([h, ffn], mix_norm_g[1], od_w_in[0].astype(BF16), tn=512)
    row = lambda a: a.reshape(1, -1)
    cw = jnp.concatenate([conv_w[0], jnp.zeros((8 - CONV_WIDTH, conv_w.shape[-1]), F32)], axis=0)
    y = lru_branch(z, cw, row(conv_b[0]), lru_wa[0].astype(BF16), row(lru_ba[0]),
                   lru_wx[0].astype(BF16), row(lru_bx[0]), row(lru_lambda[0]), batch, seq)
    h = matmul_res([y], [od_w_out[0].astype(BF16)], [h, ffn])
    h, hn, w = _memory_and_route(h, mem2, 1, seq, *tail)
    out = add_rmsnorm(h, peer_dense(hn, u16, w, v16, 1, tile), final_norm_g)
    return out.reshape(batch, seq, d)
```

```python
import functools
import math

import jax
import jax.numpy as jnp
import numpy as np
from jax import lax
from jax.experimental import pallas as pl
from jax.experimental.pallas import tpu as pltpu

F32 = jnp.float32
BF16 = jnp.bfloat16
I32 = jnp.int32

EPS = 1e-6
ROPE_THETA = 10000.0

MLA_HEADS = 8
MLA_NOPE = 128
MLA_ROPE = 64
MLA_V = 128
MLA_Q_RANK = 512
MLA_KV_RANK = 256
MLA_QK = 256

RET_HEADS = 8
RET_DK = 128
RET_DV = 128
RET_BLOCK = 256

LRU_BLOCKS = 16
LRU_BLOCK = 128
CONV_WIDTH = 4
LRU_C = 8.0

XA_HEADS = 4
XA_HEAD_DIM = 128

PEER_HEADS = 8
PEER_NKEYS = 128
PEER_HALF = 128
PEER_TOPK = 16
PEER_SEL = PEER_HEADS * PEER_TOPK

LANES = 128
VMEM_LIMIT = 48 * 1024 * 1024

NEG_INF = float("-inf")


def _cparams(sem, vmem=VMEM_LIMIT):
    return pltpu.CompilerParams(dimension_semantics=sem, vmem_limit_bytes=vmem)


def _rms(x, g):
    ms = jnp.mean(x * x, axis=-1, keepdims=True)
    return x * lax.rsqrt(ms + EPS) * g


def _norm_matmul_kernel(*refs, n_x):
    x_refs = refs[:n_x]
    g_ref, w_ref, o_ref, xn_ref = refs[n_x:]

    @pl.when(pl.program_id(1) == 0)
    def _():
        x = x_refs[0][...]
        for extra in x_refs[1:]:
            x = x + extra[...]
        xn_ref[...] = _rms(x, g_ref[...]).astype(BF16)

    o_ref[...] = jnp.dot(xn_ref[...], w_ref[...], preferred_element_type=F32).astype(o_ref.dtype)


NORM_MATMUL_VMEM_LIMIT = 56 * 1024 * 1024


def norm_matmul(xs, g, w, tm=1024, tn=1024):
    t, d = xs[0].shape
    n = w.shape[1]
    tm = min(tm, t)
    tn = min(tn, n)
    return pl.pallas_call(
        functools.partial(_norm_matmul_kernel, n_x=len(xs)),
        grid=(t // tm, n // tn),
        in_specs=[pl.BlockSpec((tm, d), lambda i, j: (i, 0)) for _ in xs] + [
            pl.BlockSpec((1, d), lambda i, j: (0, 0)),
            pl.BlockSpec((d, tn), lambda i, j: (0, j)),
        ],
        out_specs=pl.BlockSpec((tm, tn), lambda i, j: (i, j)),
        out_shape=jax.ShapeDtypeStruct((t, n), BF16),
        scratch_shapes=[pltpu.VMEM((tm, d), BF16)],
        compiler_params=_cparams(("parallel", "arbitrary"), NORM_MATMUL_VMEM_LIMIT),
        name="norm_matmul",
    )(*xs, g.reshape(1, d), w)


def _matmul_res_kernel(*refs, n_in):
    xs = refs[:n_in]
    ws = refs[n_in:2 * n_in]
    res_refs = refs[2 * n_in:-1]
    o_ref = refs[-1]
    acc = res_refs[0][...]
    for res_ref in res_refs[1:]:
        acc = acc + res_ref[...]
    for x_ref, w_ref in zip(xs, ws):
        acc = acc + jnp.dot(x_ref[...], w_ref[...], preferred_element_type=F32)
    o_ref[...] = acc


def matmul_res(xs, ws, ress, tm=512, tn=2048):
    t, n = ress[0].shape
    tm = min(tm, t)
    tn = min(tn, n)
    n_in = len(xs)
    in_specs = [pl.BlockSpec((tm, x.shape[1]), lambda i, j: (i, 0)) for x in xs]
    in_specs += [pl.BlockSpec((w.shape[0], tn), lambda i, j: (0, j)) for w in ws]
    in_specs += [pl.BlockSpec((tm, tn), lambda i, j: (i, j)) for _ in ress]
    return pl.pallas_call(
        functools.partial(_matmul_res_kernel, n_in=n_in),
        grid=(t // tm, n // tn),
        in_specs=in_specs,
        out_specs=pl.BlockSpec((tm, tn), lambda i, j: (i, j)),
        out_shape=jax.ShapeDtypeStruct((t, n), F32),
        compiler_params=_cparams(("parallel", "parallel")),
        name="matmul_res",
    )(*xs, *ws, *ress)


def _mla_proj_kernel(z_ref, cos_ref, sin_ref, qg_ref, kvg_ref, wqa_ref, wqb_ref, wkv_ref,
                     q_ref, kv_ref, kr_ref):
    scale = (MLA_NOPE + MLA_ROPE) ** -0.5
    z = z_ref[...]
    cq = _rms(z[:, :MLA_Q_RANK].astype(F32), qg_ref[...]).astype(BF16)
    ckv = _rms(z[:, MLA_Q_RANK:MLA_Q_RANK + MLA_KV_RANK].astype(F32), kvg_ref[...]).astype(BF16)
    cos = cos_ref[...]
    sin = sin_ref[...]
    qa = jnp.dot(cq, wqa_ref[...], preferred_element_type=F32)
    qb = jnp.dot(cq, wqb_ref[...], preferred_element_type=F32)
    for h in range(MLA_HEADS):
        base = h * MLA_QK
        q_ref[:, base:base + LANES] = (qa[:, base:base + LANES] * scale).astype(BF16)
        rot = qa[:, base + LANES:base + 2 * LANES] * cos + qb[:, h * LANES:(h + 1) * LANES] * sin
        q_ref[:, base + LANES:base + 2 * LANES] = (rot * scale).astype(BF16)
    kv_ref[...] = jnp.dot(ckv, wkv_ref[...], preferred_element_type=F32).astype(BF16)
    off = MLA_Q_RANK + MLA_KV_RANK
    kr = z[:, off:off + LANES].astype(F32) * cos + z[:, off + LANES:off + 2 * LANES].astype(F32) * sin
    kr_ref[...] = kr.astype(BF16)


def mla_proj(z, cos_t, sin_t, qg, kvg, wqa, wqb, wkv, tm=512):
    t = z.shape[0]
    tm = min(tm, t)
    hq = MLA_HEADS * MLA_QK
    full = lambda a: pl.BlockSpec(a.shape, lambda i: (0, 0))
    return pl.pallas_call(
        _mla_proj_kernel,
        grid=(t // tm,),
        in_specs=[
            pl.BlockSpec((tm, 1024), lambda i: (i, 0)),
            pl.BlockSpec((tm, LANES), lambda i: (i, 0)),
            pl.BlockSpec((tm, LANES), lambda i: (i, 0)),
            full(qg), full(kvg), full(wqa), full(wqb), full(wkv),
        ],
        out_specs=[
            pl.BlockSpec((tm, hq), lambda i: (i, 0)),
            pl.BlockSpec((tm, hq), lambda i: (i, 0)),
            pl.BlockSpec((tm, LANES), lambda i: (i, 0)),
        ],
        out_shape=[
            jax.ShapeDtypeStruct((t, hq), BF16),
            jax.ShapeDtypeStruct((t, hq), BF16),
            jax.ShapeDtypeStruct((t, LANES), BF16),
        ],
        compiler_params=_cparams(("parallel",)),
        name="mla_proj",
    )(z, cos_t, sin_t, qg, kvg, wqa, wqb, wkv)


def _mla_attn_kernel(q_ref, kv_ref, kr_ref, o_ref, kcat_ref, *, seq, tq):
    kcat_ref[:, :LANES] = kv_ref[:, :LANES]
    kcat_ref[:, LANES:] = kr_ref[...]
    for qi in range(seq // tq):
        kend = (qi + 1) * tq
        q = q_ref[qi * tq:(qi + 1) * tq, :]
        s = lax.dot_general(q, kcat_ref[:kend, :], (((1,), (1,)), ((), ())),
                            preferred_element_type=F32)
        row = lax.broadcasted_iota(I32, (tq, kend), 0) + qi * tq
        col = lax.broadcasted_iota(I32, (tq, kend), 1)
        s = jnp.where(col <= row, s, NEG_INF)
        m = jnp.max(s, axis=-1, keepdims=True)
        p = jnp.exp(s - m)
        l = jnp.sum(p, axis=-1, keepdims=True)
        o = jnp.dot(p.astype(BF16), kv_ref[:kend, LANES:], preferred_element_type=F32)
        o_ref[qi * tq:(qi + 1) * tq, :] = (o / l).astype(BF16)


def mla_attention(qf, kvf, kr, batch, seq, tq=256):
    tq = min(tq, seq)
    t = batch * seq
    return pl.pallas_call(
        functools.partial(_mla_attn_kernel, seq=seq, tq=tq),
        grid=(batch, MLA_HEADS),
        in_specs=[
            pl.BlockSpec((seq, MLA_QK), lambda b, h: (b, h)),
            pl.BlockSpec((seq, MLA_QK), lambda b, h: (b, h)),
            pl.BlockSpec((seq, LANES), lambda b, h: (b, 0)),
        ],
        out_specs=pl.BlockSpec((seq, MLA_V), lambda b, h: (b, h)),
        out_shape=jax.ShapeDtypeStruct((t, MLA_HEADS * MLA_V), BF16),
        scratch_shapes=[pltpu.VMEM((seq, MLA_QK), BF16)],
        compiler_params=_cparams(("parallel", "parallel")),
        name="mla_attention",
    )(qf, kvf, kr)


def _retention_kernel(q_ref, k_ref, v_ref, g_ref, cos_ref, sin_ref, logg_ref, gn_ref, o_ref,
                      *, seq, blk):
    lg = logg_ref[...]
    cos = cos_ref[...]
    sin = sin_ref[...]
    half = RET_DK // 2

    def rope(x):
        return x * cos + pltpu.roll(x, half, 1) * sin

    q_all = rope(q_ref[...].astype(F32))
    k_all = rope(k_ref[...].astype(F32)) * (RET_DK ** -0.5)

    ri = lax.broadcasted_iota(I32, (blk, blk), 0)
    ci = lax.broadcasted_iota(I32, (blk, blk), 1)
    diff = (ri - ci).astype(F32)
    decay = jnp.where(diff >= 0, jnp.exp(lg[:, :1] * jnp.maximum(diff, 0.0)), 0.0)
    pos = lax.broadcasted_iota(I32, (blk, RET_DK), 0).astype(F32)
    xi = jnp.exp(lg * (pos + 1.0))
    zeta = jnp.exp(lg * (blk - 1.0 - pos))
    blk_decay = jnp.exp(lg * float(blk))

    state = jnp.zeros((RET_DK, RET_DV), F32)
    for n in range(seq // blk):
        sl = slice(n * blk, (n + 1) * blk)
        qc = q_all[sl]
        kc = k_all[sl]
        vc = v_ref[sl, :]
        att = lax.dot_general(qc.astype(BF16), kc.astype(BF16), (((1,), (1,)), ((), ())),
                              preferred_element_type=F32) * decay
        inner = jnp.dot(att.astype(BF16), vc, preferred_element_type=F32)
        cross = jnp.dot((qc * xi).astype(BF16), state.astype(BF16), preferred_element_type=F32)
        ret = inner + cross
        kz_t = (kc * zeta).T.astype(BF16)
        dstate = jnp.dot(kz_t, vc, preferred_element_type=F32)
        state = state * blk_decay + dstate
        mu = jnp.mean(ret, axis=-1, keepdims=True)
        cen = ret - mu
        var = jnp.mean(cen * cen, axis=-1, keepdims=True)
        normed = cen * lax.rsqrt(var + EPS) * gn_ref[...]
        gate = g_ref[sl, :].astype(F32)
        o_ref[sl, :] = (gate * jax.nn.sigmoid(gate) * normed).astype(BF16)


def retention(z, cos_t, sin_t, logg, gn_g, batch, seq):
    blk = min(RET_BLOCK, seq)
    t = batch * seq
    first = 1024 // LANES
    col = lambda part: (lambda b, h: (b, first + part * RET_HEADS + h))
    return pl.pallas_call(
        functools.partial(_retention_kernel, seq=seq, blk=blk),
        grid=(batch, RET_HEADS),
        in_specs=[
            pl.BlockSpec((seq, LANES), col(0)),
            pl.BlockSpec((seq, LANES), col(1)),
            pl.BlockSpec((seq, LANES), col(2)),
            pl.BlockSpec((seq, LANES), col(3)),
            pl.BlockSpec((seq, LANES), lambda b, h: (b, 0)),
            pl.BlockSpec((seq, LANES), lambda b, h: (b, 0)),
            pl.BlockSpec((None, 1, LANES), lambda b, h: (h, 0, 0)),
            pl.BlockSpec((None, 1, LANES), lambda b, h: (h, 0, 0)),
        ],
        out_specs=pl.BlockSpec((seq, RET_DV), lambda b, h: (b, h)),
        out_shape=jax.ShapeDtypeStruct((t, RET_HEADS * RET_DV), BF16),
        compiler_params=_cparams(("parallel", "parallel")),
        name="retention",
    )(z, z, z, z, cos_t, sin_t, logg, gn_g)


def _xattn_kernel(h_ref, g_ref, wq_ref, kv_ref, wo_ref, o_ref):
    h = h_ref[...]
    hn = _rms(h, g_ref[...]).astype(BF16)
    q = jnp.dot(hn, wq_ref[...], preferred_element_type=F32) * (XA_HEAD_DIM ** -0.5)
    q = q.astype(BF16)
    width = XA_HEADS * XA_HEAD_DIM
    outs = []
    for hd in range(XA_HEADS):
        sl = slice(hd * XA_HEAD_DIM, (hd + 1) * XA_HEAD_DIM)
        k = kv_ref[:, sl]
        v = kv_ref[:, width + hd * XA_HEAD_DIM:width + (hd + 1) * XA_HEAD_DIM]
        s = lax.dot_general(q[:, sl], k, (((1,), (1,)), ((), ())), preferred_element_type=F32)
        m = jnp.max(s, axis=-1, keepdims=True)
        p = jnp.exp(s - m)
        l = jnp.sum(p, axis=-1, keepdims=True)
        o = jnp.dot(p.astype(BF16), v, preferred_element_type=F32) / l
        outs.append(o.astype(BF16))
    o_all = jnp.concatenate(outs, axis=-1)
    o_ref[...] = h + jnp.dot(o_all, wo_ref[...], preferred_element_type=F32)


def cross_attention(h, g, wq, memkv, wo, seq, tm=512):
    t, d = h.shape
    tm = min(tm, seq)
    per_b = seq // tm
    m = memkv.shape[0] // (t // seq)
    full = lambda a: pl.BlockSpec(a.shape, lambda i: (0, 0))
    return pl.pallas_call(
        _xattn_kernel,
        grid=(t // tm,),
        in_specs=[
            pl.BlockSpec((tm, d), lambda i: (i, 0)),
            pl.BlockSpec((1, d), lambda i: (0, 0)),
            full(wq),
            pl.BlockSpec((m, memkv.shape[1]), lambda i: (i // per_b, 0)),
            full(wo),
        ],
        out_specs=pl.BlockSpec((tm, d), lambda i: (i, 0)),
        out_shape=jax.ShapeDtypeStruct((t, d), F32),
        compiler_params=_cparams(("parallel",)),
        name="cross_attention",
    )(h, g.reshape(1, d), wq, memkv, wo)


SC_PITCH = PEER_NKEYS + 8
SUBLANES = 8
TOPK_TOKENS = SUBLANES * LANES


def _peer_scores_kernel(h_ref, g_ref, wqt_ref, keys_ref, hn_ref, sc_ref, *, tm):
    hn = _rms(h_ref[...], g_ref[...]).astype(BF16)
    hn_ref[...] = hn
    q_t = lax.dot_general(wqt_ref[...], hn, (((1,), (1,)), ((), ())),
                          preferred_element_type=F32).astype(BF16)
    pad = jnp.zeros((SC_PITCH - PEER_NKEYS, LANES), F32)
    for grp in range(2 * PEER_HEADS):
        s = jnp.dot(keys_ref[grp], q_t[grp * PEER_HALF:(grp + 1) * PEER_HALF, :],
                    preferred_element_type=F32)
        for c in range(tm // LANES):
            sc_ref[grp, c * SC_PITCH:c * SC_PITCH + PEER_NKEYS, :] = s[:, c * LANES:(c + 1) * LANES]
            sc_ref[grp, c * SC_PITCH + PEER_NKEYS:(c + 1) * SC_PITCH, :] = pad


def peer_scores(h, g, wq_t, keys, tm=512):
    t, d = h.shape
    tm = min(tm, t)
    rows = (tm // LANES) * SC_PITCH
    return pl.pallas_call(
        functools.partial(_peer_scores_kernel, tm=tm),
        grid=(t // tm,),
        in_specs=[
            pl.BlockSpec((tm, d), lambda i: (i, 0)),
            pl.BlockSpec((1, d), lambda i: (0, 0)),
            pl.BlockSpec(wq_t.shape, lambda i: (0, 0)),
            pl.BlockSpec(keys.shape, lambda i: (0, 0, 0)),
        ],
        out_specs=[
            pl.BlockSpec((tm, d), lambda i: (i, 0)),
            pl.BlockSpec((2 * PEER_HEADS, rows, LANES), lambda i: (0, i, 0)),
        ],
        out_shape=[
            jax.ShapeDtypeStruct((t, d), BF16),
            jax.ShapeDtypeStruct((2 * PEER_HEADS, (t // LANES) * SC_PITCH, LANES), F32),
        ],
        compiler_params=_cparams(("parallel",)),
        name="peer_scores",
    )(h, g.reshape(1, d), wq_t, keys)


def _tree(op, xs):
    xs = list(xs)
    while len(xs) > 1:
        xs = [op(xs[i], xs[i + 1]) if i + 1 < len(xs) else xs[i] for i in range(0, len(xs), 2)]
    return xs[0]


def _sorting_network(n):
    def merge(lo, hi, r):
        step = 2 * r
        if step < hi - lo:
            yield from merge(lo, hi, step)
            yield from merge(lo + r, hi, step)
            yield from ((i, i + r) for i in range(lo + r, hi - r, step))
        else:
            yield (lo, lo + r)

    def sort(lo, hi):
        if hi > lo:
            mid = lo + (hi - lo) // 2
            yield from sort(lo, mid)
            yield from sort(mid + 1, hi)
            yield from merge(lo, hi, 1)

    return list(sort(0, n - 1))


_SORT16 = _sorting_network(PEER_TOPK)


def _sort_block(vals, ids):
    vals, ids = list(vals), list(ids)
    for i, j in _SORT16:
        va, vb, ia, ib = vals[i], vals[j], ids[i], ids[j]
        swap = (vb > va) | ((vb == va) & (ib < ia))
        vals[i], vals[j] = jnp.maximum(va, vb), jnp.minimum(va, vb)
        ids[i], ids[j] = jnp.where(swap, ib, ia), jnp.where(swap, ia, ib)
    return vals, ids


def _merge_top16(lists):
    def better(a, b):
        return jnp.maximum(a[0], b[0]), jnp.where(a[0] >= b[0], a[1], b[1])

    lists = [list(entries) for entries in lists]
    vals, idxs = [], []
    for r in range(PEER_TOPK):
        m, pos = _tree(better, [entries[0] for entries in lists if entries])
        vals.append(m)
        idxs.append(pos)
        keep = PEER_TOPK - 1 - r
        owner = jnp.floor(pos * (1.0 / PEER_TOPK))
        for n, entries in enumerate(lists):
            if not entries:
                continue
            won = owner == float(n)
            nxt = entries[1:] + [(NEG_INF, float(PEER_TOPK * n))]
            lists[n] = [(jnp.where(won, b[0], a[0]), jnp.where(won, b[1], a[1]))
                        for a, b in list(zip(entries, nxt))[:keep]]
    return vals, idxs


def _top16(rows):
    blocks = []
    for n in range(len(rows) // PEER_TOPK):
        base = n * PEER_TOPK
        ids = [jnp.full(rows[0].shape, float(base + k), F32) for k in range(PEER_TOPK)]
        vals, ids = _sort_block(rows[base:base + PEER_TOPK], ids)
        blocks.append(list(zip(vals, ids)))
    return _merge_top16(blocks)


def _pick(table, sel):
    out = table[0]
    for i in range(1, PEER_TOPK):
        out = jnp.where(sel == float(i), table[i], out)
    return out


_PAIRS = [(i, j) for i in range(PEER_TOPK) for j in range(PEER_TOPK) if (i + 1) * (j + 1) <= PEER_TOPK]


def _route_head(s1, s2):
    v1, i1 = _top16(s1)
    v2, i2 = _top16(s2)
    sums = [[(v1[i] + v2[j], float(i * PEER_TOPK + j)) for j in range(PEER_TOPK) if (i, j) in _PAIRS]
            for i in range(PEER_TOPK)]
    best, flat = _merge_top16(sums)
    probs = [jnp.exp(b - best[0]) for b in best]
    inv = 1.0 / _tree(jnp.add, probs)
    experts = []
    for r in range(PEER_TOPK):
        hi = jnp.floor(flat[r] * (1.0 / PEER_TOPK))
        experts.append(_pick(i1, hi) * PEER_NKEYS + _pick(i2, flat[r] - hi * PEER_TOPK))
    return experts, [p * inv for p in probs]


def _peer_topk_kernel(sc_ref, e_ref, gate_ref, esel_ref, gsel_ref):
    head = pl.program_id(1)

    def load(grp):
        return [sc_ref[grp, pl.ds(k, SUBLANES, stride=SC_PITCH), :] for k in range(PEER_NKEYS)]

    experts, gates = _route_head(load(0), load(1))
    for r in range(PEER_TOPK):
        row = pl.multiple_of((head * PEER_TOPK + r) * SUBLANES, SUBLANES)
        esel_ref[pl.ds(row, SUBLANES), :] = experts[r]
        gsel_ref[pl.ds(row, SUBLANES), :] = gates[r]

    @pl.when(head == PEER_HEADS - 1)
    def _():
        for s in range(SUBLANES):
            e_sel = esel_ref[pl.ds(s, PEER_SEL, stride=SUBLANES), :]
            g_sel = gsel_ref[pl.ds(s, PEER_SEL, stride=SUBLANES), :]
            e_ref[s * LANES:(s + 1) * LANES, :] = e_sel.T.astype(I32)
            gate_ref[s * LANES:(s + 1) * LANES, :] = g_sel.T


def peer_topk(scores):
    t = scores.shape[1] // SC_PITCH * LANES
    rows = SUBLANES * SC_PITCH
    return pl.pallas_call(
        _peer_topk_kernel,
        grid=(t // TOPK_TOKENS, PEER_HEADS),
        in_specs=[pl.BlockSpec((2, rows, LANES), lambda i, hd: (hd, i, 0))],
        out_specs=[
            pl.BlockSpec((TOPK_TOKENS, PEER_SEL), lambda i, hd: (i, 0)),
            pl.BlockSpec((TOPK_TOKENS, PEER_SEL), lambda i, hd: (i, 0)),
        ],
        out_shape=[
            jax.ShapeDtypeStruct((t, PEER_SEL), I32),
            jax.ShapeDtypeStruct((t, PEER_SEL), F32),
        ],
        scratch_shapes=[
            pltpu.VMEM((PEER_SEL * SUBLANES, LANES), F32),
            pltpu.VMEM((PEER_SEL * SUBLANES, LANES), F32),
        ],
        compiler_params=_cparams(("parallel", "arbitrary")),
        name="peer_topk",
    )(scores)


W_UNROLL = 32
W_SLABS = PEER_NKEYS // SUBLANES
U32 = jnp.uint32
BF16_HI = np.uint32(0xFFFF0000)
HALF_WORD = np.uint32(16)


def _token_gates(e_row, g_row, sub):
    a_hot = jnp.where(sub == lax.shift_right_logical(e_row, 7), 1.0, 0.0).astype(BF16)
    b_hot = jnp.where(sub == (e_row & (PEER_NKEYS - 1)), g_row, 0.0).astype(BF16)
    return lax.dot_general(a_hot, b_hot, (((1,), (1,)), ((), ())), preferred_element_type=F32)


def _bf16_bits(x):
    return lax.bitcast_convert_type(x.astype(BF16).astype(F32), U32)


def _peer_gates_kernel(e_hi_ref, g_hi_ref, e_lo_ref, g_lo_ref, w_ref, *, tw):
    sub = lax.broadcasted_iota(I32, (PEER_NKEYS, PEER_SEL), 0)

    def group(gi, carry):
        p0 = pl.multiple_of(gi * W_UNROLL, W_UNROLL)
        for i in range(W_UNROLL):
            rows = pl.ds(p0 + i, 1)
            w_hi = _token_gates(e_hi_ref[rows, :], g_hi_ref[rows, :], sub)
            w_lo = _token_gates(e_lo_ref[rows, :], g_lo_ref[rows, :], sub)
            packed = _bf16_bits(w_hi) | lax.shift_right_logical(_bf16_bits(w_lo), HALF_WORD)
            row = pl.multiple_of((p0 + i) * SUBLANES, SUBLANES)
            for s in range(W_SLABS):
                w_ref[s, pl.ds(row, SUBLANES), :] = packed[s * SUBLANES:(s + 1) * SUBLANES, :]
        return carry

    lax.fori_loop(0, tw // W_UNROLL, group, 0)


def peer_gates(e, g, tile, tw=64):
    t = e.shape[0]
    half = tile // 2
    tw = min(tw, half)
    per_half = half // tw
    hi = lambda i, c: (i * 2 * per_half + c, 0)
    lo = lambda i, c: (i * 2 * per_half + per_half + c, 0)
    return pl.pallas_call(
        functools.partial(_peer_gates_kernel, tw=tw),
        grid=(t // tile, per_half),
        in_specs=[
            pl.BlockSpec((tw, PEER_SEL), hi),
            pl.BlockSpec((tw, PEER_SEL), hi),
            pl.BlockSpec((tw, PEER_SEL), lo),
            pl.BlockSpec((tw, PEER_SEL), lo),
        ],
        out_specs=pl.BlockSpec((W_SLABS, tw * SUBLANES, PEER_NKEYS), lambda i, c: (0, i * per_half + c, 0)),
        out_shape=jax.ShapeDtypeStruct((W_SLABS, (t // 2) * SUBLANES, PEER_NKEYS), U32),
        compiler_params=_cparams(("parallel", "parallel")),
        name="peer_gates",
    )(e, g, e, g)


DENSE_CHUNK = 512
DENSE_VMEM_LIMIT = 60 * 1024 * 1024


def _peer_dense_kernel(hn_ref, u_ref, w_ref, v_ref, o_ref):
    j = pl.program_id(1)

    @pl.when(j == 0)
    def _():
        o_ref[...] = jnp.zeros_like(o_ref)

    hn = hn_ref[...]
    tm = hn.shape[0]
    te = u_ref.shape[0]
    k1_per_chunk = DENSE_CHUNK // PEER_NKEYS
    steps_per_slab = (SUBLANES * PEER_NKEYS) // te
    k1_base = (j % steps_per_slab) * (te // PEER_NKEYS)
    parts = []
    for c in range(te // DENSE_CHUNK):
        rows = slice(c * DENSE_CHUNK, (c + 1) * DENSE_CHUNK)
        pre = lax.dot_general(hn, u_ref[rows, :], (((1,), (1,)), ((), ())),
                              preferred_element_type=F32)
        act = 0.5 * pre * (1.0 + lax.erf(pre * (2.0 ** -0.5)))
        words = [w_ref[pl.ds(k1_base + c * k1_per_chunk + k, tm // 2, stride=SUBLANES), :]
                 for k in range(k1_per_chunk)]
        first = jnp.concatenate([lax.bitcast_convert_type(x & BF16_HI, F32) for x in words], axis=1)
        second = jnp.concatenate(
            [lax.bitcast_convert_type(lax.shift_left(x, HALF_WORD), F32) for x in words], axis=1)
        gate = jnp.concatenate([first, second], axis=0)
        parts.append((act * gate).astype(BF16))
    weighted = parts[0] if len(parts) == 1 else jnp.concatenate(parts, axis=1)
    o_ref[...] += jnp.dot(weighted, v_ref[...], preferred_element_type=F32)


DENSE_TOKENS = 1024


def peer_dense(hn, u, w, v, layer, tm, te=1024):
    t, d = hn.shape
    n_exp = u.shape[1]
    slab = SUBLANES * PEER_NKEYS
    return pl.pallas_call(
        _peer_dense_kernel,
        grid=(t // tm, n_exp // te),
        in_specs=[
            pl.BlockSpec((tm, d), lambda i, j: (i, 0)),
            pl.BlockSpec((None, te, d), lambda i, j: (layer, j, 0)),
            pl.BlockSpec((None, (tm // 2) * SUBLANES, PEER_NKEYS), lambda i, j: (j * te // slab, i, 0)),
            pl.BlockSpec((None, te, d), lambda i, j: (layer, j, 0)),
        ],
        out_specs=pl.BlockSpec((tm, d), lambda i, j: (i, 0)),
        out_shape=jax.ShapeDtypeStruct((t, d), F32),
        compiler_params=_cparams(("parallel", "arbitrary"), DENSE_VMEM_LIMIT),
        name="peer_dense",
    )(hn, u, w, v)


def _add_rmsnorm_kernel(a_ref, b_ref, g_ref, o_ref):
    o_ref[...] = _rms(a_ref[...] + b_ref[...], g_ref[...])


def add_rmsnorm(a, b, g, tm=512):
    t, d = a.shape
    tm = min(tm, t)
    return pl.pallas_call(
        _add_rmsnorm_kernel,
        grid=(t // tm,),
        in_specs=[
            pl.BlockSpec((tm, d), lambda i: (i, 0)),
            pl.BlockSpec((tm, d), lambda i: (i, 0)),
            pl.BlockSpec((1, d), lambda i: (0, 0)),
        ],
        out_specs=pl.BlockSpec((tm, d), lambda i: (i, 0)),
        out_shape=jax.ShapeDtypeStruct((t, d), F32),
        compiler_params=_cparams(("parallel",)),
        name="add_rmsnorm",
    )(a, b, g.reshape(1, d))


LRU_ROWS = 8


def _lru_kernel(gate_ref, rec_ref, cw_ref, cb_ref, wa_ref, ba_ref, wx_ref, bx_ref, lam_ref, o_ref,
                tail_ref, carry_ref, a_ref, u_ref, hs_ref, *, ts):
    @pl.when(pl.program_id(1) == 0)
    def _():
        tail_ref[...] = jnp.zeros_like(tail_ref)
        carry_ref[...] = jnp.zeros_like(carry_ref)

    x = rec_ref[...].astype(F32)
    xp = jnp.concatenate([tail_ref[...], x], axis=0)
    tail_ref[...] = x[ts - LRU_ROWS:, :]
    xc = cb_ref[...] + jnp.zeros_like(x)
    for j in range(CONV_WIDTH):
        off = LRU_ROWS - (CONV_WIDTH - 1) + j
        xc = xc + cw_ref[j:j + 1, :] * xp[off:off + ts, :]

    soft = jnp.log(1.0 + jnp.exp(-lam_ref[...]))
    for n in range(LRU_BLOCKS):
        sl = slice(n * LRU_BLOCK, (n + 1) * LRU_BLOCK)
        xb = xc[:, sl]
        xb16 = xb.astype(BF16)
        r = jax.nn.sigmoid(jnp.dot(xb16, wa_ref[n], preferred_element_type=F32) + ba_ref[:, sl])
        i = jax.nn.sigmoid(jnp.dot(xb16, wx_ref[n], preferred_element_type=F32) + bx_ref[:, sl])
        log_a = -LRU_C * r * soft[:, sl]
        a = jnp.exp(log_a)
        a_ref[:, sl] = a
        u_ref[:, sl] = jnp.sqrt(1.0 - a * a) * (i * xb)

    def step(c, h):
        r0 = pl.multiple_of(c * LRU_ROWS, LRU_ROWS)
        a8 = a_ref[pl.ds(r0, LRU_ROWS), :]
        u8 = u_ref[pl.ds(r0, LRU_ROWS), :]
        rows = []
        for r in range(LRU_ROWS):
            h = a8[r:r + 1, :] * h + u8[r:r + 1, :]
            rows.append(h)
        hs_ref[pl.ds(r0, LRU_ROWS), :] = jnp.concatenate(rows, axis=0)
        return h

    carry_ref[...] = lax.fori_loop(0, ts // LRU_ROWS, step, carry_ref[...])
    o_ref[...] = (jax.nn.gelu(gate_ref[...].astype(F32)) * hs_ref[...]).astype(BF16)


def lru_branch(z, conv_w, conv_b, wa, ba, wx, bx, lam, batch, seq, ts=256):
    t = z.shape[0]
    w = z.shape[1] // 2
    ts = min(ts, seq)
    per_b = seq // ts
    row = lambda a: pl.BlockSpec(a.shape, lambda b, s: (0, 0))
    cube = lambda a: pl.BlockSpec(a.shape, lambda b, s: (0, 0, 0))
    return pl.pallas_call(
        functools.partial(_lru_kernel, ts=ts),
        grid=(batch, per_b),
        in_specs=[
            pl.BlockSpec((ts, w), lambda b, s: (b * per_b + s, 0)),
            pl.BlockSpec((ts, w), lambda b, s: (b * per_b + s, 1)),
            row(conv_w), row(conv_b), cube(wa), row(ba), cube(wx), row(bx), row(lam),
        ],
        out_specs=pl.BlockSpec((ts, w), lambda b, s: (b * per_b + s, 0)),
        out_shape=jax.ShapeDtypeStruct((t, w), BF16),
        scratch_shapes=[
            pltpu.VMEM((LRU_ROWS, w), F32),
            pltpu.VMEM((1, w), F32),
            pltpu.VMEM((ts, w), F32),
            pltpu.VMEM((ts, w), F32),
            pltpu.VMEM((ts, w), F32),
        ],
        compiler_params=_cparams(("parallel", "arbitrary")),
        name="lru_branch",
    )(z, z, conv_w, conv_b, wa, ba, wx, bx, lam)


def _rope_tables(positions, dim, width):
    inv = ROPE_THETA ** (-jnp.arange(0, dim, 2, dtype=F32) / dim)
    ang = positions.astype(F32).reshape(-1, 1) * inv
    pad = jnp.zeros((ang.shape[0], width - dim), F32)
    cos = jnp.concatenate([jnp.cos(ang), jnp.cos(ang), pad], axis=-1)
    sin = jnp.concatenate([jnp.sin(ang), jnp.sin(ang), pad], axis=-1)
    return cos, sin


def _rotate_half_cols(w):
    half = w.shape[-1] // 2
    return jnp.concatenate([-w[..., half:], w[..., :half]], axis=-1)


def _even_weights(w_in, w_uq):
    d = w_in.shape[0]
    o = [0, 512, 768, 832]
    z64 = jnp.zeros((d, 64), F32)
    k_rope = w_in[:, o[2]:o[3]]
    w_in_p = jnp.concatenate(
        [w_in[:, :o[2]], k_rope, z64, _rotate_half_cols(k_rope), z64, w_in[:, o[3]:]], axis=-1)
    wq3 = w_uq.reshape(MLA_Q_RANK, MLA_HEADS, MLA_NOPE + MLA_ROPE)
    nope, rope = wq3[..., :MLA_NOPE], wq3[..., MLA_NOPE:]
    zq = jnp.zeros((MLA_Q_RANK, MLA_HEADS, 64), F32)
    wqa = jnp.concatenate([nope, rope, zq], axis=-1).reshape(MLA_Q_RANK, MLA_HEADS * MLA_QK)
    wqb = jnp.concatenate([_rotate_half_cols(rope), zq], axis=-1).reshape(MLA_Q_RANK, MLA_HEADS * LANES)
    return w_in_p.astype(BF16), wqa.astype(BF16), wqb.astype(BF16)


def _memory_and_route(h, mem2, layer, seq, xa_norm_g, mem_norm_g, xa_wq, xa_wk, xa_wv, xa_wo,
                      ffn_norm_g, peer_wq, peer_sub_keys):
    w_kv = jnp.concatenate([xa_wk[layer], xa_wv[layer]], axis=-1).astype(BF16)
    memkv = norm_matmul([mem2], mem_norm_g[layer], w_kv)
    h = cross_attention(h, xa_norm_g[layer], xa_wq[layer].astype(BF16), memkv,
                        xa_wo[layer].astype(BF16), seq)
    keys = peer_sub_keys[layer].reshape(2 * PEER_HEADS, PEER_NKEYS, PEER_HALF).astype(BF16)
    hn, scores = peer_scores(h, ffn_norm_g[layer], peer_wq[layer].T.astype(BF16), keys)
    e_sel, g_sel = peer_topk(scores)
    return h, hn, peer_gates(e_sel, g_sel, min(DENSE_TOKENS, h.shape[0]))


def kernel(x, mem, positions, mix_norm_g, xa_norm_g, ffn_norm_g, mem_norm_g, ev_w_in, ev_w_out, mla_q_norm_g, mla_w_uq, mla_kv_norm_g, mla_w_ukv, ret_gn_g, od_w_in, od_w_out, conv_w, conv_b, lru_wa, lru_ba, lru_wx, lru_bx, lru_lambda, xa_wq, xa_wk, xa_wv, xa_wo, peer_wq, peer_sub_keys, peer_u, peer_v, final_norm_g):
    batch, seq, d = x.shape
    t = batch * seq
    h = x.reshape(t, d)
    mem2 = mem.reshape(-1, d)
    tail = (xa_norm_g, mem_norm_g, xa_wq, xa_wk, xa_wv, xa_wo, ffn_norm_g, peer_wq, peer_sub_keys)
    u16 = peer_u.astype(BF16)
    v16 = peer_v.astype(BF16)

    w_in_p, wqa, wqb = _even_weights(ev_w_in[0], mla_w_uq[0])
    z = norm_matmul([h], mix_norm_g[0], w_in_p)
    cos_m, sin_m = _rope_tables(positions, MLA_ROPE, LANES)
    qf, kvf, kr = mla_proj(z, cos_m, sin_m, mla_q_norm_g[0].reshape(1, -1),
                           mla_kv_norm_g[0].reshape(1, -1), wqa, wqb, mla_w_ukv[0].astype(BF16))
    attn = mla_attention(qf, kvf, kr, batch, seq)
    cos_r, sin_r = _rope_tables(positions, RET_DK, RET_DK)
    sign = jnp.concatenate([-jnp.ones((RET_DK // 2,), F32), jnp.ones((RET_DK // 2,), F32)])
    log_g = jnp.log(1.0 - 2.0 ** (-5.0 - jnp.arange(RET_HEADS, dtype=F32)))
    logg = jnp.broadcast_to(log_g[:, None, None], (RET_HEADS, 1, LANES))
    ret = retention(z, cos_r, sin_r * sign, logg, ret_gn_g[0].reshape(RET_HEADS, 1, RET_DV), batch, seq)
    w_out = ev_w_out[0].astype(BF16)
    n_a = MLA_HEADS * MLA_V
    h = matmul_res([attn, ret], [w_out[:n_a], w_out[n_a:]], [h])
    h, hn, w = _memory_and_route(h, mem2, 0, seq, *tail)
    tile = min(DENSE_TOKENS, t)
    ffn = peer_dense(hn, u16, w, v16, 0, tile)

    z = norm_matmul([h, ffn], mix_norm_g[1], od_w_in[0].astype(BF16), tn=512)
    row = lambda a: a.reshape(1, -1)
    cw = jnp.concatenate([conv_w[0], jnp.zeros((8 - CONV_WIDTH, conv_w.shape[-1]), F32)], axis=0)
    y = lru_branch(z, cw, row(conv_b[0]), lru_wa[0].astype(BF16), row(lru_ba[0]),
                   lru_wx[0].astype(BF16), row(lru_bx[0]), row(lru_lambda[0]), batch, seq)
    h = matmul_res([y], [od_w_out[0].astype(BF16)], [h, ffn])
    h, hn, w = _memory_and_route(h, mem2, 1, seq, *tail)
    out = add_rmsnorm(h, peer_dense(hn, u16, w, v16, 1, tile), final_norm_g)
    return out.reshape(batch, seq, d)
```

```python
import functools
import math

import jax
import jax.numpy as jnp
import numpy as np
from jax import lax
from jax.experimental import pallas as pl
from jax.experimental.pallas import tpu as pltpu

F32 = jnp.float32
BF16 = jnp.bfloat16
I32 = jnp.int32

EPS = 1e-6
ROPE_THETA = 10000.0

MLA_HEADS = 8
MLA_NOPE = 128
MLA_ROPE = 64
MLA_V = 128
MLA_Q_RANK = 512
MLA_KV_RANK = 256
MLA_QK = 256

RET_HEADS = 8
RET_DK = 128
RET_DV = 128
RET_BLOCK = 256

LRU_BLOCKS = 16
LRU_BLOCK = 128
CONV_WIDTH = 4
LRU_C = 8.0

XA_HEADS = 4
XA_HEAD_DIM = 128

PEER_HEADS = 8
PEER_NKEYS = 128
PEER_HALF = 128
PEER_TOPK = 16
PEER_SEL = PEER_HEADS * PEER_TOPK

LANES = 128
VMEM_LIMIT = 48 * 1024 * 1024

NEG_INF = float("-inf")


def _cparams(sem, vmem=VMEM_LIMIT):
    return pltpu.CompilerParams(dimension_semantics=sem, vmem_limit_bytes=vmem)


def _rms(x, g):
    ms = jnp.mean(x * x, axis=-1, keepdims=True)
    return x * lax.rsqrt(ms + EPS) * g


def _norm_matmul_kernel(*refs, n_x):
    x_refs = refs[:n_x]
    g_ref, w_ref, o_ref, xn_ref = refs[n_x:]

    @pl.when(pl.program_id(1) == 0)
    def _():
        x = x_refs[0][...]
        for extra in x_refs[1:]:
            x = x + extra[...]
        xn_ref[...] = _rms(x, g_ref[...]).astype(BF16)

    o_ref[...] = jnp.dot(xn_ref[...], w_ref[...], preferred_element_type=F32).astype(o_ref.dtype)


NORM_MATMUL_VMEM_LIMIT = 56 * 1024 * 1024


def norm_matmul(xs, g, w, tm=1024, tn=1024):
    t, d = xs[0].shape
    n = w.shape[1]
    tm = min(tm, t)
    tn = min(tn, n)
    return pl.pallas_call(
        functools.partial(_norm_matmul_kernel, n_x=len(xs)),
        grid=(t // tm, n // tn),
        in_specs=[pl.BlockSpec((tm, d), lambda i, j: (i, 0)) for _ in xs] + [
            pl.BlockSpec((1, d), lambda i, j: (0, 0)),
            pl.BlockSpec((d, tn), lambda i, j: (0, j)),
        ],
        out_specs=pl.BlockSpec((tm, tn), lambda i, j: (i, j)),
        out_shape=jax.ShapeDtypeStruct((t, n), BF16),
        scratch_shapes=[pltpu.VMEM((tm, d), BF16)],
        compiler_params=_cparams(("parallel", "arbitrary"), NORM_MATMUL_VMEM_LIMIT),
        name="norm_matmul",
    )(*xs, g.reshape(1, d), w)


def _matmul_res_kernel(*refs, n_in):
    xs = refs[:n_in]
    ws = refs[n_in:2 * n_in]
    res_refs = refs[2 * n_in:-1]
    o_ref = refs[-1]
    acc = res_refs[0][...]
    for res_ref in res_refs[1:]:
        acc = acc + res_ref[...]
    for x_ref, w_ref in zip(xs, ws):
        acc = acc + jnp.dot(x_ref[...], w_ref[...], preferred_element_type=F32)
    o_ref[...] = acc


def matmul_res(xs, ws, ress, tm=512, tn=2048):
    t, n = ress[0].shape
    tm = min(tm, t)
    tn = min(tn, n)
    n_in = len(xs)
    in_specs = [pl.BlockSpec((tm, x.shape[1]), lambda i, j: (i, 0)) for x in xs]
    in_specs += [pl.BlockSpec((w.shape[0], tn), lambda i, j: (0, j)) for w in ws]
    in_specs += [pl.BlockSpec((tm, tn), lambda i, j: (i, j)) for _ in ress]
    return pl.pallas_call(
        functools.partial(_matmul_res_kernel, n_in=n_in),
        grid=(t // tm, n // tn),
        in_specs=in_specs,
        out_specs=pl.BlockSpec((tm, tn), lambda i, j: (i, j)),
        out_shape=jax.ShapeDtypeStruct((t, n), F32),
        compiler_params=_cparams(("parallel", "parallel")),
        name="matmul_res",
    )(*xs, *ws, *ress)


def _mla_proj_kernel(z_ref, cos_ref, sin_ref, qg_ref, kvg_ref, wqa_ref, wqb_ref, wkv_ref,
                     q_ref, kv_ref, kr_ref):
    scale = (MLA_NOPE + MLA_ROPE) ** -0.5
    z = z_ref[...]
    cq = _rms(z[:, :MLA_Q_RANK].astype(F32), qg_ref[...]).astype(BF16)
    ckv = _rms(z[:, MLA_Q_RANK:MLA_Q_RANK + MLA_KV_RANK].astype(F32), kvg_ref[...]).astype(BF16)
    cos = cos_ref[...]
    sin = sin_ref[...]
    qa = jnp.dot(cq, wqa_ref[...], preferred_element_type=F32)
    qb = jnp.dot(cq, wqb_ref[...], preferred_element_type=F32)
    for h in range(MLA_HEADS):
        base = h * MLA_QK
        q_ref[:, base:base + LANES] = (qa[:, base:base + LANES] * scale).astype(BF16)
        rot = qa[:, base + LANES:base + 2 * LANES] * cos + qb[:, h * LANES:(h + 1) * LANES] * sin
        q_ref[:, base + LANES:base + 2 * LANES] = (rot * scale).astype(BF16)
    kv_ref[...] = jnp.dot(ckv, wkv_ref[...], preferred_element_type=F32).astype(BF16)
    off = MLA_Q_RANK + MLA_KV_RANK
    kr = z[:, off:off + LANES].astype(F32) * cos + z[:, off + LANES:off + 2 * LANES].astype(F32) * sin
    kr_ref[...] = kr.astype(BF16)


def mla_proj(z, cos_t, sin_t, qg, kvg, wqa, wqb, wkv, tm=512):
    t = z.shape[0]
    tm = min(tm, t)
    hq = MLA_HEADS * MLA_QK
    full = lambda a: pl.BlockSpec(a.shape, lambda i: (0, 0))
    return pl.pallas_call(
        _mla_proj_kernel,
        grid=(t // tm,),
        in_specs=[
            pl.BlockSpec((tm, 1024), lambda i: (i, 0)),
            pl.BlockSpec((tm, LANES), lambda i: (i, 0)),
            pl.BlockSpec((tm, LANES), lambda i: (i, 0)),
            full(qg), full(kvg), full(wqa), full(wqb), full(wkv),
        ],
        out_specs=[
            pl.BlockSpec((tm, hq), lambda i: (i, 0)),
            pl.BlockSpec((tm, hq), lambda i: (i, 0)),
            pl.BlockSpec((tm, LANES), lambda i: (i, 0)),
        ],
        out_shape=[
            jax.ShapeDtypeStruct((t, hq), BF16),
            jax.ShapeDtypeStruct((t, hq), BF16),
            jax.ShapeDtypeStruct((t, LANES), BF16),
        ],
        compiler_params=_cparams(("parallel",)),
        name="mla_proj",
    )(z, cos_t, sin_t, qg, kvg, wqa, wqb, wkv)


def _mla_attn_kernel(q_ref, kv_ref, kr_ref, o_ref, kcat_ref, *, seq, tq):
    kcat_ref[:, :LANES] = kv_ref[:, :LANES]
    kcat_ref[:, LANES:] = kr_ref[...]
    for qi in range(seq // tq):
        kend = (qi + 1) * tq
        q = q_ref[qi * tq:(qi + 1) * tq, :]
        s = lax.dot_general(q, kcat_ref[:kend, :], (((1,), (1,)), ((), ())),
                            preferred_element_type=F32)
        row = lax.broadcasted_iota(I32, (tq, kend), 0) + qi * tq
        col = lax.broadcasted_iota(I32, (tq, kend), 1)
        s = jnp.where(col <= row, s, NEG_INF)
        m = jnp.max(s, axis=-1, keepdims=True)
        p = jnp.exp(s - m)
        l = jnp.sum(p, axis=-1, keepdims=True)
        o = jnp.dot(p.astype(BF16), kv_ref[:kend, LANES:], preferred_element_type=F32)
        o_ref[qi * tq:(qi + 1) * tq, :] = (o / l).astype(BF16)


def mla_attention(qf, kvf, kr, batch, seq, tq=256):
    tq = min(tq, seq)
    t = batch * seq
    return pl.pallas_call(
        functools.partial(_mla_attn_kernel, seq=seq, tq=tq),
        grid=(batch, MLA_HEADS),
        in_specs=[
            pl.BlockSpec((seq, MLA_QK), lambda b, h: (b, h)),
            pl.BlockSpec((seq, MLA_QK), lambda b, h: (b, h)),
            pl.BlockSpec((seq, LANES), lambda b, h: (b, 0)),
        ],
        out_specs=pl.BlockSpec((seq, MLA_V), lambda b, h: (b, h)),
        out_shape=jax.ShapeDtypeStruct((t, MLA_HEADS * MLA_V), BF16),
        scratch_shapes=[pltpu.VMEM((seq, MLA_QK), BF16)],
        compiler_params=_cparams(("parallel", "parallel")),
        name="mla_attention",
    )(qf, kvf, kr)


def _retention_kernel(q_ref, k_ref, v_ref, g_ref, cos_ref, sin_ref, logg_ref, gn_ref, o_ref,
                      *, seq, blk):
    lg = logg_ref[...]
    cos = cos_ref[...]
    sin = sin_ref[...]
    half = RET_DK // 2

    def rope(x):
        return x * cos + pltpu.roll(x, half, 1) * sin

    q_all = rope(q_ref[...].astype(F32))
    k_all = rope(k_ref[...].astype(F32)) * (RET_DK ** -0.5)

    ri = lax.broadcasted_iota(I32, (blk, blk), 0)
    ci = lax.broadcasted_iota(I32, (blk, blk), 1)
    diff = (ri - ci).astype(F32)
    decay = jnp.where(diff >= 0, jnp.exp(lg[:, :1] * jnp.maximum(diff, 0.0)), 0.0)
    pos = lax.broadcasted_iota(I32, (blk, RET_DK), 0).astype(F32)
    xi = jnp.exp(lg * (pos + 1.0))
    zeta = jnp.exp(lg * (blk - 1.0 - pos))
    blk_decay = jnp.exp(lg * float(blk))

    state = jnp.zeros((RET_DK, RET_DV), F32)
    for n in range(seq // blk):
        sl = slice(n * blk, (n + 1) * blk)
        qc = q_all[sl]
        kc = k_all[sl]
        vc = v_ref[sl, :]
        att = lax.dot_general(qc.astype(BF16), kc.astype(BF16), (((1,), (1,)), ((), ())),
                              preferred_element_type=F32) * decay
        inner = jnp.dot(att.astype(BF16), vc, preferred_element_type=F32)
        cross = jnp.dot((qc * xi).astype(BF16), state.astype(BF16), preferred_element_type=F32)
        ret = inner + cross
        kz_t = (kc * zeta).T.astype(BF16)
        dstate = jnp.dot(kz_t, vc, preferred_element_type=F32)
        state = state * blk_decay + dstate
        mu = jnp.mean(ret, axis=-1, keepdims=True)
        cen = ret - mu
        var = jnp.mean(cen * cen, axis=-1, keepdims=True)
        normed = cen * lax.rsqrt(var + EPS) * gn_ref[...]
        gate = g_ref[sl, :].astype(F32)
        o_ref[sl, :] = (gate * jax.nn.sigmoid(gate) * normed).astype(BF16)


def retention(z, cos_t, sin_t, logg, gn_g, batch, seq):
    blk = min(RET_BLOCK, seq)
    t = batch * seq
    first = 1024 // LANES
    col = lambda part: (lambda b, h: (b, first + part * RET_HEADS + h))
    return pl.pallas_call(
        functools.partial(_retention_kernel, seq=seq, blk=blk),
        grid=(batch, RET_HEADS),
        in_specs=[
            pl.BlockSpec((seq, LANES), col(0)),
            pl.BlockSpec((seq, LANES), col(1)),
            pl.BlockSpec((seq, LANES), col(2)),
            pl.BlockSpec((seq, LANES), col(3)),
            pl.BlockSpec((seq, LANES), lambda b, h: (b, 0)),
            pl.BlockSpec((seq, LANES), lambda b, h: (b, 0)),
            pl.BlockSpec((None, 1, LANES), lambda b, h: (h, 0, 0)),
            pl.BlockSpec((None, 1, LANES), lambda b, h: (h, 0, 0)),
        ],
        out_specs=pl.BlockSpec((seq, RET_DV), lambda b, h: (b, h)),
        out_shape=jax.ShapeDtypeStruct((t, RET_HEADS * RET_DV), BF16),
        compiler_params=_cparams(("parallel", "parallel")),
        name="retention",
    )(z, z, z, z, cos_t, sin_t, logg, gn_g)


def _xattn_kernel(h_ref, g_ref, wq_ref, kv_ref, wo_ref, o_ref):
    h = h_ref[...]
    hn = _rms(h, g_ref[...]).astype(BF16)
    q = jnp.dot(hn, wq_ref[...], preferred_element_type=F32) * (XA_HEAD_DIM ** -0.5)
    q = q.astype(BF16)
    width = XA_HEADS * XA_HEAD_DIM
    outs = []
    for hd in range(XA_HEADS):
        sl = slice(hd * XA_HEAD_DIM, (hd + 1) * XA_HEAD_DIM)
        k = kv_ref[:, sl]
        v = kv_ref[:, width + hd * XA_HEAD_DIM:width + (hd + 1) * XA_HEAD_DIM]
        s = lax.dot_general(q[:, sl], k, (((1,), (1,)), ((), ())), preferred_element_type=F32)
        m = jnp.max(s, axis=-1, keepdims=True)
        p = jnp.exp(s - m)
        l = jnp.sum(p, axis=-1, keepdims=True)
        o = jnp.dot(p.astype(BF16), v, preferred_element_type=F32) / l
        outs.append(o.astype(BF16))
    o_all = jnp.concatenate(outs, axis=-1)
    o_ref[...] = h + jnp.dot(o_all, wo_ref[...], preferred_element_type=F32)


def cross_attention(h, g, wq, memkv, wo, seq, tm=512):
    t, d = h.shape
    tm = min(tm, seq)
    per_b = seq // tm
    m = memkv.shape[0] // (t // seq)
    full = lambda a: pl.BlockSpec(a.shape, lambda i: (0, 0))
    return pl.pallas_call(
        _xattn_kernel,
        grid=(t // tm,),
        in_specs=[
            pl.BlockSpec((tm, d), lambda i: (i, 0)),
            pl.BlockSpec((1, d), lambda i: (0, 0)),
            full(wq),
            pl.BlockSpec((m, memkv.shape[1]), lambda i: (i // per_b, 0)),
            full(wo),
        ],
        out_specs=pl.BlockSpec((tm, d), lambda i: (i, 0)),
        out_shape=jax.ShapeDtypeStruct((t, d), F32),
        compiler_params=_cparams(("parallel",)),
        name="cross_attention",
    )(h, g.reshape(1, d), wq, memkv, wo)


SC_PITCH = PEER_NKEYS + 8
SUBLANES = 8
TOPK_TOKENS = SUBLANES * LANES


def _peer_scores_kernel(h_ref, g_ref, wqt_ref, keys_ref, hn_ref, sc_ref, *, tm):
    hn = _rms(h_ref[...], g_ref[...]).astype(BF16)
    hn_ref[...] = hn
    q_t = lax.dot_general(wqt_ref[...], hn, (((1,), (1,)), ((), ())),
                          preferred_element_type=F32).astype(BF16)
    pad = jnp.zeros((SC_PITCH - PEER_NKEYS, LANES), F32)
    for grp in range(2 * PEER_HEADS):
        s = jnp.dot(keys_ref[grp], q_t[grp * PEER_HALF:(grp + 1) * PEER_HALF, :],
                    preferred_element_type=F32)
        for c in range(tm // LANES):
            sc_ref[grp, c * SC_PITCH:c * SC_PITCH + PEER_NKEYS, :] = s[:, c * LANES:(c + 1) * LANES]
            sc_ref[grp, c * SC_PITCH + PEER_NKEYS:(c + 1) * SC_PITCH, :] = pad


def peer_scores(h, g, wq_t, keys, tm=512):
    t, d = h.shape
    tm = min(tm, t)
    rows = (tm // LANES) * SC_PITCH
    return pl.pallas_call(
        functools.partial(_peer_scores_kernel, tm=tm),
        grid=(t // tm,),
        in_specs=[
            pl.BlockSpec((tm, d), lambda i: (i, 0)),
            pl.BlockSpec((1, d), lambda i: (0, 0)),
            pl.BlockSpec(wq_t.shape, lambda i: (0, 0)),
            pl.BlockSpec(keys.shape, lambda i: (0, 0, 0)),
        ],
        out_specs=[
            pl.BlockSpec((tm, d), lambda i: (i, 0)),
            pl.BlockSpec((2 * PEER_HEADS, rows, LANES), lambda i: (0, i, 0)),
        ],
        out_shape=[
            jax.ShapeDtypeStruct((t, d), BF16),
            jax.ShapeDtypeStruct((2 * PEER_HEADS, (t // LANES) * SC_PITCH, LANES), F32),
        ],
        compiler_params=_cparams(("parallel",)),
        name="peer_scores",
    )(h, g.reshape(1, d), wq_t, keys)


def _tree(op, xs):
    xs = list(xs)
    while len(xs) > 1:
        xs = [op(xs[i], xs[i + 1]) if i + 1 < len(xs) else xs[i] for i in range(0, len(xs), 2)]
    return xs[0]


def _sorting_network(n):
    def merge(lo, hi, r):
        step = 2 * r
        if step < hi - lo:
            yield from merge(lo, hi, step)
            yield from merge(lo + r, hi, step)
            yield from ((i, i + r) for i in range(lo + r, hi - r, step))
        else:
            yield (lo, lo + r)

    def sort(lo, hi):
        if hi > lo:
            mid = lo + (hi - lo) // 2
            yield from sort(lo, mid)
            yield from sort(mid + 1, hi)
            yield from merge(lo, hi, 1)

    return list(sort(0, n - 1))


_SORT16 = _sorting_network(PEER_TOPK)


def _sort_block(vals, ids):
    vals, ids = list(vals), list(ids)
    for i, j in _SORT16:
        va, vb, ia, ib = vals[i], vals[j], ids[i], ids[j]
        swap = (vb > va) | ((vb == va) & (ib < ia))
        vals[i], vals[j] = jnp.maximum(va, vb), jnp.minimum(va, vb)
        ids[i], ids[j] = jnp.where(swap, ib, ia), jnp.where(swap, ia, ib)
    return vals, ids


def _merge_top16(lists):
    def better(a, b):
        return jnp.maximum(a[0], b[0]), jnp.where(a[0] >= b[0], a[1], b[1])

    lists = [list(entries) for entries in lists]
    vals, idxs = [], []
    for r in range(PEER_TOPK):
        m, pos = _tree(better, [entries[0] for entries in lists if entries])
        vals.append(m)
        idxs.append(pos)
        keep = PEER_TOPK - 1 - r
        owner = jnp.floor(pos * (1.0 / PEER_TOPK))
        for n, entries in enumerate(lists):
            if not entries:
                continue
            won = owner == float(n)
            nxt = entries[1:] + [(NEG_INF, float(PEER_TOPK * n))]
            lists[n] = [(jnp.where(won, b[0], a[0]), jnp.where(won, b[1], a[1]))
                        for a, b in list(zip(entries, nxt))[:keep]]
    return vals, idxs


def _top16(rows):
    blocks = []
    for n in range(len(rows) // PEER_TOPK):
        base = n * PEER_TOPK
        ids = [jnp.full(rows[0].shape, float(base + k), F32) for k in range(PEER_TOPK)]
        vals, ids = _sort_block(rows[base:base + PEER_TOPK], ids)
        blocks.append(list(zip(vals, ids)))
    return _merge_top16(blocks)


def _pick(table, sel):
    out = table[0]
    for i in range(1, PEER_TOPK):
        out = jnp.where(sel == float(i), table[i], out)
    return out


_PAIRS = [(i, j) for i in range(PEER_TOPK) for j in range(PEER_TOPK) if (i + 1) * (j + 1) <= PEER_TOPK]


def _route_head(s1, s2):
    v1, i1 = _top16(s1)
    v2, i2 = _top16(s2)
    sums = [[(v1[i] + v2[j], float(i * PEER_TOPK + j)) for j in range(PEER_TOPK) if (i, j) in _PAIRS]
            for i in range(PEER_TOPK)]
    best, flat = _merge_top16(sums)
    probs = [jnp.exp(b - best[0]) for b in best]
    inv = 1.0 / _tree(jnp.add, probs)
    experts = []
    for r in range(PEER_TOPK):
        hi = jnp.floor(flat[r] * (1.0 / PEER_TOPK))
        experts.append(_pick(i1, hi) * PEER_NKEYS + _pick(i2, flat[r] - hi * PEER_TOPK))
    return experts, [p * inv for p in probs]


def _peer_topk_kernel(sc_ref, e_ref, gate_ref, esel_ref, gsel_ref):
    head = pl.program_id(1)

    def load(grp):
        return [sc_ref[grp, pl.ds(k, SUBLANES, stride=SC_PITCH), :] for k in range(PEER_NKEYS)]

    experts, gates = _route_head(load(0), load(1))
    for r in range(PEER_TOPK):
        row = pl.multiple_of((head * PEER_TOPK + r) * SUBLANES, SUBLANES)
        esel_ref[pl.ds(row, SUBLANES), :] = experts[r]
        gsel_ref[pl.ds(row, SUBLANES), :] = gates[r]

    @pl.when(head == PEER_HEADS - 1)
    def _():
        for s in range(SUBLANES):
            e_sel = esel_ref[pl.ds(s, PEER_SEL, stride=SUBLANES), :]
            g_sel = gsel_ref[pl.ds(s, PEER_SEL, stride=SUBLANES), :]
            e_ref[s * LANES:(s + 1) * LANES, :] = e_sel.T.astype(I32)
            gate_ref[s * LANES:(s + 1) * LANES, :] = g_sel.T


def peer_topk(scores):
    t = scores.shape[1] // SC_PITCH * LANES
    rows = SUBLANES * SC_PITCH
    return pl.pallas_call(
        _peer_topk_kernel,
        grid=(t // TOPK_TOKENS, PEER_HEADS),
        in_specs=[pl.BlockSpec((2, rows, LANES), lambda i, hd: (hd, i, 0))],
        out_specs=[
            pl.BlockSpec((TOPK_TOKENS, PEER_SEL), lambda i, hd: (i, 0)),
            pl.BlockSpec((TOPK_TOKENS, PEER_SEL), lambda i, hd: (i, 0)),
        ],
        out_shape=[
            jax.ShapeDtypeStruct((t, PEER_SEL), I32),
            jax.ShapeDtypeStruct((t, PEER_SEL), F32),
        ],
        scratch_shapes=[
            pltpu.VMEM((PEER_SEL * SUBLANES, LANES), F32),
            pltpu.VMEM((PEER_SEL * SUBLANES, LANES), F32),
        ],
        compiler_params=_cparams(("parallel", "arbitrary")),
        name="peer_topk",
    )(scores)


W_UNROLL = 32
W_SLABS = PEER_NKEYS // SUBLANES
U32 = jnp.uint32
BF16_HI = np.uint32(0xFFFF0000)
HALF_WORD = np.uint32(16)


def _token_gates(e_row, g_row, sub):
    a_hot = jnp.where(sub == lax.shift_right_logical(e_row, 7), 1.0, 0.0).astype(BF16)
    b_hot = jnp.where(sub == (e_row & (PEER_NKEYS - 1)), g_row, 0.0).astype(BF16)
    return lax.dot_general(a_hot, b_hot, (((1,), (1,)), ((), ())), preferred_element_type=F32)


def _bf16_bits(x):
    return lax.bitcast_convert_type(x.astype(BF16).astype(F32), U32)


def _peer_gates_kernel(e_hi_ref, g_hi_ref, e_lo_ref, g_lo_ref, w_ref, *, tw):
    sub = lax.broadcasted_iota(I32, (PEER_NKEYS, PEER_SEL), 0)

    def group(gi, carry):
        p0 = pl.multiple_of(gi * W_UNROLL, W_UNROLL)
        for i in range(W_UNROLL):
            rows = pl.ds(p0 + i, 1)
            w_hi = _token_gates(e_hi_ref[rows, :], g_hi_ref[rows, :], sub)
            w_lo = _token_gates(e_lo_ref[rows, :], g_lo_ref[rows, :], sub)
            packed = _bf16_bits(w_hi) | lax.shift_right_logical(_bf16_bits(w_lo), HALF_WORD)
            row = pl.multiple_of((p0 + i) * SUBLANES, SUBLANES)
            for s in range(W_SLABS):
                w_ref[s, pl.ds(row, SUBLANES), :] = packed[s * SUBLANES:(s + 1) * SUBLANES, :]
        return carry

    lax.fori_loop(0, tw // W_UNROLL, group, 0)


def peer_gates(e, g, tile, tw=128):
    t = e.shape[0]
    half = tile // 2
    tw = min(tw, half)
    per_half = half // tw
    hi = lambda i, c: (i * 2 * per_half + c, 0)
    lo = lambda i, c: (i * 2 * per_half + per_half + c, 0)
    return pl.pallas_call(
        functools.partial(_peer_gates_kernel, tw=tw),
        grid=(t // tile, per_half),
        in_specs=[
            pl.BlockSpec((tw, PEER_SEL), hi),
            pl.BlockSpec((tw, PEER_SEL), hi),
            pl.BlockSpec((tw, PEER_SEL), lo),
            pl.BlockSpec((tw, PEER_SEL), lo),
        ],
        out_specs=pl.BlockSpec((W_SLABS, tw * SUBLANES, PEER_NKEYS), lambda i, c: (0, i * per_half + c, 0)),
        out_shape=jax.ShapeDtypeStruct((W_SLABS, (t // 2) * SUBLANES, PEER_NKEYS), U32),
        compiler_params=_cparams(("parallel", "parallel")),
        name="peer_gates",
    )(e, g, e, g)


DENSE_CHUNK = 512
DENSE_VMEM_LIMIT = 60 * 1024 * 1024


def _peer_dense_kernel(hn_ref, u_ref, w_ref, v_ref, o_ref):
    j = pl.program_id(1)

    @pl.when(j == 0)
    def _():
        o_ref[...] = jnp.zeros_like(o_ref)

    hn = hn_ref[...]
    tm = hn.shape[0]
    te = u_ref.shape[0]
    k1_per_chunk = DENSE_CHUNK // PEER_NKEYS
    steps_per_slab = (SUBLANES * PEER_NKEYS) // te
    k1_base = (j % steps_per_slab) * (te // PEER_NKEYS)
    parts = []
    for c in range(te // DENSE_CHUNK):
        rows = slice(c * DENSE_CHUNK, (c + 1) * DENSE_CHUNK)
        pre = lax.dot_general(hn, u_ref[rows, :], (((1,), (1,)), ((), ())),
                              preferred_element_type=F32)
        act = 0.5 * pre * (1.0 + lax.erf(pre * (2.0 ** -0.5)))
        words = [w_ref[pl.ds(k1_base + c * k1_per_chunk + k, tm // 2, stride=SUBLANES), :]
                 for k in range(k1_per_chunk)]
        first = jnp.concatenate([lax.bitcast_convert_type(x & BF16_HI, F32) for x in words], axis=1)
        second = jnp.concatenate(
            [lax.bitcast_convert_type(lax.shift_left(x, HALF_WORD), F32) for x in words], axis=1)
        gate = jnp.concatenate([first, second], axis=0)
        parts.append((act * gate).astype(BF16))
    weighted = parts[0] if len(parts) == 1 else jnp.concatenate(parts, axis=1)
    o_ref[...] += jnp.dot(weighted, v_ref[...], preferred_element_type=F32)


DENSE_TOKENS = 1024


def peer_dense(hn, u, w, v, layer, tm, te=1024):
    t, d = hn.shape
    n_exp = u.shape[1]
    slab = SUBLANES * PEER_NKEYS
    return pl.pallas_call(
        _peer_dense_kernel,
        grid=(t // tm, n_exp // te),
        in_specs=[
            pl.BlockSpec((tm, d), lambda i, j: (i, 0)),
            pl.BlockSpec((None, te, d), lambda i, j: (layer, j, 0)),
            pl.BlockSpec((None, (tm // 2) * SUBLANES, PEER_NKEYS), lambda i, j: (j * te // slab, i, 0)),
            pl.BlockSpec((None, te, d), lambda i, j: (layer, j, 0)),
        ],
        out_specs=pl.BlockSpec((tm, d), lambda i, j: (i, 0)),
        out_shape=jax.ShapeDtypeStruct((t, d), F32),
        compiler_params=_cparams(("parallel", "arbitrary"), DENSE_VMEM_LIMIT),
        name="peer_dense",
    )(hn, u, w, v)


def _add_rmsnorm_kernel(a_ref, b_ref, g_ref, o_ref):
    o_ref[...] = _rms(a_ref[...] + b_ref[...], g_ref[...])


def add_rmsnorm(a, b, g, tm=512):
    t, d = a.shape
    tm = min(tm, t)
    return pl.pallas_call(
        _add_rmsnorm_kernel,
        grid=(t // tm,),
        in_specs=[
            pl.BlockSpec((tm, d), lambda i: (i, 0)),
            pl.BlockSpec((tm, d), lambda i: (i, 0)),
            pl.BlockSpec((1, d), lambda i: (0, 0)),
        ],
        out_specs=pl.BlockSpec((tm, d), lambda i: (i, 0)),
        out_shape=jax.ShapeDtypeStruct((t, d), F32),
        compiler_params=_cparams(("parallel",)),
        name="add_rmsnorm",
    )(a, b, g.reshape(1, d))


LRU_ROWS = 8


def _lru_kernel(gate_ref, rec_ref, cw_ref, cb_ref, wa_ref, ba_ref, wx_ref, bx_ref, lam_ref, o_ref,
                tail_ref, carry_ref, a_ref, u_ref, hs_ref, *, ts):
    @pl.when(pl.program_id(1) == 0)
    def _():
        tail_ref[...] = jnp.zeros_like(tail_ref)
        carry_ref[...] = jnp.zeros_like(carry_ref)

    x = rec_ref[...].astype(F32)
    xp = jnp.concatenate([tail_ref[...], x], axis=0)
    tail_ref[...] = x[ts - LRU_ROWS:, :]
    xc = cb_ref[...] + cw_ref[CONV_WIDTH - 1:CONV_WIDTH, :] * x
    for back in range(1, CONV_WIDTH):
        shifted = pltpu.roll(xp, back, 0)[LRU_ROWS:, :]
        xc = xc + cw_ref[CONV_WIDTH - 1 - back:CONV_WIDTH - back, :] * shifted

    soft = jnp.log(1.0 + jnp.exp(-lam_ref[...]))
    for n in range(LRU_BLOCKS):
        sl = slice(n * LRU_BLOCK, (n + 1) * LRU_BLOCK)
        xb = xc[:, sl]
        xb16 = xb.astype(BF16)
        r = jax.nn.sigmoid(jnp.dot(xb16, wa_ref[n], preferred_element_type=F32) + ba_ref[:, sl])
        i = jax.nn.sigmoid(jnp.dot(xb16, wx_ref[n], preferred_element_type=F32) + bx_ref[:, sl])
        log_a = -LRU_C * r * soft[:, sl]
        a = jnp.exp(log_a)
        a_ref[:, sl] = a
        u_ref[:, sl] = jnp.sqrt(1.0 - a * a) * (i * xb)

    groups = ts // LRU_ROWS
    a = a_ref[...].reshape(groups, LRU_ROWS, -1)
    u = u_ref[...].reshape(groups, LRU_ROWS, -1)
    sub = lax.broadcasted_iota(I32, a.shape, 1)
    for shift in (1, 2, 4):
        ok = sub >= shift
        u = u + a * jnp.where(ok, pltpu.roll(u, shift, 1), 0.0)
        a = a * jnp.where(ok, pltpu.roll(a, shift, 1), 1.0)
    a_ref[...] = a.reshape(ts, -1)
    u_ref[...] = u.reshape(ts, -1)

    def step(c, h):
        r0 = pl.multiple_of(c * LRU_ROWS, LRU_ROWS)
        h8 = u_ref[pl.ds(r0, LRU_ROWS), :] + a_ref[pl.ds(r0, LRU_ROWS), :] * h
        hs_ref[pl.ds(r0, LRU_ROWS), :] = h8
        return h8[LRU_ROWS - 1:, :]

    carry_ref[...] = lax.fori_loop(0, ts // LRU_ROWS, step, carry_ref[...])
    o_ref[...] = (jax.nn.gelu(gate_ref[...].astype(F32)) * hs_ref[...]).astype(BF16)


def lru_branch(z, conv_w, conv_b, wa, ba, wx, bx, lam, batch, seq, ts=256):
    t = z.shape[0]
    w = z.shape[1] // 2
    ts = min(ts, seq)
    per_b = seq // ts
    row = lambda a: pl.BlockSpec(a.shape, lambda b, s: (0, 0))
    cube = lambda a: pl.BlockSpec(a.shape, lambda b, s: (0, 0, 0))
    return pl.pallas_call(
        functools.partial(_lru_kernel, ts=ts),
        grid=(batch, per_b),
        in_specs=[
            pl.BlockSpec((ts, w), lambda b, s: (b * per_b + s, 0)),
            pl.BlockSpec((ts, w), lambda b, s: (b * per_b + s, 1)),
            row(conv_w), row(conv_b), cube(wa), row(ba), cube(wx), row(bx), row(lam),
        ],
        out_specs=pl.BlockSpec((ts, w), lambda b, s: (b * per_b + s, 0)),
        out_shape=jax.ShapeDtypeStruct((t, w), BF16),
        scratch_shapes=[
            pltpu.VMEM((LRU_ROWS, w), F32),
            pltpu.VMEM((1, w), F32),
            pltpu.VMEM((ts, w), F32),
            pltpu.VMEM((ts, w), F32),
            pltpu.VMEM((ts, w), F32),
        ],
        compiler_params=_cparams(("parallel", "arbitrary")),
        name="lru_branch",
    )(z, z, conv_w, conv_b, wa, ba, wx, bx, lam)


def _rope_tables(positions, dim, width):
    inv = ROPE_THETA ** (-jnp.arange(0, dim, 2, dtype=F32) / dim)
    ang = positions.astype(F32).reshape(-1, 1) * inv
    pad = jnp.zeros((ang.shape[0], width - dim), F32)
    cos = jnp.concatenate([jnp.cos(ang), jnp.cos(ang), pad], axis=-1)
    sin = jnp.concatenate([jnp.sin(ang), jnp.sin(ang), pad], axis=-1)
    return cos, sin


def _rotate_half_cols(w):
    half = w.shape[-1] // 2
    return jnp.concatenate([-w[..., half:], w[..., :half]], axis=-1)


def _even_weights(w_in, w_uq):
    d = w_in.shape[0]
    o = [0, 512, 768, 832]
    z64 = jnp.zeros((d, 64), F32)
    k_rope = w_in[:, o[2]:o[3]]
    w_in_p = jnp.concatenate(
        [w_in[:, :o[2]], k_rope, z64, _rotate_half_cols(k_rope), z64, w_in[:, o[3]:]], axis=-1)
    wq3 = w_uq.reshape(MLA_Q_RANK, MLA_HEADS, MLA_NOPE + MLA_ROPE)
    nope, rope = wq3[..., :MLA_NOPE], wq3[..., MLA_NOPE:]
    zq = jnp.zeros((MLA_Q_RANK, MLA_HEADS, 64), F32)
    wqa = jnp.concatenate([nope, rope, zq], axis=-1).reshape(MLA_Q_RANK, MLA_HEADS * MLA_QK)
    wqb = jnp.concatenate([_rotate_half_cols(rope), zq], axis=-1).reshape(MLA_Q_RANK, MLA_HEADS * LANES)
    return w_in_p.astype(BF16), wqa.astype(BF16), wqb.astype(BF16)


def _memory_and_route(h, mem2, layer, seq, xa_norm_g, mem_norm_g, xa_wq, xa_wk, xa_wv, xa_wo,
                      ffn_norm_g, peer_wq, peer_sub_keys):
    w_kv = jnp.concatenate([xa_wk[layer], xa_wv[layer]], axis=-1).astype(BF16)
    memkv = norm_matmul([mem2], mem_norm_g[layer], w_kv)
    h = cross_attention(h, xa_norm_g[layer], xa_wq[layer].astype(BF16), memkv,
                        xa_wo[layer].astype(BF16), seq)
    keys = peer_sub_keys[layer].reshape(2 * PEER_HEADS, PEER_NKEYS, PEER_HALF).astype(BF16)
    hn, scores = peer_scores(h, ffn_norm_g[layer], peer_wq[layer].T.astype(BF16), keys)
    e_sel, g_sel = peer_topk(scores)
    return h, hn, peer_gates(e_sel, g_sel, min(DENSE_TOKENS, h.shape[0]))


def kernel(x, mem, positions, mix_norm_g, xa_norm_g, ffn_norm_g, mem_norm_g, ev_w_in, ev_w_out, mla_q_norm_g, mla_w_uq, mla_kv_norm_g, mla_w_ukv, ret_gn_g, od_w_in, od_w_out, conv_w, conv_b, lru_wa, lru_ba, lru_wx, lru_bx, lru_lambda, xa_wq, xa_wk, xa_wv, xa_wo, peer_wq, peer_sub_keys, peer_u, peer_v, final_norm_g):
    batch, seq, d = x.shape
    t = batch * seq
    h = x.reshape(t, d)
    mem2 = mem.reshape(-1, d)
    tail = (xa_norm_g, mem_norm_g, xa_wq, xa_wk, xa_wv, xa_wo, ffn_norm_g, peer_wq, peer_sub_keys)
    u16 = peer_u.astype(BF16)
    v16 = peer_v.astype(BF16)

    w_in_p, wqa, wqb = _even_weights(ev_w_in[0], mla_w_uq[0])
    z = norm_matmul([h], mix_norm_g[0], w_in_p)
    cos_m, sin_m = _rope_tables(positions, MLA_ROPE, LANES)
    qf, kvf, kr = mla_proj(z, cos_m, sin_m, mla_q_norm_g[0].reshape(1, -1),
                           mla_kv_norm_g[0].reshape(1, -1), wqa, wqb, mla_w_ukv[0].astype(BF16))
    attn = mla_attention(qf, kvf, kr, batch, seq)
    cos_r, sin_r = _rope_tables(positions, RET_DK, RET_DK)
    sign = jnp.concatenate([-jnp.ones((RET_DK // 2,), F32), jnp.ones((RET_DK // 2,), F32)])
    log_g = jnp.log(1.0 - 2.0 ** (-5.0 - jnp.arange(RET_HEADS, dtype=F32)))
    logg = jnp.broadcast_to(log_g[:, None, None], (RET_HEADS, 1, LANES))
    ret = retention(z, cos_r, sin_r * sign, logg, ret_gn_g[0].reshape(RET_HEADS, 1, RET_DV), batch, seq)
    w_out = ev_w_out[0].astype(BF16)
    n_a = MLA_HEADS * MLA_V
    h = matmul_res([attn, ret], [w_out[:n_a], w_out[n_a:]], [h])
    h, hn, w = _memory_and_route(h, mem2, 0, seq, *tail)
    tile = min(DENSE_TOKENS, t)
    ffn = peer_dense(hn, u16, w, v16, 0, tile)

    z = norm_matmul([h, ffn], mix_norm_g[1], od_w_in[0].astype(BF16), tn=512)
    row = lambda a: a.reshape(1, -1)
    cw = jnp.concatenate([conv_w[0], jnp.zeros((8 - CONV_WIDTH, conv_w.shape[-1]), F32)], axis=0)
    y = lru_branch(z, cw, row(conv_b[0]), lru_wa[0].astype(BF16), row(lru_ba[0]),
                   lru_wx[0].astype(BF16), row(lru_bx[0]), row(lru_lambda[0]), batch, seq)
    h = matmul_res([y], [od_w_out[0].astype(BF16)], [h, ffn])
    h, hn, w = _memory_and_route(h, mem2, 1, seq, *tail)
    out = add_rmsnorm(h, peer_dense(hn, u16, w, v16, 1, tile), final_norm_g)
    return out.reshape(batch, seq, d)
```

```python
import functools

import jax
import jax.numpy as jnp
import numpy as np
from jax import lax
from jax.experimental import pallas as pl
from jax.experimental.pallas import tpu as pltpu

F32 = jnp.float32
BF16 = jnp.bfloat16
I32 = jnp.int32

EPS = 1e-6
ROPE_THETA = 10000.0

MLA_HEADS = 8
MLA_NOPE = 128
MLA_ROPE = 64
MLA_V = 128
MLA_Q_RANK = 512
MLA_KV_RANK = 256
MLA_QK = 256

RET_HEADS = 8
RET_DK = 128
RET_DV = 128
RET_BLOCK = 256

LRU_BLOCKS = 16
LRU_BLOCK = 128
CONV_WIDTH = 4
LRU_C = 8.0

XA_HEADS = 4
XA_HEAD_DIM = 128

PEER_HEADS = 8
PEER_NKEYS = 128
PEER_HALF = 128
PEER_TOPK = 16
PEER_SEL = PEER_HEADS * PEER_TOPK

LANES = 128
VMEM_LIMIT = 48 * 1024 * 1024

NEG_INF = float("-inf")


def _cparams(sem, vmem=VMEM_LIMIT):
    return pltpu.CompilerParams(dimension_semantics=sem, vmem_limit_bytes=vmem)


def _rms(x, g):
    ms = jnp.mean(x * x, axis=-1, keepdims=True)
    return x * lax.rsqrt(ms + EPS) * g


def _norm_matmul_kernel(*refs, n_x):
    x_refs = refs[:n_x]
    g_ref, w_ref, o_ref, xn_ref = refs[n_x:]

    @pl.when(pl.program_id(1) == 0)
    def _():
        rows_per_chunk = min(NORM_ROWS, xn_ref.shape[0])
        for c in range(xn_ref.shape[0] // rows_per_chunk):
            rows = slice(c * rows_per_chunk, (c + 1) * rows_per_chunk)
            x = x_refs[0][rows, :]
            for extra in x_refs[1:]:
                x = x + extra[rows, :]
            xn_ref[rows, :] = _rms(x, g_ref[...]).astype(BF16)

    o_ref[...] = jnp.dot(xn_ref[...], w_ref[...], preferred_element_type=F32).astype(o_ref.dtype)


NORM_ROWS = 128
NORM_MATMUL_VMEM_LIMIT = 60 * 1024 * 1024


def norm_matmul(xs, g, w, tm=1024, tn=1024):
    t, d = xs[0].shape
    n = w.shape[1]
    tm = min(tm, t)
    tn = min(tn, n)
    return pl.pallas_call(
        functools.partial(_norm_matmul_kernel, n_x=len(xs)),
        grid=(t // tm, n // tn),
        in_specs=[pl.BlockSpec((tm, d), lambda i, j: (i, 0)) for _ in xs] + [
            pl.BlockSpec((1, d), lambda i, j: (0, 0)),
            pl.BlockSpec((d, tn), lambda i, j: (0, j)),
        ],
        out_specs=pl.BlockSpec((tm, tn), lambda i, j: (i, j)),
        out_shape=jax.ShapeDtypeStruct((t, n), BF16),
        scratch_shapes=[pltpu.VMEM((tm, d), BF16)],
        compiler_params=_cparams(("parallel", "arbitrary"), NORM_MATMUL_VMEM_LIMIT),
        name="norm_matmul",
    )(*xs, g.reshape(1, d), w)


def _matmul_res_kernel(*refs, n_in):
    xs = refs[:n_in]
    ws = refs[n_in:2 * n_in]
    res_refs = refs[2 * n_in:-1]
    o_ref = refs[-1]
    acc = res_refs[0][...]
    for res_ref in res_refs[1:]:
        acc = acc + res_ref[...]
    for x_ref, w_ref in zip(xs, ws):
        acc = acc + jnp.dot(x_ref[...], w_ref[...], preferred_element_type=F32)
    o_ref[...] = acc


def matmul_res(xs, ws, ress, tm=512, tn=2048):
    t, n = ress[0].shape
    tm = min(tm, t)
    tn = min(tn, n)
    n_in = len(xs)
    in_specs = [pl.BlockSpec((tm, x.shape[1]), lambda i, j: (i, 0)) for x in xs]
    in_specs += [pl.BlockSpec((w.shape[0], tn), lambda i, j: (0, j)) for w in ws]
    in_specs += [pl.BlockSpec((tm, tn), lambda i, j: (i, j)) for _ in ress]
    return pl.pallas_call(
        functools.partial(_matmul_res_kernel, n_in=n_in),
        grid=(t // tm, n // tn),
        in_specs=in_specs,
        out_specs=pl.BlockSpec((tm, tn), lambda i, j: (i, j)),
        out_shape=jax.ShapeDtypeStruct((t, n), F32),
        compiler_params=_cparams(("parallel", "parallel")),
        name="matmul_res",
    )(*xs, *ws, *ress)


def _mla_proj_kernel(z_ref, cos_ref, sin_ref, qg_ref, kvg_ref, wqa_ref, wqb_ref, wkv_ref,
                     q_ref, kv_ref, kr_ref):
    scale = (MLA_NOPE + MLA_ROPE) ** -0.5
    z = z_ref[...]
    cq = _rms(z[:, :MLA_Q_RANK].astype(F32), qg_ref[...]).astype(BF16)
    ckv = _rms(z[:, MLA_Q_RANK:MLA_Q_RANK + MLA_KV_RANK].astype(F32), kvg_ref[...]).astype(BF16)
    cos = cos_ref[...]
    sin = sin_ref[...]
    qa = jnp.dot(cq, wqa_ref[...], preferred_element_type=F32)
    qb = jnp.dot(cq, wqb_ref[...], preferred_element_type=F32)
    for h in range(MLA_HEADS):
        base = h * MLA_QK
        q_ref[:, base:base + LANES] = (qa[:, base:base + LANES] * scale).astype(BF16)
        rot = qa[:, base + LANES:base + 2 * LANES] * cos + qb[:, h * LANES:(h + 1) * LANES] * sin
        q_ref[:, base + LANES:base + 2 * LANES] = (rot * scale).astype(BF16)
    kv_ref[...] = jnp.dot(ckv, wkv_ref[...], preferred_element_type=F32).astype(BF16)
    off = MLA_Q_RANK + MLA_KV_RANK
    kr = z[:, off:off + LANES].astype(F32) * cos + z[:, off + LANES:off + 2 * LANES].astype(F32) * sin
    kr_ref[...] = kr.astype(BF16)


def mla_proj(z, cos_t, sin_t, qg, kvg, wqa, wqb, wkv, tm=512):
    t = z.shape[0]
    tm = min(tm, t)
    hq = MLA_HEADS * MLA_QK
    full = lambda a: pl.BlockSpec(a.shape, lambda i: (0, 0))
    return pl.pallas_call(
        _mla_proj_kernel,
        grid=(t // tm,),
        in_specs=[
            pl.BlockSpec((tm, 1024), lambda i: (i, 0)),
            pl.BlockSpec((tm, LANES), lambda i: (i, 0)),
            pl.BlockSpec((tm, LANES), lambda i: (i, 0)),
            full(qg), full(kvg), full(wqa), full(wqb), full(wkv),
        ],
        out_specs=[
            pl.BlockSpec((tm, hq), lambda i: (i, 0)),
            pl.BlockSpec((tm, hq), lambda i: (i, 0)),
            pl.BlockSpec((tm, LANES), lambda i: (i, 0)),
        ],
        out_shape=[
            jax.ShapeDtypeStruct((t, hq), BF16),
            jax.ShapeDtypeStruct((t, hq), BF16),
            jax.ShapeDtypeStruct((t, LANES), BF16),
        ],
        compiler_params=_cparams(("parallel",)),
        name="mla_proj",
    )(z, cos_t, sin_t, qg, kvg, wqa, wqb, wkv)


def _mla_attn_kernel(q_ref, kv_ref, kr_ref, o_ref, kcat_ref, *, seq, tq):
    kcat_ref[:, :LANES] = kv_ref[:, :LANES]
    kcat_ref[:, LANES:] = kr_ref[...]
    for qi in range(seq // tq):
        kend = (qi + 1) * tq
        q = q_ref[qi * tq:(qi + 1) * tq, :]
        s = lax.dot_general(q, kcat_ref[:kend, :], (((1,), (1,)), ((), ())),
                            preferred_element_type=F32)
        row = lax.broadcasted_iota(I32, (tq, kend), 0) + qi * tq
        col = lax.broadcasted_iota(I32, (tq, kend), 1)
        s = jnp.where(col <= row, s, NEG_INF)
        m = jnp.max(s, axis=-1, keepdims=True)
        p = jnp.exp(s - m)
        l = jnp.sum(p, axis=-1, keepdims=True)
        o = jnp.dot(p.astype(BF16), kv_ref[:kend, LANES:], preferred_element_type=F32)
        o_ref[qi * tq:(qi + 1) * tq, :] = (o / l).astype(BF16)


def mla_attention(qf, kvf, kr, batch, seq, tq=256):
    tq = min(tq, seq)
    t = batch * seq
    return pl.pallas_call(
        functools.partial(_mla_attn_kernel, seq=seq, tq=tq),
        grid=(batch, MLA_HEADS),
        in_specs=[
            pl.BlockSpec((seq, MLA_QK), lambda b, h: (b, h)),
            pl.BlockSpec((seq, MLA_QK), lambda b, h: (b, h)),
            pl.BlockSpec((seq, LANES), lambda b, h: (b, 0)),
        ],
        out_specs=pl.BlockSpec((seq, MLA_V), lambda b, h: (b, h)),
        out_shape=jax.ShapeDtypeStruct((t, MLA_HEADS * MLA_V), BF16),
        scratch_shapes=[pltpu.VMEM((seq, MLA_QK), BF16)],
        compiler_params=_cparams(("parallel", "parallel")),
        name="mla_attention",
    )(qf, kvf, kr)


def _retention_kernel(q_ref, k_ref, v_ref, g_ref, cos_ref, sin_ref, logg_ref, gn_ref, o_ref,
                      *, seq, blk):
    lg = logg_ref[...]
    cos = cos_ref[...]
    sin = sin_ref[...]
    half = RET_DK // 2

    def rope(x):
        return x * cos + pltpu.roll(x, half, 1) * sin

    q_all = rope(q_ref[...].astype(F32))
    k_all = rope(k_ref[...].astype(F32)) * (RET_DK ** -0.5)

    ri = lax.broadcasted_iota(I32, (blk, blk), 0)
    ci = lax.broadcasted_iota(I32, (blk, blk), 1)
    diff = (ri - ci).astype(F32)
    decay = jnp.where(diff >= 0, jnp.exp(lg[:, :1] * jnp.maximum(diff, 0.0)), 0.0)
    pos = lax.broadcasted_iota(I32, (blk, RET_DK), 0).astype(F32)
    xi = jnp.exp(lg * (pos + 1.0))
    zeta = jnp.exp(lg * (blk - 1.0 - pos))
    blk_decay = jnp.exp(lg * float(blk))

    state = jnp.zeros((RET_DK, RET_DV), F32)
    for n in range(seq // blk):
        sl = slice(n * blk, (n + 1) * blk)
        qc = q_all[sl]
        kc = k_all[sl]
        vc = v_ref[sl, :]
        att = lax.dot_general(qc.astype(BF16), kc.astype(BF16), (((1,), (1,)), ((), ())),
                              preferred_element_type=F32) * decay
        inner = jnp.dot(att.astype(BF16), vc, preferred_element_type=F32)
        cross = jnp.dot((qc * xi).astype(BF16), state.astype(BF16), preferred_element_type=F32)
        ret = inner + cross
        kz_t = (kc * zeta).T.astype(BF16)
        dstate = jnp.dot(kz_t, vc, preferred_element_type=F32)
        state = state * blk_decay + dstate
        mu = jnp.mean(ret, axis=-1, keepdims=True)
        cen = ret - mu
        var = jnp.mean(cen * cen, axis=-1, keepdims=True)
        normed = cen * lax.rsqrt(var + EPS) * gn_ref[...]
        gate = g_ref[sl, :].astype(F32)
        o_ref[sl, :] = (gate * jax.nn.sigmoid(gate) * normed).astype(BF16)


def retention(z, cos_t, sin_t, logg, gn_g, batch, seq):
    blk = min(RET_BLOCK, seq)
    t = batch * seq
    first = 1024 // LANES
    col = lambda part: (lambda b, h: (b, first + part * RET_HEADS + h))
    return pl.pallas_call(
        functools.partial(_retention_kernel, seq=seq, blk=blk),
        grid=(batch, RET_HEADS),
        in_specs=[
            pl.BlockSpec((seq, LANES), col(0)),
            pl.BlockSpec((seq, LANES), col(1)),
            pl.BlockSpec((seq, LANES), col(2)),
            pl.BlockSpec((seq, LANES), col(3)),
            pl.BlockSpec((seq, LANES), lambda b, h: (b, 0)),
            pl.BlockSpec((seq, LANES), lambda b, h: (b, 0)),
            pl.BlockSpec((None, 1, LANES), lambda b, h: (h, 0, 0)),
            pl.BlockSpec((None, 1, LANES), lambda b, h: (h, 0, 0)),
        ],
        out_specs=pl.BlockSpec((seq, RET_DV), lambda b, h: (b, h)),
        out_shape=jax.ShapeDtypeStruct((t, RET_HEADS * RET_DV), BF16),
        compiler_params=_cparams(("parallel", "parallel")),
        name="retention",
    )(z, z, z, z, cos_t, sin_t, logg, gn_g)


def _xattn_kernel(h_ref, g_ref, wq_ref, kv_ref, wo_ref, o_ref):
    h = h_ref[...]
    hn = _rms(h, g_ref[...]).astype(BF16)
    q = jnp.dot(hn, wq_ref[...], preferred_element_type=F32) * (XA_HEAD_DIM ** -0.5)
    q = q.astype(BF16)
    width = XA_HEADS * XA_HEAD_DIM
    outs = []
    for hd in range(XA_HEADS):
        sl = slice(hd * XA_HEAD_DIM, (hd + 1) * XA_HEAD_DIM)
        k = kv_ref[:, sl]
        v = kv_ref[:, width + hd * XA_HEAD_DIM:width + (hd + 1) * XA_HEAD_DIM]
        s = lax.dot_general(q[:, sl], k, (((1,), (1,)), ((), ())), preferred_element_type=F32)
        m = jnp.max(s, axis=-1, keepdims=True)
        p = jnp.exp(s - m)
        l = jnp.sum(p, axis=-1, keepdims=True)
        o = jnp.dot(p.astype(BF16), v, preferred_element_type=F32) / l
        outs.append(o.astype(BF16))
    o_all = jnp.concatenate(outs, axis=-1)
    o_ref[...] = h + jnp.dot(o_all, wo_ref[...], preferred_element_type=F32)


def cross_attention(h, g, wq, memkv, wo, seq, tm=512):
    t, d = h.shape
    tm = min(tm, seq)
    per_b = seq // tm
    m = memkv.shape[0] // (t // seq)
    full = lambda a: pl.BlockSpec(a.shape, lambda i: (0, 0))
    return pl.pallas_call(
        _xattn_kernel,
        grid=(t // tm,),
        in_specs=[
            pl.BlockSpec((tm, d), lambda i: (i, 0)),
            pl.BlockSpec((1, d), lambda i: (0, 0)),
            full(wq),
            pl.BlockSpec((m, memkv.shape[1]), lambda i: (i // per_b, 0)),
            full(wo),
        ],
        out_specs=pl.BlockSpec((tm, d), lambda i: (i, 0)),
        out_shape=jax.ShapeDtypeStruct((t, d), F32),
        compiler_params=_cparams(("parallel",)),
        name="cross_attention",
    )(h, g.reshape(1, d), wq, memkv, wo)


SC_PITCH = PEER_NKEYS + 8
SUBLANES = 8
TOPK_TOKENS = SUBLANES * LANES


def _peer_scores_kernel(h_ref, g_ref, wqt_ref, keys_ref, hn_ref, sc_ref, *, tm):
    hn = _rms(h_ref[...], g_ref[...]).astype(BF16)
    hn_ref[...] = hn
    q_t = lax.dot_general(wqt_ref[...], hn, (((1,), (1,)), ((), ())),
                          preferred_element_type=F32).astype(BF16)
    pad = jnp.zeros((SC_PITCH - PEER_NKEYS, LANES), F32)
    for grp in range(2 * PEER_HEADS):
        s = jnp.dot(keys_ref[grp], q_t[grp * PEER_HALF:(grp + 1) * PEER_HALF, :],
                    preferred_element_type=F32)
        for c in range(tm // LANES):
            sc_ref[grp, c * SC_PITCH:c * SC_PITCH + PEER_NKEYS, :] = s[:, c * LANES:(c + 1) * LANES]
            sc_ref[grp, c * SC_PITCH + PEER_NKEYS:(c + 1) * SC_PITCH, :] = pad


def peer_scores(h, g, wq_t, keys, tm=512):
    t, d = h.shape
    tm = min(tm, t)
    rows = (tm // LANES) * SC_PITCH
    return pl.pallas_call(
        functools.partial(_peer_scores_kernel, tm=tm),
        grid=(t // tm,),
        in_specs=[
            pl.BlockSpec((tm, d), lambda i: (i, 0)),
            pl.BlockSpec((1, d), lambda i: (0, 0)),
            pl.BlockSpec(wq_t.shape, lambda i: (0, 0)),
            pl.BlockSpec(keys.shape, lambda i: (0, 0, 0)),
        ],
        out_specs=[
            pl.BlockSpec((tm, d), lambda i: (i, 0)),
            pl.BlockSpec((2 * PEER_HEADS, rows, LANES), lambda i: (0, i, 0)),
        ],
        out_shape=[
            jax.ShapeDtypeStruct((t, d), BF16),
            jax.ShapeDtypeStruct((2 * PEER_HEADS, (t // LANES) * SC_PITCH, LANES), F32),
        ],
        compiler_params=_cparams(("parallel",)),
        name="peer_scores",
    )(h, g.reshape(1, d), wq_t, keys)


def _tree(op, xs):
    xs = list(xs)
    while len(xs) > 1:
        xs = [op(xs[i], xs[i + 1]) if i + 1 < len(xs) else xs[i] for i in range(0, len(xs), 2)]
    return xs[0]


def _sorting_network(n):
    def merge(lo, hi, r):
        step = 2 * r
        if step < hi - lo:
            yield from merge(lo, hi, step)
            yield from merge(lo + r, hi, step)
            yield from ((i, i + r) for i in range(lo + r, hi - r, step))
        else:
            yield (lo, lo + r)

    def sort(lo, hi):
        if hi > lo:
            mid = lo + (hi - lo) // 2
            yield from sort(lo, mid)
            yield from sort(mid + 1, hi)
            yield from merge(lo, hi, 1)

    return list(sort(0, n - 1))


_SORT16 = _sorting_network(PEER_TOPK)


def _sort_block(vals, ids):
    vals, ids = list(vals), list(ids)
    for i, j in _SORT16:
        va, vb, ia, ib = vals[i], vals[j], ids[i], ids[j]
        swap = (vb > va) | ((vb == va) & (ib < ia))
        vals[i], vals[j] = jnp.maximum(va, vb), jnp.minimum(va, vb)
        ids[i], ids[j] = jnp.where(swap, ib, ia), jnp.where(swap, ia, ib)
    return vals, ids


def _merge_top16(lists):
    def better(a, b):
        return jnp.maximum(a[0], b[0]), jnp.where(a[0] >= b[0], a[1], b[1])

    lists = [list(entries) for entries in lists]
    vals, idxs = [], []
    for r in range(PEER_TOPK):
        m, pos = _tree(better, [entries[0] for entries in lists if entries])
        vals.append(m)
        idxs.append(pos)
        keep = PEER_TOPK - 1 - r
        owner = jnp.floor(pos * (1.0 / PEER_TOPK))
        for n, entries in enumerate(lists):
            if not entries:
                continue
            won = owner == float(n)
            nxt = entries[1:] + [(NEG_INF, float(PEER_TOPK * n))]
            lists[n] = [(jnp.where(won, b[0], a[0]), jnp.where(won, b[1], a[1]))
                        for a, b in list(zip(entries, nxt))[:keep]]
    return vals, idxs


def _top16(rows):
    blocks = []
    for n in range(len(rows) // PEER_TOPK):
        base = n * PEER_TOPK
        ids = [jnp.full(rows[0].shape, float(base + k), F32) for k in range(PEER_TOPK)]
        vals, ids = _sort_block(rows[base:base + PEER_TOPK], ids)
        blocks.append(list(zip(vals, ids)))
    return _merge_top16(blocks)


def _pick(table, sel):
    out = table[0]
    for i in range(1, PEER_TOPK):
        out = jnp.where(sel == float(i), table[i], out)
    return out


_PAIRS = [(i, j) for i in range(PEER_TOPK) for j in range(PEER_TOPK) if (i + 1) * (j + 1) <= PEER_TOPK]


def _route_head(s1, s2):
    v1, i1 = _top16(s1)
    v2, i2 = _top16(s2)
    sums = [[(v1[i] + v2[j], float(i * PEER_TOPK + j)) for j in range(PEER_TOPK) if (i, j) in _PAIRS]
            for i in range(PEER_TOPK)]
    best, flat = _merge_top16(sums)
    probs = [jnp.exp(b - best[0]) for b in best]
    inv = 1.0 / _tree(jnp.add, probs)
    experts = []
    for r in range(PEER_TOPK):
        hi = jnp.floor(flat[r] * (1.0 / PEER_TOPK))
        experts.append(_pick(i1, hi) * PEER_NKEYS + _pick(i2, flat[r] - hi * PEER_TOPK))
    return experts, [p * inv for p in probs]


def _peer_topk_kernel(sc_ref, e_ref, gate_ref, esel_ref, gsel_ref):
    head = pl.program_id(1)

    def load(grp):
        return [sc_ref[grp, pl.ds(k, SUBLANES, stride=SC_PITCH), :] for k in range(PEER_NKEYS)]

    experts, gates = _route_head(load(0), load(1))
    for r in range(PEER_TOPK):
        row = pl.multiple_of((head * PEER_TOPK + r) * SUBLANES, SUBLANES)
        esel_ref[pl.ds(row, SUBLANES), :] = experts[r]
        gsel_ref[pl.ds(row, SUBLANES), :] = gates[r]

    @pl.when(head == PEER_HEADS - 1)
    def _():
        for s in range(SUBLANES):
            e_sel = esel_ref[pl.ds(s, PEER_SEL, stride=SUBLANES), :]
            g_sel = gsel_ref[pl.ds(s, PEER_SEL, stride=SUBLANES), :]
            e_ref[s * LANES:(s + 1) * LANES, :] = e_sel.T.astype(I32)
            gate_ref[s * LANES:(s + 1) * LANES, :] = g_sel.T


def peer_topk(scores):
    t = scores.shape[1] // SC_PITCH * LANES
    rows = SUBLANES * SC_PITCH
    return pl.pallas_call(
        _peer_topk_kernel,
        grid=(t // TOPK_TOKENS, PEER_HEADS),
        in_specs=[pl.BlockSpec((2, rows, LANES), lambda i, hd: (hd, i, 0))],
        out_specs=[
            pl.BlockSpec((TOPK_TOKENS, PEER_SEL), lambda i, hd: (i, 0)),
            pl.BlockSpec((TOPK_TOKENS, PEER_SEL), lambda i, hd: (i, 0)),
        ],
        out_shape=[
            jax.ShapeDtypeStruct((t, PEER_SEL), I32),
            jax.ShapeDtypeStruct((t, PEER_SEL), F32),
        ],
        scratch_shapes=[
            pltpu.VMEM((PEER_SEL * SUBLANES, LANES), F32),
            pltpu.VMEM((PEER_SEL * SUBLANES, LANES), F32),
        ],
        compiler_params=_cparams(("parallel", "arbitrary")),
        name="peer_topk",
    )(scores)


W_UNROLL = 32
W_SLABS = PEER_NKEYS // SUBLANES
U32 = jnp.uint32
BF16_HI = np.uint32(0xFFFF0000)
HALF_WORD = np.uint32(16)


def _token_gates(e_row, g_row, sub):
    a_hot = jnp.where(sub == lax.shift_right_logical(e_row, 7), 1.0, 0.0).astype(BF16)
    b_hot = jnp.where(sub == (e_row & (PEER_NKEYS - 1)), g_row, 0.0).astype(BF16)
    return lax.dot_general(a_hot, b_hot, (((1,), (1,)), ((), ())), preferred_element_type=F32)


def _bf16_bits(x):
    return lax.bitcast_convert_type(x.astype(BF16).astype(F32), U32)


def _peer_gates_kernel(e_hi_ref, g_hi_ref, e_lo_ref, g_lo_ref, w_ref, *, tw):
    sub = lax.broadcasted_iota(I32, (PEER_NKEYS, PEER_SEL), 0)

    def group(gi, carry):
        p0 = pl.multiple_of(gi * W_UNROLL, W_UNROLL)
        for i in range(W_UNROLL):
            rows = pl.ds(p0 + i, 1)
            w_hi = _token_gates(e_hi_ref[rows, :], g_hi_ref[rows, :], sub)
            w_lo = _token_gates(e_lo_ref[rows, :], g_lo_ref[rows, :], sub)
            packed = _bf16_bits(w_hi) | lax.shift_right_logical(_bf16_bits(w_lo), HALF_WORD)
            row = pl.multiple_of((p0 + i) * SUBLANES, SUBLANES)
            for s in range(W_SLABS):
                w_ref[s, pl.ds(row, SUBLANES), :] = packed[s * SUBLANES:(s + 1) * SUBLANES, :]
        return carry

    lax.fori_loop(0, tw // W_UNROLL, group, 0)


def peer_gates(e, g, tile, tw=128):
    t = e.shape[0]
    half = tile // 2
    tw = min(tw, half)
    per_half = half // tw
    hi = lambda i, c: (i * 2 * per_half + c, 0)
    lo = lambda i, c: (i * 2 * per_half + per_half + c, 0)
    return pl.pallas_call(
        functools.partial(_peer_gates_kernel, tw=tw),
        grid=(t // tile, per_half),
        in_specs=[
            pl.BlockSpec((tw, PEER_SEL), hi),
            pl.BlockSpec((tw, PEER_SEL), hi),
            pl.BlockSpec((tw, PEER_SEL), lo),
            pl.BlockSpec((tw, PEER_SEL), lo),
        ],
        out_specs=pl.BlockSpec((W_SLABS, tw * SUBLANES, PEER_NKEYS), lambda i, c: (0, i * per_half + c, 0)),
        out_shape=jax.ShapeDtypeStruct((W_SLABS, (t // 2) * SUBLANES, PEER_NKEYS), U32),
        compiler_params=_cparams(("parallel", "parallel")),
        name="peer_gates",
    )(e, g, e, g)


DENSE_CHUNK = 512
DENSE_VMEM_LIMIT = 60 * 1024 * 1024


def _peer_dense_kernel(hn_ref, u_ref, w_ref, v_ref, o_ref):
    j = pl.program_id(1)

    @pl.when(j == 0)
    def _():
        o_ref[...] = jnp.zeros_like(o_ref)

    hn = hn_ref[...]
    tm = hn.shape[0]
    te = u_ref.shape[0]
    k1_per_chunk = DENSE_CHUNK // PEER_NKEYS
    steps_per_slab = (SUBLANES * PEER_NKEYS) // te
    k1_base = (j % steps_per_slab) * (te // PEER_NKEYS)
    parts = []
    for c in range(te // DENSE_CHUNK):
        rows = slice(c * DENSE_CHUNK, (c + 1) * DENSE_CHUNK)
        pre = lax.dot_general(hn, u_ref[rows, :], (((1,), (1,)), ((), ())),
                              preferred_element_type=F32)
        act = 0.5 * pre * (1.0 + lax.erf(pre * (2.0 ** -0.5)))
        words = [w_ref[pl.ds(k1_base + c * k1_per_chunk + k, tm // 2, stride=SUBLANES), :]
                 for k in range(k1_per_chunk)]
        first = jnp.concatenate([lax.bitcast_convert_type(x & BF16_HI, F32) for x in words], axis=1)
        second = jnp.concatenate(
            [lax.bitcast_convert_type(lax.shift_left(x, HALF_WORD), F32) for x in words], axis=1)
        gate = jnp.concatenate([first, second], axis=0)
        parts.append((act * gate).astype(BF16))
    weighted = parts[0] if len(parts) == 1 else jnp.concatenate(parts, axis=1)
    o_ref[...] += jnp.dot(weighted, v_ref[...], preferred_element_type=F32)


DENSE_TOKENS = 1024


def peer_dense(hn, u, w, v, layer, tm, te=1024):
    t, d = hn.shape
    n_exp = u.shape[1]
    slab = SUBLANES * PEER_NKEYS
    return pl.pallas_call(
        _peer_dense_kernel,
        grid=(t // tm, n_exp // te),
        in_specs=[
            pl.BlockSpec((tm, d), lambda i, j: (i, 0)),
            pl.BlockSpec((None, te, d), lambda i, j: (layer, j, 0)),
            pl.BlockSpec((None, (tm // 2) * SUBLANES, PEER_NKEYS), lambda i, j: (j * te // slab, i, 0)),
            pl.BlockSpec((None, te, d), lambda i, j: (layer, j, 0)),
        ],
        out_specs=pl.BlockSpec((tm, d), lambda i, j: (i, 0)),
        out_shape=jax.ShapeDtypeStruct((t, d), F32),
        compiler_params=_cparams(("parallel", "arbitrary"), DENSE_VMEM_LIMIT),
        name="peer_dense",
    )(hn, u, w, v)


def _add_rmsnorm_kernel(a_ref, b_ref, g_ref, o_ref):
    o_ref[...] = _rms(a_ref[...] + b_ref[...], g_ref[...])


def add_rmsnorm(a, b, g, tm=512):
    t, d = a.shape
    tm = min(tm, t)
    return pl.pallas_call(
        _add_rmsnorm_kernel,
        grid=(t // tm,),
        in_specs=[
            pl.BlockSpec((tm, d), lambda i: (i, 0)),
            pl.BlockSpec((tm, d), lambda i: (i, 0)),
            pl.BlockSpec((1, d), lambda i: (0, 0)),
        ],
        out_specs=pl.BlockSpec((tm, d), lambda i: (i, 0)),
        out_shape=jax.ShapeDtypeStruct((t, d), F32),
        compiler_params=_cparams(("parallel",)),
        name="add_rmsnorm",
    )(a, b, g.reshape(1, d))


LRU_ROWS = 8


def _lru_kernel(gate_ref, rec_ref, cw_ref, cb_ref, wa_ref, ba_ref, wx_ref, bx_ref, lam_ref, o_ref,
                tail_ref, carry_ref, a_ref, u_ref, hs_ref, *, ts):
    @pl.when(pl.program_id(1) == 0)
    def _():
        tail_ref[...] = jnp.zeros_like(tail_ref)
        carry_ref[...] = jnp.zeros_like(carry_ref)

    x = rec_ref[...].astype(F32)
    xp = jnp.concatenate([tail_ref[...], x], axis=0)
    tail_ref[...] = x[ts - LRU_ROWS:, :]
    xc = cb_ref[...] + cw_ref[CONV_WIDTH - 1:CONV_WIDTH, :] * x
    for back in range(1, CONV_WIDTH):
        shifted = pltpu.roll(xp, back, 0)[LRU_ROWS:, :]
        xc = xc + cw_ref[CONV_WIDTH - 1 - back:CONV_WIDTH - back, :] * shifted

    soft = jnp.log(1.0 + jnp.exp(-lam_ref[...]))
    for n in range(LRU_BLOCKS):
        sl = slice(n * LRU_BLOCK, (n + 1) * LRU_BLOCK)
        xb = xc[:, sl]
        xb16 = xb.astype(BF16)
        r = jax.nn.sigmoid(jnp.dot(xb16, wa_ref[n], preferred_element_type=F32) + ba_ref[:, sl])
        i = jax.nn.sigmoid(jnp.dot(xb16, wx_ref[n], preferred_element_type=F32) + bx_ref[:, sl])
        log_a = -LRU_C * r * soft[:, sl]
        a = jnp.exp(log_a)
        a_ref[:, sl] = a
        u_ref[:, sl] = jnp.sqrt(1.0 - a * a) * (i * xb)

    groups = ts // LRU_ROWS
    a = a_ref[...].reshape(groups, LRU_ROWS, -1)
    u = u_ref[...].reshape(groups, LRU_ROWS, -1)
    sub = lax.broadcasted_iota(I32, a.shape, 1)
    for shift in (1, 2, 4):
        ok = sub >= shift
        u = u + a * jnp.where(ok, pltpu.roll(u, shift, 1), 0.0)
        a = a * jnp.where(ok, pltpu.roll(a, shift, 1), 1.0)
    a_ref[...] = a.reshape(ts, -1)
    u_ref[...] = u.reshape(ts, -1)

    def step(c, h):
        r0 = pl.multiple_of(c * LRU_ROWS, LRU_ROWS)
        h8 = u_ref[pl.ds(r0, LRU_ROWS), :] + a_ref[pl.ds(r0, LRU_ROWS), :] * h
        hs_ref[pl.ds(r0, LRU_ROWS), :] = h8
        return h8[LRU_ROWS - 1:, :]

    carry_ref[...] = lax.fori_loop(0, ts // LRU_ROWS, step, carry_ref[...])
    o_ref[...] = (jax.nn.gelu(gate_ref[...].astype(F32)) * hs_ref[...]).astype(BF16)


def lru_branch(z, conv_w, conv_b, wa, ba, wx, bx, lam, batch, seq, ts=256):
    t = z.shape[0]
    w = z.shape[1] // 2
    ts = min(ts, seq)
    per_b = seq // ts
    row = lambda a: pl.BlockSpec(a.shape, lambda b, s: (0, 0))
    cube = lambda a: pl.BlockSpec(a.shape, lambda b, s: (0, 0, 0))
    return pl.pallas_call(
        functools.partial(_lru_kernel, ts=ts),
        grid=(batch, per_b),
        in_specs=[
            pl.BlockSpec((ts, w), lambda b, s: (b * per_b + s, 0)),
            pl.BlockSpec((ts, w), lambda b, s: (b * per_b + s, 1)),
            row(conv_w), row(conv_b), cube(wa), row(ba), cube(wx), row(bx), row(lam),
        ],
        out_specs=pl.BlockSpec((ts, w), lambda b, s: (b * per_b + s, 0)),
        out_shape=jax.ShapeDtypeStruct((t, w), BF16),
        scratch_shapes=[
            pltpu.VMEM((LRU_ROWS, w), F32),
            pltpu.VMEM((1, w), F32),
            pltpu.VMEM((ts, w), F32),
            pltpu.VMEM((ts, w), F32),
            pltpu.VMEM((ts, w), F32),
        ],
        compiler_params=_cparams(("parallel", "arbitrary")),
        name="lru_branch",
    )(z, z, conv_w, conv_b, wa, ba, wx, bx, lam)


def _rope_tables(positions, dim, width):
    inv = ROPE_THETA ** (-jnp.arange(0, dim, 2, dtype=F32) / dim)
    ang = positions.astype(F32).reshape(-1, 1) * inv
    pad = jnp.zeros((ang.shape[0], width - dim), F32)
    cos = jnp.concatenate([jnp.cos(ang), jnp.cos(ang), pad], axis=-1)
    sin = jnp.concatenate([jnp.sin(ang), jnp.sin(ang), pad], axis=-1)
    return cos, sin


def _rotate_half_cols(w):
    half = w.shape[-1] // 2
    return jnp.concatenate([-w[..., half:], w[..., :half]], axis=-1)


def _even_weights(w_in, w_uq):
    d = w_in.shape[0]
    o = [0, 512, 768, 832]
    z64 = jnp.zeros((d, 64), F32)
    k_rope = w_in[:, o[2]:o[3]]
    w_in_p = jnp.concatenate(
        [w_in[:, :o[2]], k_rope, z64, _rotate_half_cols(k_rope), z64, w_in[:, o[3]:]], axis=-1)
    wq3 = w_uq.reshape(MLA_Q_RANK, MLA_HEADS, MLA_NOPE + MLA_ROPE)
    nope, rope = wq3[..., :MLA_NOPE], wq3[..., MLA_NOPE:]
    zq = jnp.zeros((MLA_Q_RANK, MLA_HEADS, 64), F32)
    wqa = jnp.concatenate([nope, rope, zq], axis=-1).reshape(MLA_Q_RANK, MLA_HEADS * MLA_QK)
    wqb = jnp.concatenate([_rotate_half_cols(rope), zq], axis=-1).reshape(MLA_Q_RANK, MLA_HEADS * LANES)
    return w_in_p.astype(BF16), wqa.astype(BF16), wqb.astype(BF16)


def _memory_and_route(h, mem2, layer, seq, xa_norm_g, mem_norm_g, xa_wq, xa_wk, xa_wv, xa_wo,
                      ffn_norm_g, peer_wq, peer_sub_keys):
    w_kv = jnp.concatenate([xa_wk[layer], xa_wv[layer]], axis=-1).astype(BF16)
    memkv = norm_matmul([mem2], mem_norm_g[layer], w_kv)
    h = cross_attention(h, xa_norm_g[layer], xa_wq[layer].astype(BF16), memkv,
                        xa_wo[layer].astype(BF16), seq)
    keys = peer_sub_keys[layer].reshape(2 * PEER_HEADS, PEER_NKEYS, PEER_HALF).astype(BF16)
    hn, scores = peer_scores(h, ffn_norm_g[layer], peer_wq[layer].T.astype(BF16), keys)
    e_sel, g_sel = peer_topk(scores)
    return h, hn, peer_gates(e_sel, g_sel, min(DENSE_TOKENS, h.shape[0]))


def kernel(x, mem, positions, mix_norm_g, xa_norm_g, ffn_norm_g, mem_norm_g, ev_w_in, ev_w_out, mla_q_norm_g, mla_w_uq, mla_kv_norm_g, mla_w_ukv, ret_gn_g, od_w_in, od_w_out, conv_w, conv_b, lru_wa, lru_ba, lru_wx, lru_bx, lru_lambda, xa_wq, xa_wk, xa_wv, xa_wo, peer_wq, peer_sub_keys, peer_u, peer_v, final_norm_g):
    batch, seq, d = x.shape
    t = batch * seq
    h = x.reshape(t, d)
    mem2 = mem.reshape(-1, d)
    tail = (xa_norm_g, mem_norm_g, xa_wq, xa_wk, xa_wv, xa_wo, ffn_norm_g, peer_wq, peer_sub_keys)
    u16 = peer_u.astype(BF16)
    v16 = peer_v.astype(BF16)

    w_in_p, wqa, wqb = _even_weights(ev_w_in[0], mla_w_uq[0])
    z = norm_matmul([h], mix_norm_g[0], w_in_p)
    cos_m, sin_m = _rope_tables(positions, MLA_ROPE, LANES)
    qf, kvf, kr = mla_proj(z, cos_m, sin_m, mla_q_norm_g[0].reshape(1, -1),
                           mla_kv_norm_g[0].reshape(1, -1), wqa, wqb, mla_w_ukv[0].astype(BF16))
    attn = mla_attention(qf, kvf, kr, batch, seq)
    cos_r, sin_r = _rope_tables(positions, RET_DK, RET_DK)
    sign = jnp.concatenate([-jnp.ones((RET_DK // 2,), F32), jnp.ones((RET_DK // 2,), F32)])
    log_g = jnp.log(1.0 - 2.0 ** (-5.0 - jnp.arange(RET_HEADS, dtype=F32)))
    logg = jnp.broadcast_to(log_g[:, None, None], (RET_HEADS, 1, LANES))
    ret = retention(z, cos_r, sin_r * sign, logg, ret_gn_g[0].reshape(RET_HEADS, 1, RET_DV), batch, seq)
    w_out = ev_w_out[0].astype(BF16)
    n_a = MLA_HEADS * MLA_V
    h = matmul_res([attn, ret], [w_out[:n_a], w_out[n_a:]], [h])
    h, hn, w = _memory_and_route(h, mem2, 0, seq, *tail)
    tile = min(DENSE_TOKENS, t)
    ffn = peer_dense(hn, u16, w, v16, 0, tile)

    z = norm_matmul([h, ffn], mix_norm_g[1], od_w_in[0].astype(BF16))
    row = lambda a: a.reshape(1, -1)
    cw = jnp.concatenate([conv_w[0], jnp.zeros((8 - CONV_WIDTH, conv_w.shape[-1]), F32)], axis=0)
    y = lru_branch(z, cw, row(conv_b[0]), lru_wa[0].astype(BF16), row(lru_ba[0]),
                   lru_wx[0].astype(BF16), row(lru_bx[0]), row(lru_lambda[0]), batch, seq)
    h = matmul_res([y], [od_w_out[0].astype(BF16)], [h, ffn])
    h, hn, w = _memory_and_route(h, mem2, 1, seq, *tail)
    out = add_rmsnorm(h, peer_dense(hn, u16, w, v16, 1, tile), final_norm_g)
    return out.reshape(batch, seq, d)
```

```python
import functools

import jax
import jax.numpy as jnp
import numpy as np
from jax import lax
from jax.experimental import pallas as pl
from jax.experimental.pallas import tpu as pltpu

F32 = jnp.float32
BF16 = jnp.bfloat16
I32 = jnp.int32

EPS = 1e-6
ROPE_THETA = 10000.0

MLA_HEADS = 8
MLA_NOPE = 128
MLA_ROPE = 64
MLA_V = 128
MLA_Q_RANK = 512
MLA_KV_RANK = 256
MLA_QK = 256

RET_HEADS = 8
RET_DK = 128
RET_DV = 128
RET_BLOCK = 256

LRU_BLOCKS = 16
LRU_BLOCK = 128
CONV_WIDTH = 4
LRU_C = 8.0

XA_HEADS = 4
XA_HEAD_DIM = 128

PEER_HEADS = 8
PEER_NKEYS = 128
PEER_HALF = 128
PEER_TOPK = 16
PEER_SEL = PEER_HEADS * PEER_TOPK

LANES = 128
VMEM_LIMIT = 48 * 1024 * 1024

NEG_INF = float("-inf")


def _cparams(sem, vmem=VMEM_LIMIT):
    return pltpu.CompilerParams(dimension_semantics=sem, vmem_limit_bytes=vmem)


def _rms(x, g):
    ms = jnp.mean(x * x, axis=-1, keepdims=True)
    return x * lax.rsqrt(ms + EPS) * g


def _norm_matmul_kernel(*refs, n_x):
    x_refs = refs[:n_x]
    g_ref, w_ref, o_ref, xn_ref = refs[n_x:]

    @pl.when(pl.program_id(1) == 0)
    def _():
        rows_per_chunk = min(NORM_ROWS, xn_ref.shape[0])
        for c in range(xn_ref.shape[0] // rows_per_chunk):
            rows = slice(c * rows_per_chunk, (c + 1) * rows_per_chunk)
            x = x_refs[0][rows, :]
            for extra in x_refs[1:]:
                x = x + extra[rows, :]
            xn_ref[rows, :] = _rms(x, g_ref[...]).astype(BF16)

    o_ref[...] = jnp.dot(xn_ref[...], w_ref[...], preferred_element_type=F32).astype(o_ref.dtype)


NORM_ROWS = 128
NORM_MATMUL_VMEM_LIMIT = 60 * 1024 * 1024


def norm_matmul(xs, g, w, tm=1024, tn=1024):
    t, d = xs[0].shape
    n = w.shape[1]
    tm = min(tm, t)
    tn = min(tn, n)
    return pl.pallas_call(
        functools.partial(_norm_matmul_kernel, n_x=len(xs)),
        grid=(t // tm, n // tn),
        in_specs=[pl.BlockSpec((tm, d), lambda i, j: (i, 0)) for _ in xs] + [
            pl.BlockSpec((1, d), lambda i, j: (0, 0)),
            pl.BlockSpec((d, tn), lambda i, j: (0, j)),
        ],
        out_specs=pl.BlockSpec((tm, tn), lambda i, j: (i, j)),
        out_shape=jax.ShapeDtypeStruct((t, n), BF16),
        scratch_shapes=[pltpu.VMEM((tm, d), BF16)],
        compiler_params=_cparams(("parallel", "arbitrary"), NORM_MATMUL_VMEM_LIMIT),
        name="norm_matmul",
    )(*xs, g.reshape(1, d), w)


def _matmul_res_kernel(*refs, n_in):
    xs = refs[:n_in]
    ws = refs[n_in:2 * n_in]
    res_refs = refs[2 * n_in:-1]
    o_ref = refs[-1]
    acc = res_refs[0][...]
    for res_ref in res_refs[1:]:
        acc = acc + res_ref[...]
    for x_ref, w_ref in zip(xs, ws):
        acc = acc + jnp.dot(x_ref[...], w_ref[...], preferred_element_type=F32)
    o_ref[...] = acc


def matmul_res(xs, ws, ress, tm=512, tn=2048):
    t, n = ress[0].shape
    tm = min(tm, t)
    tn = min(tn, n)
    n_in = len(xs)
    in_specs = [pl.BlockSpec((tm, x.shape[1]), lambda i, j: (i, 0)) for x in xs]
    in_specs += [pl.BlockSpec((w.shape[0], tn), lambda i, j: (0, j)) for w in ws]
    in_specs += [pl.BlockSpec((tm, tn), lambda i, j: (i, j)) for _ in ress]
    return pl.pallas_call(
        functools.partial(_matmul_res_kernel, n_in=n_in),
        grid=(t // tm, n // tn),
        in_specs=in_specs,
        out_specs=pl.BlockSpec((tm, tn), lambda i, j: (i, j)),
        out_shape=jax.ShapeDtypeStruct((t, n), F32),
        compiler_params=_cparams(("parallel", "parallel")),
        name="matmul_res",
    )(*xs, *ws, *ress)


def _mla_proj_kernel(z_ref, cos_ref, sin_ref, qg_ref, kvg_ref, wqa_ref, wqb_ref, wkv_ref,
                     q_ref, kv_ref, kr_ref):
    scale = (MLA_NOPE + MLA_ROPE) ** -0.5
    z = z_ref[...]
    cq = _rms(z[:, :MLA_Q_RANK].astype(F32), qg_ref[...]).astype(BF16)
    ckv = _rms(z[:, MLA_Q_RANK:MLA_Q_RANK + MLA_KV_RANK].astype(F32), kvg_ref[...]).astype(BF16)
    cos = cos_ref[...]
    sin = sin_ref[...]
    qa = jnp.dot(cq, wqa_ref[...], preferred_element_type=F32)
    qb = jnp.dot(cq, wqb_ref[...], preferred_element_type=F32)
    for h in range(MLA_HEADS):
        base = h * MLA_QK
        q_ref[:, base:base + LANES] = (qa[:, base:base + LANES] * scale).astype(BF16)
        rot = qa[:, base + LANES:base + 2 * LANES] * cos + qb[:, h * LANES:(h + 1) * LANES] * sin
        q_ref[:, base + LANES:base + 2 * LANES] = (rot * scale).astype(BF16)
    kv_ref[...] = jnp.dot(ckv, wkv_ref[...], preferred_element_type=F32).astype(BF16)
    off = MLA_Q_RANK + MLA_KV_RANK
    kr = z[:, off:off + LANES].astype(F32) * cos + z[:, off + LANES:off + 2 * LANES].astype(F32) * sin
    kr_ref[...] = kr.astype(BF16)


def mla_proj(z, cos_t, sin_t, qg, kvg, wqa, wqb, wkv, tm=512):
    t = z.shape[0]
    tm = min(tm, t)
    hq = MLA_HEADS * MLA_QK
    full = lambda a: pl.BlockSpec(a.shape, lambda i: (0, 0))
    return pl.pallas_call(
        _mla_proj_kernel,
        grid=(t // tm,),
        in_specs=[
            pl.BlockSpec((tm, 1024), lambda i: (i, 0)),
            pl.BlockSpec((tm, LANES), lambda i: (i, 0)),
            pl.BlockSpec((tm, LANES), lambda i: (i, 0)),
            full(qg), full(kvg), full(wqa), full(wqb), full(wkv),
        ],
        out_specs=[
            pl.BlockSpec((tm, hq), lambda i: (i, 0)),
            pl.BlockSpec((tm, hq), lambda i: (i, 0)),
            pl.BlockSpec((tm, LANES), lambda i: (i, 0)),
        ],
        out_shape=[
            jax.ShapeDtypeStruct((t, hq), BF16),
            jax.ShapeDtypeStruct((t, hq), BF16),
            jax.ShapeDtypeStruct((t, LANES), BF16),
        ],
        compiler_params=_cparams(("parallel",)),
        name="mla_proj",
    )(z, cos_t, sin_t, qg, kvg, wqa, wqb, wkv)


def _mla_attn_kernel(q_ref, kv_ref, kr_ref, o_ref, kcat_ref, vpair_ref, *, seq, tq):
    for hd in range(2):
        kcat_ref[hd, :, :LANES] = kv_ref[:, hd * MLA_QK:hd * MLA_QK + LANES]
        kcat_ref[hd, :, LANES:] = kr_ref[...]
        vpair_ref[:, hd * MLA_V:(hd + 1) * MLA_V] = kv_ref[:, hd * MLA_QK + LANES:(hd + 1) * MLA_QK]
    for qi in range(seq // tq):
        kend = (qi + 1) * tq
        rows = slice(qi * tq, (qi + 1) * tq)
        row = lax.broadcasted_iota(I32, (tq, kend), 0) + qi * tq
        col = lax.broadcasted_iota(I32, (tq, kend), 1)
        probs, norms = [], []
        for hd in range(2):
            s = lax.dot_general(q_ref[rows, hd * MLA_QK:(hd + 1) * MLA_QK], kcat_ref[hd, :kend, :],
                                (((1,), (1,)), ((), ())), preferred_element_type=F32)
            s = jnp.where(col <= row, s, NEG_INF)
            p = jnp.exp(s - jnp.max(s, axis=-1, keepdims=True))
            norms.append(jnp.sum(p, axis=-1, keepdims=True))
            probs.append(p.astype(BF16))
        o = jnp.dot(jnp.concatenate(probs, axis=0), vpair_ref[:kend, :], preferred_element_type=F32)
        o_ref[rows, :MLA_V] = (o[:tq, :MLA_V] / norms[0]).astype(BF16)
        o_ref[rows, MLA_V:] = (o[tq:, MLA_V:] / norms[1]).astype(BF16)


def mla_attention(qf, kvf, kr, batch, seq, tq=256):
    tq = min(tq, seq)
    t = batch * seq
    return pl.pallas_call(
        functools.partial(_mla_attn_kernel, seq=seq, tq=tq),
        grid=(batch, MLA_HEADS // 2),
        in_specs=[
            pl.BlockSpec((seq, 2 * MLA_QK), lambda b, h: (b, h)),
            pl.BlockSpec((seq, 2 * MLA_QK), lambda b, h: (b, h)),
            pl.BlockSpec((seq, LANES), lambda b, h: (b, 0)),
        ],
        out_specs=pl.BlockSpec((seq, 2 * MLA_V), lambda b, h: (b, h)),
        out_shape=jax.ShapeDtypeStruct((t, MLA_HEADS * MLA_V), BF16),
        scratch_shapes=[pltpu.VMEM((2, seq, MLA_QK), BF16), pltpu.VMEM((seq, 2 * MLA_V), BF16)],
        compiler_params=_cparams(("parallel", "parallel")),
        name="mla_attention",
    )(qf, kvf, kr)


def _retention_kernel(q_ref, k_ref, v_ref, g_ref, cos_ref, sin_ref, logg_ref, gn_ref, o_ref,
                      *, seq, blk):
    lg = logg_ref[...]
    cos = cos_ref[...]
    sin = sin_ref[...]
    half = RET_DK // 2

    def rope(x):
        return x * cos + pltpu.roll(x, half, 1) * sin

    q_all = rope(q_ref[...].astype(F32))
    k_all = rope(k_ref[...].astype(F32)) * (RET_DK ** -0.5)

    ri = lax.broadcasted_iota(I32, (blk, blk), 0)
    ci = lax.broadcasted_iota(I32, (blk, blk), 1)
    diff = (ri - ci).astype(F32)
    decay = jnp.where(diff >= 0, jnp.exp(lg[:, :1] * jnp.maximum(diff, 0.0)), 0.0)
    pos = lax.broadcasted_iota(I32, (blk, RET_DK), 0).astype(F32)
    xi = jnp.exp(lg * (pos + 1.0))
    zeta = jnp.exp(lg * (blk - 1.0 - pos))
    blk_decay = jnp.exp(lg * float(blk))

    state = jnp.zeros((RET_DK, RET_DV), F32)
    for n in range(seq // blk):
        sl = slice(n * blk, (n + 1) * blk)
        qc = q_all[sl]
        kc = k_all[sl]
        vc = v_ref[sl, :]
        att = lax.dot_general(qc.astype(BF16), kc.astype(BF16), (((1,), (1,)), ((), ())),
                              preferred_element_type=F32) * decay
        inner = jnp.dot(att.astype(BF16), vc, preferred_element_type=F32)
        cross = jnp.dot((qc * xi).astype(BF16), state.astype(BF16), preferred_element_type=F32)
        ret = inner + cross
        kz_t = (kc * zeta).T.astype(BF16)
        dstate = jnp.dot(kz_t, vc, preferred_element_type=F32)
        state = state * blk_decay + dstate
        mu = jnp.mean(ret, axis=-1, keepdims=True)
        cen = ret - mu
        var = jnp.mean(cen * cen, axis=-1, keepdims=True)
        normed = cen * lax.rsqrt(var + EPS) * gn_ref[...]
        gate = g_ref[sl, :].astype(F32)
        o_ref[sl, :] = (gate * jax.nn.sigmoid(gate) * normed).astype(BF16)


def retention(z, cos_t, sin_t, logg, gn_g, batch, seq):
    blk = min(RET_BLOCK, seq)
    t = batch * seq
    first = 1024 // LANES
    col = lambda part: (lambda b, h: (b, first + part * RET_HEADS + h))
    return pl.pallas_call(
        functools.partial(_retention_kernel, seq=seq, blk=blk),
        grid=(batch, RET_HEADS),
        in_specs=[
            pl.BlockSpec((seq, LANES), col(0)),
            pl.BlockSpec((seq, LANES), col(1)),
            pl.BlockSpec((seq, LANES), col(2)),
            pl.BlockSpec((seq, LANES), col(3)),
            pl.BlockSpec((seq, LANES), lambda b, h: (b, 0)),
            pl.BlockSpec((seq, LANES), lambda b, h: (b, 0)),
            pl.BlockSpec((None, 1, LANES), lambda b, h: (h, 0, 0)),
            pl.BlockSpec((None, 1, LANES), lambda b, h: (h, 0, 0)),
        ],
        out_specs=pl.BlockSpec((seq, RET_DV), lambda b, h: (b, h)),
        out_shape=jax.ShapeDtypeStruct((t, RET_HEADS * RET_DV), BF16),
        compiler_params=_cparams(("parallel", "parallel")),
        name="retention",
    )(z, z, z, z, cos_t, sin_t, logg, gn_g)


def _xattn_kernel(h_ref, g_ref, wq_ref, kv_ref, wo_ref, o_ref):
    h = h_ref[...]
    hn = _rms(h, g_ref[...]).astype(BF16)
    q = jnp.dot(hn, wq_ref[...], preferred_element_type=F32) * (XA_HEAD_DIM ** -0.5)
    q = q.astype(BF16)
    width = XA_HEADS * XA_HEAD_DIM
    outs = []
    for hd in range(XA_HEADS):
        sl = slice(hd * XA_HEAD_DIM, (hd + 1) * XA_HEAD_DIM)
        k = kv_ref[:, sl]
        v = kv_ref[:, width + hd * XA_HEAD_DIM:width + (hd + 1) * XA_HEAD_DIM]
        s = lax.dot_general(q[:, sl], k, (((1,), (1,)), ((), ())), preferred_element_type=F32)
        m = jnp.max(s, axis=-1, keepdims=True)
        p = jnp.exp(s - m)
        l = jnp.sum(p, axis=-1, keepdims=True)
        o = jnp.dot(p.astype(BF16), v, preferred_element_type=F32) / l
        outs.append(o.astype(BF16))
    o_all = jnp.concatenate(outs, axis=-1)
    o_ref[...] = h + jnp.dot(o_all, wo_ref[...], preferred_element_type=F32)


def cross_attention(h, g, wq, memkv, wo, seq, tm=512):
    t, d = h.shape
    tm = min(tm, seq)
    per_b = seq // tm
    m = memkv.shape[0] // (t // seq)
    full = lambda a: pl.BlockSpec(a.shape, lambda i: (0, 0))
    return pl.pallas_call(
        _xattn_kernel,
        grid=(t // tm,),
        in_specs=[
            pl.BlockSpec((tm, d), lambda i: (i, 0)),
            pl.BlockSpec((1, d), lambda i: (0, 0)),
            full(wq),
            pl.BlockSpec((m, memkv.shape[1]), lambda i: (i // per_b, 0)),
            full(wo),
        ],
        out_specs=pl.BlockSpec((tm, d), lambda i: (i, 0)),
        out_shape=jax.ShapeDtypeStruct((t, d), F32),
        compiler_params=_cparams(("parallel",)),
        name="cross_attention",
    )(h, g.reshape(1, d), wq, memkv, wo)


SC_PITCH = PEER_NKEYS + 8
SUBLANES = 8
TOPK_TOKENS = SUBLANES * LANES


def _peer_scores_kernel(h_ref, g_ref, wqt_ref, keys_ref, hn_ref, sc_ref, *, tm):
    hn = _rms(h_ref[...], g_ref[...]).astype(BF16)
    hn_ref[...] = hn
    q_t = lax.dot_general(wqt_ref[...], hn, (((1,), (1,)), ((), ())),
                          preferred_element_type=F32).astype(BF16)
    pad = jnp.zeros((SC_PITCH - PEER_NKEYS, LANES), F32)
    for grp in range(2 * PEER_HEADS):
        s = jnp.dot(keys_ref[grp], q_t[grp * PEER_HALF:(grp + 1) * PEER_HALF, :],
                    preferred_element_type=F32)
        for c in range(tm // LANES):
            sc_ref[grp, c * SC_PITCH:c * SC_PITCH + PEER_NKEYS, :] = s[:, c * LANES:(c + 1) * LANES]
            sc_ref[grp, c * SC_PITCH + PEER_NKEYS:(c + 1) * SC_PITCH, :] = pad


def peer_scores(h, g, wq_t, keys, tm=512):
    t, d = h.shape
    tm = min(tm, t)
    rows = (tm // LANES) * SC_PITCH
    return pl.pallas_call(
        functools.partial(_peer_scores_kernel, tm=tm),
        grid=(t // tm,),
        in_specs=[
            pl.BlockSpec((tm, d), lambda i: (i, 0)),
            pl.BlockSpec((1, d), lambda i: (0, 0)),
            pl.BlockSpec(wq_t.shape, lambda i: (0, 0)),
            pl.BlockSpec(keys.shape, lambda i: (0, 0, 0)),
        ],
        out_specs=[
            pl.BlockSpec((tm, d), lambda i: (i, 0)),
            pl.BlockSpec((2 * PEER_HEADS, rows, LANES), lambda i: (0, i, 0)),
        ],
        out_shape=[
            jax.ShapeDtypeStruct((t, d), BF16),
            jax.ShapeDtypeStruct((2 * PEER_HEADS, (t // LANES) * SC_PITCH, LANES), F32),
        ],
        compiler_params=_cparams(("parallel",)),
        name="peer_scores",
    )(h, g.reshape(1, d), wq_t, keys)


def _tree(op, xs):
    xs = list(xs)
    while len(xs) > 1:
        xs = [op(xs[i], xs[i + 1]) if i + 1 < len(xs) else xs[i] for i in range(0, len(xs), 2)]
    return xs[0]


def _sorting_network(n):
    def merge(lo, hi, r):
        step = 2 * r
        if step < hi - lo:
            yield from merge(lo, hi, step)
            yield from merge(lo + r, hi, step)
            yield from ((i, i + r) for i in range(lo + r, hi - r, step))
        else:
            yield (lo, lo + r)

    def sort(lo, hi):
        if hi > lo:
            mid = lo + (hi - lo) // 2
            yield from sort(lo, mid)
            yield from sort(mid + 1, hi)
            yield from merge(lo, hi, 1)

    return list(sort(0, n - 1))


_SORT16 = _sorting_network(PEER_TOPK)


def _sort_block(vals, ids):
    vals, ids = list(vals), list(ids)
    for i, j in _SORT16:
        va, vb, ia, ib = vals[i], vals[j], ids[i], ids[j]
        swap = (vb > va) | ((vb == va) & (ib < ia))
        vals[i], vals[j] = jnp.maximum(va, vb), jnp.minimum(va, vb)
        ids[i], ids[j] = jnp.where(swap, ib, ia), jnp.where(swap, ia, ib)
    return vals, ids


def _merge_top16(lists):
    def better(a, b):
        return jnp.maximum(a[0], b[0]), jnp.where(a[0] >= b[0], a[1], b[1])

    lists = [list(entries) for entries in lists]
    vals, idxs = [], []
    for r in range(PEER_TOPK):
        m, pos = _tree(better, [entries[0] for entries in lists if entries])
        vals.append(m)
        idxs.append(pos)
        keep = PEER_TOPK - 1 - r
        owner = jnp.floor(pos * (1.0 / PEER_TOPK))
        for n, entries in enumerate(lists):
            if not entries:
                continue
            won = owner == float(n)
            nxt = entries[1:] + [(NEG_INF, float(PEER_TOPK * n))]
            lists[n] = [(jnp.where(won, b[0], a[0]), jnp.where(won, b[1], a[1]))
                        for a, b in list(zip(entries, nxt))[:keep]]
    return vals, idxs


def _top16(rows):
    blocks = []
    for n in range(len(rows) // PEER_TOPK):
        base = n * PEER_TOPK
        ids = [jnp.full(rows[0].shape, float(base + k), F32) for k in range(PEER_TOPK)]
        vals, ids = _sort_block(rows[base:base + PEER_TOPK], ids)
        blocks.append(list(zip(vals, ids)))
    return _merge_top16(blocks)


def _pick(table, sel):
    out = table[0]
    for i in range(1, PEER_TOPK):
        out = jnp.where(sel == float(i), table[i], out)
    return out


_PAIRS = [(i, j) for i in range(PEER_TOPK) for j in range(PEER_TOPK) if (i + 1) * (j + 1) <= PEER_TOPK]


def _route_head(s1, s2):
    v1, i1 = _top16(s1)
    v2, i2 = _top16(s2)
    sums = [[(v1[i] + v2[j], float(i * PEER_TOPK + j)) for j in range(PEER_TOPK) if (i, j) in _PAIRS]
            for i in range(PEER_TOPK)]
    best, flat = _merge_top16(sums)
    probs = [jnp.exp(b - best[0]) for b in best]
    inv = 1.0 / _tree(jnp.add, probs)
    experts = []
    for r in range(PEER_TOPK):
        hi = jnp.floor(flat[r] * (1.0 / PEER_TOPK))
        experts.append(_pick(i1, hi) * PEER_NKEYS + _pick(i2, flat[r] - hi * PEER_TOPK))
    return experts, [p * inv for p in probs]


def _peer_topk_kernel(sc_ref, e_ref, gate_ref, esel_ref, gsel_ref):
    head = pl.program_id(1)

    def load(grp):
        return [sc_ref[grp, pl.ds(k, SUBLANES, stride=SC_PITCH), :] for k in range(PEER_NKEYS)]

    experts, gates = _route_head(load(0), load(1))
    for r in range(PEER_TOPK):
        row = pl.multiple_of((head * PEER_TOPK + r) * SUBLANES, SUBLANES)
        esel_ref[pl.ds(row, SUBLANES), :] = experts[r]
        gsel_ref[pl.ds(row, SUBLANES), :] = gates[r]

    @pl.when(head == PEER_HEADS - 1)
    def _():
        for s in range(SUBLANES):
            e_sel = esel_ref[pl.ds(s, PEER_SEL, stride=SUBLANES), :]
            g_sel = gsel_ref[pl.ds(s, PEER_SEL, stride=SUBLANES), :]
            e_ref[s * LANES:(s + 1) * LANES, :] = e_sel.T.astype(I32)
            gate_ref[s * LANES:(s + 1) * LANES, :] = g_sel.T


def peer_topk(scores):
    t = scores.shape[1] // SC_PITCH * LANES
    rows = SUBLANES * SC_PITCH
    return pl.pallas_call(
        _peer_topk_kernel,
        grid=(t // TOPK_TOKENS, PEER_HEADS),
        in_specs=[pl.BlockSpec((2, rows, LANES), lambda i, hd: (hd, i, 0))],
        out_specs=[
            pl.BlockSpec((TOPK_TOKENS, PEER_SEL), lambda i, hd: (i, 0)),
            pl.BlockSpec((TOPK_TOKENS, PEER_SEL), lambda i, hd: (i, 0)),
        ],
        out_shape=[
            jax.ShapeDtypeStruct((t, PEER_SEL), I32),
            jax.ShapeDtypeStruct((t, PEER_SEL), F32),
        ],
        scratch_shapes=[
            pltpu.VMEM((PEER_SEL * SUBLANES, LANES), F32),
            pltpu.VMEM((PEER_SEL * SUBLANES, LANES), F32),
        ],
        compiler_params=_cparams(("parallel", "arbitrary")),
        name="peer_topk",
    )(scores)


W_UNROLL = 32
W_SLABS = PEER_NKEYS // SUBLANES
U32 = jnp.uint32
BF16_HI = np.uint32(0xFFFF0000)
HALF_WORD = np.uint32(16)


def _token_gates(e_row, g_row, sub):
    a_hot = jnp.where(sub == lax.shift_right_logical(e_row, 7), 1.0, 0.0).astype(BF16)
    b_hot = jnp.where(sub == (e_row & (PEER_NKEYS - 1)), g_row, 0.0).astype(BF16)
    return lax.dot_general(a_hot, b_hot, (((1,), (1,)), ((), ())), preferred_element_type=F32)


def _bf16_bits(x):
    return lax.bitcast_convert_type(x.astype(BF16).astype(F32), U32)


def _peer_gates_kernel(e_hi_ref, g_hi_ref, e_lo_ref, g_lo_ref, w_ref, *, tw):
    sub = lax.broadcasted_iota(I32, (PEER_NKEYS, PEER_SEL), 0)

    def group(gi, carry):
        p0 = pl.multiple_of(gi * W_UNROLL, W_UNROLL)
        for i in range(W_UNROLL):
            rows = pl.ds(p0 + i, 1)
            w_hi = _token_gates(e_hi_ref[rows, :], g_hi_ref[rows, :], sub)
            w_lo = _token_gates(e_lo_ref[rows, :], g_lo_ref[rows, :], sub)
            packed = _bf16_bits(w_hi) | lax.shift_right_logical(_bf16_bits(w_lo), HALF_WORD)
            row = pl.multiple_of((p0 + i) * SUBLANES, SUBLANES)
            for s in range(W_SLABS):
                w_ref[s, pl.ds(row, SUBLANES), :] = packed[s * SUBLANES:(s + 1) * SUBLANES, :]
        return carry

    lax.fori_loop(0, tw // W_UNROLL, group, 0)


def peer_gates(e, g, tile, tw=128):
    t = e.shape[0]
    half = tile // 2
    tw = min(tw, half)
    per_half = half // tw
    hi = lambda i, c: (i * 2 * per_half + c, 0)
    lo = lambda i, c: (i * 2 * per_half + per_half + c, 0)
    return pl.pallas_call(
        functools.partial(_peer_gates_kernel, tw=tw),
        grid=(t // tile, per_half),
        in_specs=[
            pl.BlockSpec((tw, PEER_SEL), hi),
            pl.BlockSpec((tw, PEER_SEL), hi),
            pl.BlockSpec((tw, PEER_SEL), lo),
            pl.BlockSpec((tw, PEER_SEL), lo),
        ],
        out_specs=pl.BlockSpec((W_SLABS, tw * SUBLANES, PEER_NKEYS), lambda i, c: (0, i * per_half + c, 0)),
        out_shape=jax.ShapeDtypeStruct((W_SLABS, (t // 2) * SUBLANES, PEER_NKEYS), U32),
        compiler_params=_cparams(("parallel", "parallel")),
        name="peer_gates",
    )(e, g, e, g)


DENSE_CHUNK = 512
DENSE_VMEM_LIMIT = 60 * 1024 * 1024


def _peer_dense_kernel(hn_ref, u_ref, w_ref, v_ref, o_ref):
    j = pl.program_id(1)

    @pl.when(j == 0)
    def _():
        o_ref[...] = jnp.zeros_like(o_ref)

    hn = hn_ref[...]
    tm = hn.shape[0]
    te = u_ref.shape[0]
    k1_per_chunk = DENSE_CHUNK // PEER_NKEYS
    steps_per_slab = (SUBLANES * PEER_NKEYS) // te
    k1_base = (j % steps_per_slab) * (te // PEER_NKEYS)
    parts = []
    for c in range(te // DENSE_CHUNK):
        rows = slice(c * DENSE_CHUNK, (c + 1) * DENSE_CHUNK)
        pre = lax.dot_general(hn, u_ref[rows, :], (((1,), (1,)), ((), ())),
                              preferred_element_type=F32)
        act = 0.5 * pre * (1.0 + lax.erf(pre * (2.0 ** -0.5)))
        words = [w_ref[pl.ds(k1_base + c * k1_per_chunk + k, tm // 2, stride=SUBLANES), :]
                 for k in range(k1_per_chunk)]
        first = jnp.concatenate([lax.bitcast_convert_type(x & BF16_HI, F32) for x in words], axis=1)
        second = jnp.concatenate(
            [lax.bitcast_convert_type(lax.shift_left(x, HALF_WORD), F32) for x in words], axis=1)
        gate = jnp.concatenate([first, second], axis=0)
        parts.append((act * gate).astype(BF16))
    weighted = parts[0] if len(parts) == 1 else jnp.concatenate(parts, axis=1)
    o_ref[...] += jnp.dot(weighted, v_ref[...], preferred_element_type=F32)


DENSE_TOKENS = 1024


def peer_dense(hn, u, w, v, layer, tm, te=1024):
    t, d = hn.shape
    n_exp = u.shape[1]
    slab = SUBLANES * PEER_NKEYS
    return pl.pallas_call(
        _peer_dense_kernel,
        grid=(t // tm, n_exp // te),
        in_specs=[
            pl.BlockSpec((tm, d), lambda i, j: (i, 0)),
            pl.BlockSpec((None, te, d), lambda i, j: (layer, j, 0)),
            pl.BlockSpec((None, (tm // 2) * SUBLANES, PEER_NKEYS), lambda i, j: (j * te // slab, i, 0)),
            pl.BlockSpec((None, te, d), lambda i, j: (layer, j, 0)),
        ],
        out_specs=pl.BlockSpec((tm, d), lambda i, j: (i, 0)),
        out_shape=jax.ShapeDtypeStruct((t, d), F32),
        compiler_params=_cparams(("parallel", "arbitrary"), DENSE_VMEM_LIMIT),
        name="peer_dense",
    )(hn, u, w, v)


def _add_rmsnorm_kernel(a_ref, b_ref, g_ref, o_ref):
    o_ref[...] = _rms(a_ref[...] + b_ref[...], g_ref[...])


def add_rmsnorm(a, b, g, tm=512):
    t, d = a.shape
    tm = min(tm, t)
    return pl.pallas_call(
        _add_rmsnorm_kernel,
        grid=(t // tm,),
        in_specs=[
            pl.BlockSpec((tm, d), lambda i: (i, 0)),
            pl.BlockSpec((tm, d), lambda i: (i, 0)),
            pl.BlockSpec((1, d), lambda i: (0, 0)),
        ],
        out_specs=pl.BlockSpec((tm, d), lambda i: (i, 0)),
        out_shape=jax.ShapeDtypeStruct((t, d), F32),
        compiler_params=_cparams(("parallel",)),
        name="add_rmsnorm",
    )(a, b, g.reshape(1, d))


LRU_ROWS = 8


def _lru_kernel(gate_ref, rec_ref, cw_ref, cb_ref, wa_ref, ba_ref, wx_ref, bx_ref, lam_ref, o_ref,
                tail_ref, carry_ref, a_ref, u_ref, hs_ref, *, ts):
    @pl.when(pl.program_id(1) == 0)
    def _():
        tail_ref[...] = jnp.zeros_like(tail_ref)
        carry_ref[...] = jnp.zeros_like(carry_ref)

    x = rec_ref[...].astype(F32)
    xp = jnp.concatenate([tail_ref[...], x], axis=0)
    tail_ref[...] = x[ts - LRU_ROWS:, :]
    xc = cb_ref[...] + cw_ref[CONV_WIDTH - 1:CONV_WIDTH, :] * x
    for back in range(1, CONV_WIDTH):
        shifted = pltpu.roll(xp, back, 0)[LRU_ROWS:, :]
        xc = xc + cw_ref[CONV_WIDTH - 1 - back:CONV_WIDTH - back, :] * shifted

    soft = jnp.log(1.0 + jnp.exp(-lam_ref[...]))
    for n in range(LRU_BLOCKS):
        sl = slice(n * LRU_BLOCK, (n + 1) * LRU_BLOCK)
        xb = xc[:, sl]
        xb16 = xb.astype(BF16)
        r = jax.nn.sigmoid(jnp.dot(xb16, wa_ref[n], preferred_element_type=F32) + ba_ref[:, sl])
        i = jax.nn.sigmoid(jnp.dot(xb16, wx_ref[n], preferred_element_type=F32) + bx_ref[:, sl])
        log_a = -LRU_C * r * soft[:, sl]
        a = jnp.exp(log_a)
        a_ref[:, sl] = a
        u_ref[:, sl] = jnp.sqrt(1.0 - a * a) * (i * xb)

    groups = ts // LRU_ROWS
    a = a_ref[...].reshape(groups, LRU_ROWS, -1)
    u = u_ref[...].reshape(groups, LRU_ROWS, -1)
    sub = lax.broadcasted_iota(I32, a.shape, 1)
    for shift in (1, 2, 4):
        ok = sub >= shift
        u = u + a * jnp.where(ok, pltpu.roll(u, shift, 1), 0.0)
        a = a * jnp.where(ok, pltpu.roll(a, shift, 1), 1.0)
    a_ref[...] = a.reshape(ts, -1)
    u_ref[...] = u.reshape(ts, -1)

    def step(c, h):
        r0 = pl.multiple_of(c * LRU_ROWS, LRU_ROWS)
        h8 = u_ref[pl.ds(r0, LRU_ROWS), :] + a_ref[pl.ds(r0, LRU_ROWS), :] * h
        hs_ref[pl.ds(r0, LRU_ROWS), :] = h8
        return h8[LRU_ROWS - 1:, :]

    carry_ref[...] = lax.fori_loop(0, ts // LRU_ROWS, step, carry_ref[...])
    o_ref[...] = (jax.nn.gelu(gate_ref[...].astype(F32)) * hs_ref[...]).astype(BF16)


def lru_branch(z, conv_w, conv_b, wa, ba, wx, bx, lam, batch, seq, ts=256):
    t = z.shape[0]
    w = z.shape[1] // 2
    ts = min(ts, seq)
    per_b = seq // ts
    row = lambda a: pl.BlockSpec(a.shape, lambda b, s: (0, 0))
    cube = lambda a: pl.BlockSpec(a.shape, lambda b, s: (0, 0, 0))
    return pl.pallas_call(
        functools.partial(_lru_kernel, ts=ts),
        grid=(batch, per_b),
        in_specs=[
            pl.BlockSpec((ts, w), lambda b, s: (b * per_b + s, 0)),
            pl.BlockSpec((ts, w), lambda b, s: (b * per_b + s, 1)),
            row(conv_w), row(conv_b), cube(wa), row(ba), cube(wx), row(bx), row(lam),
        ],
        out_specs=pl.BlockSpec((ts, w), lambda b, s: (b * per_b + s, 0)),
        out_shape=jax.ShapeDtypeStruct((t, w), BF16),
        scratch_shapes=[
            pltpu.VMEM((LRU_ROWS, w), F32),
            pltpu.VMEM((1, w), F32),
            pltpu.VMEM((ts, w), F32),
            pltpu.VMEM((ts, w), F32),
            pltpu.VMEM((ts, w), F32),
        ],
        compiler_params=_cparams(("parallel", "arbitrary")),
        name="lru_branch",
    )(z, z, conv_w, conv_b, wa, ba, wx, bx, lam)


def _rope_tables(positions, dim, width):
    inv = ROPE_THETA ** (-jnp.arange(0, dim, 2, dtype=F32) / dim)
    ang = positions.astype(F32).reshape(-1, 1) * inv
    pad = jnp.zeros((ang.shape[0], width - dim), F32)
    cos = jnp.concatenate([jnp.cos(ang), jnp.cos(ang), pad], axis=-1)
    sin = jnp.concatenate([jnp.sin(ang), jnp.sin(ang), pad], axis=-1)
    return cos, sin


def _rotate_half_cols(w):
    half = w.shape[-1] // 2
    return jnp.concatenate([-w[..., half:], w[..., :half]], axis=-1)


def _even_weights(w_in, w_uq):
    d = w_in.shape[0]
    o = [0, 512, 768, 832]
    z64 = jnp.zeros((d, 64), F32)
    k_rope = w_in[:, o[2]:o[3]]
    w_in_p = jnp.concatenate(
        [w_in[:, :o[2]], k_rope, z64, _rotate_half_cols(k_rope), z64, w_in[:, o[3]:]], axis=-1)
    wq3 = w_uq.reshape(MLA_Q_RANK, MLA_HEADS, MLA_NOPE + MLA_ROPE)
    nope, rope = wq3[..., :MLA_NOPE], wq3[..., MLA_NOPE:]
    zq = jnp.zeros((MLA_Q_RANK, MLA_HEADS, 64), F32)
    wqa = jnp.concatenate([nope, rope, zq], axis=-1).reshape(MLA_Q_RANK, MLA_HEADS * MLA_QK)
    wqb = jnp.concatenate([_rotate_half_cols(rope), zq], axis=-1).reshape(MLA_Q_RANK, MLA_HEADS * LANES)
    return w_in_p.astype(BF16), wqa.astype(BF16), wqb.astype(BF16)


def _memory_and_route(h, mem2, layer, seq, xa_norm_g, mem_norm_g, xa_wq, xa_wk, xa_wv, xa_wo,
                      ffn_norm_g, peer_wq, peer_sub_keys):
    w_kv = jnp.concatenate([xa_wk[layer], xa_wv[layer]], axis=-1).astype(BF16)
    memkv = norm_matmul([mem2], mem_norm_g[layer], w_kv)
    h = cross_attention(h, xa_norm_g[layer], xa_wq[layer].astype(BF16), memkv,
                        xa_wo[layer].astype(BF16), seq)
    keys = peer_sub_keys[layer].reshape(2 * PEER_HEADS, PEER_NKEYS, PEER_HALF).astype(BF16)
    hn, scores = peer_scores(h, ffn_norm_g[layer], peer_wq[layer].T.astype(BF16), keys)
    e_sel, g_sel = peer_topk(scores)
    return h, hn, peer_gates(e_sel, g_sel, min(DENSE_TOKENS, h.shape[0]))


def kernel(x, mem, positions, mix_norm_g, xa_norm_g, ffn_norm_g, mem_norm_g, ev_w_in, ev_w_out, mla_q_norm_g, mla_w_uq, mla_kv_norm_g, mla_w_ukv, ret_gn_g, od_w_in, od_w_out, conv_w, conv_b, lru_wa, lru_ba, lru_wx, lru_bx, lru_lambda, xa_wq, xa_wk, xa_wv, xa_wo, peer_wq, peer_sub_keys, peer_u, peer_v, final_norm_g):
    batch, seq, d = x.shape
    t = batch * seq
    h = x.reshape(t, d)
    mem2 = mem.reshape(-1, d)
    tail = (xa_norm_g, mem_norm_g, xa_wq, xa_wk, xa_wv, xa_wo, ffn_norm_g, peer_wq, peer_sub_keys)
    u16 = peer_u.astype(BF16)
    v16 = peer_v.astype(BF16)

    w_in_p, wqa, wqb = _even_weights(ev_w_in[0], mla_w_uq[0])
    z = norm_matmul([h], mix_norm_g[0], w_in_p)
    cos_m, sin_m = _rope_tables(positions, MLA_ROPE, LANES)
    qf, kvf, kr = mla_proj(z, cos_m, sin_m, mla_q_norm_g[0].reshape(1, -1),
                           mla_kv_norm_g[0].reshape(1, -1), wqa, wqb, mla_w_ukv[0].astype(BF16))
    attn = mla_attention(qf, kvf, kr, batch, seq)
    cos_r, sin_r = _rope_tables(positions, RET_DK, RET_DK)
    sign = jnp.concatenate([-jnp.ones((RET_DK // 2,), F32), jnp.ones((RET_DK // 2,), F32)])
    log_g = jnp.log(1.0 - 2.0 ** (-5.0 - jnp.arange(RET_HEADS, dtype=F32)))
    logg = jnp.broadcast_to(log_g[:, None, None], (RET_HEADS, 1, LANES))
    ret = retention(z, cos_r, sin_r * sign, logg, ret_gn_g[0].reshape(RET_HEADS, 1, RET_DV), batch, seq)
    w_out = ev_w_out[0].astype(BF16)
    n_a = MLA_HEADS * MLA_V
    h = matmul_res([attn, ret], [w_out[:n_a], w_out[n_a:]], [h])
    h, hn, w = _memory_and_route(h, mem2, 0, seq, *tail)
    tile = min(DENSE_TOKENS, t)
    ffn = peer_dense(hn, u16, w, v16, 0, tile)

    z = norm_matmul([h, ffn], mix_norm_g[1], od_w_in[0].astype(BF16))
    row = lambda a: a.reshape(1, -1)
    cw = jnp.concatenate([conv_w[0], jnp.zeros((8 - CONV_WIDTH, conv_w.shape[-1]), F32)], axis=0)
    y = lru_branch(z, cw, row(conv_b[0]), lru_wa[0].astype(BF16), row(lru_ba[0]),
                   lru_wx[0].astype(BF16), row(lru_bx[0]), row(lru_lambda[0]), batch, seq)
    h = matmul_res([y], [od_w_out[0].astype(BF16)], [h, ffn])
    h, hn, w = _memory_and_route(h, mem2, 1, seq, *tail)
    out = add_rmsnorm(h, peer_dense(hn, u16, w, v16, 1, tile), final_norm_g)
    return out.reshape(batch, seq, d)
```

```python
import functools

import jax
import jax.numpy as jnp
import numpy as np
from jax import lax
from jax.experimental import pallas as pl
from jax.experimental.pallas import tpu as pltpu

F32 = jnp.float32
BF16 = jnp.bfloat16
I32 = jnp.int32

EPS = 1e-6
ROPE_THETA = 10000.0

MLA_HEADS = 8
MLA_NOPE = 128
MLA_ROPE = 64
MLA_V = 128
MLA_Q_RANK = 512
MLA_KV_RANK = 256
MLA_QK = 256
MLA_SECTION = 1024

RET_HEADS = 8
RET_DK = 128
RET_DV = 128
RET_BLOCK = 256

LRU_BLOCKS = 16
LRU_BLOCK = 128
CONV_WIDTH = 4
LRU_C = 8.0

XA_HEADS = 4
XA_HEAD_DIM = 128

PEER_HEADS = 8
PEER_NKEYS = 128
PEER_HALF = 128
PEER_TOPK = 16
PEER_SEL = PEER_HEADS * PEER_TOPK

LANES = 128
VMEM_LIMIT = 48 * 1024 * 1024

NEG_INF = float("-inf")


def _cparams(sem, vmem=VMEM_LIMIT):
    return pltpu.CompilerParams(dimension_semantics=sem, vmem_limit_bytes=vmem)


def _rms(x, g):
    ms = jnp.mean(x * x, axis=-1, keepdims=True)
    return x * lax.rsqrt(ms + EPS) * g


def _norm_matmul_kernel(*refs, n_x):
    x_refs = refs[:n_x]
    g_ref, w_ref, o_ref, xn_ref = refs[n_x:]

    @pl.when(pl.program_id(1) == 0)
    def _():
        rows_per_chunk = min(NORM_ROWS, xn_ref.shape[0])
        for c in range(xn_ref.shape[0] // rows_per_chunk):
            rows = slice(c * rows_per_chunk, (c + 1) * rows_per_chunk)
            x = x_refs[0][rows, :]
            for extra in x_refs[1:]:
                x = x + extra[rows, :]
            xn_ref[rows, :] = _rms(x, g_ref[...]).astype(BF16)

    o_ref[...] = jnp.dot(xn_ref[...], w_ref[...], preferred_element_type=F32).astype(o_ref.dtype)


NORM_ROWS = 128
NORM_MATMUL_VMEM_LIMIT = 60 * 1024 * 1024


def norm_matmul(xs, g, w, tm=1024, tn=1024):
    t, d = xs[0].shape
    n = w.shape[1]
    tm = min(tm, t)
    tn = min(tn, n)
    return pl.pallas_call(
        functools.partial(_norm_matmul_kernel, n_x=len(xs)),
        grid=(t // tm, n // tn),
        in_specs=[pl.BlockSpec((tm, d), lambda i, j: (i, 0)) for _ in xs] + [
            pl.BlockSpec((1, d), lambda i, j: (0, 0)),
            pl.BlockSpec((d, tn), lambda i, j: (0, j)),
        ],
        out_specs=pl.BlockSpec((tm, tn), lambda i, j: (i, j)),
        out_shape=jax.ShapeDtypeStruct((t, n), BF16),
        scratch_shapes=[pltpu.VMEM((tm, d), BF16)],
        compiler_params=_cparams(("parallel", "arbitrary"), NORM_MATMUL_VMEM_LIMIT),
        name="norm_matmul",
    )(*xs, g.reshape(1, d), w)


def _matmul_res_kernel(*refs, n_in):
    xs = refs[:n_in]
    ws = refs[n_in:2 * n_in]
    res_refs = refs[2 * n_in:-1]
    o_ref = refs[-1]
    acc = res_refs[0][...]
    for res_ref in res_refs[1:]:
        acc = acc + res_ref[...]
    for x_ref, w_ref in zip(xs, ws):
        acc = acc + jnp.dot(x_ref[...], w_ref[...], preferred_element_type=F32)
    o_ref[...] = acc


def matmul_res(xs, ws, ress, tm=512, tn=2048):
    t, n = ress[0].shape
    tm = min(tm, t)
    tn = min(tn, n)
    n_in = len(xs)
    in_specs = [pl.BlockSpec((tm, x.shape[1]), lambda i, j: (i, 0)) for x in xs]
    in_specs += [pl.BlockSpec((w.shape[0], tn), lambda i, j: (0, j)) for w in ws]
    in_specs += [pl.BlockSpec((tm, tn), lambda i, j: (i, j)) for _ in ress]
    return pl.pallas_call(
        functools.partial(_matmul_res_kernel, n_in=n_in),
        grid=(t // tm, n // tn),
        in_specs=in_specs,
        out_specs=pl.BlockSpec((tm, tn), lambda i, j: (i, j)),
        out_shape=jax.ShapeDtypeStruct((t, n), F32),
        compiler_params=_cparams(("parallel", "parallel")),
        name="matmul_res",
    )(*xs, *ws, *ress)


def _mla_proj_kernel(z_ref, cos_ref, sin_ref, qg_ref, kvg_ref, wqa_ref, wqb_ref, wkv_ref,
                     q_ref, kv_ref, kr_ref):
    scale = (MLA_NOPE + MLA_ROPE) ** -0.5
    z = z_ref[...]
    cq = _rms(z[:, :MLA_Q_RANK].astype(F32), qg_ref[...]).astype(BF16)
    ckv = _rms(z[:, MLA_Q_RANK:MLA_Q_RANK + MLA_KV_RANK].astype(F32), kvg_ref[...]).astype(BF16)
    cos = cos_ref[...]
    sin = sin_ref[...]
    qa = jnp.dot(cq, wqa_ref[...], preferred_element_type=F32)
    qb = jnp.dot(cq, wqb_ref[...], preferred_element_type=F32)
    for h in range(MLA_HEADS):
        base = h * MLA_QK
        q_ref[:, base:base + LANES] = (qa[:, base:base + LANES] * scale).astype(BF16)
        rot = qa[:, base + LANES:base + 2 * LANES] * cos + qb[:, h * LANES:(h + 1) * LANES] * sin
        q_ref[:, base + LANES:base + 2 * LANES] = (rot * scale).astype(BF16)
    kv_ref[...] = jnp.dot(ckv, wkv_ref[...], preferred_element_type=F32).astype(BF16)
    off = MLA_Q_RANK + MLA_KV_RANK
    kr = z[:, off:off + LANES].astype(F32) * cos + z[:, off + LANES:off + 2 * LANES].astype(F32) * sin
    kr_ref[...] = kr.astype(BF16)


def mla_proj(z, cos_t, sin_t, qg, kvg, wqa, wqb, wkv, tm=512):
    t = z.shape[0]
    tm = min(tm, t)
    hq = MLA_HEADS * MLA_QK
    full = lambda a: pl.BlockSpec(a.shape, lambda i: (0, 0))
    return pl.pallas_call(
        _mla_proj_kernel,
        grid=(t // tm,),
        in_specs=[
            pl.BlockSpec((tm, MLA_SECTION), lambda i: (i, 0)),
            pl.BlockSpec((tm, LANES), lambda i: (i, 0)),
            pl.BlockSpec((tm, LANES), lambda i: (i, 0)),
            full(qg), full(kvg), full(wqa), full(wqb), full(wkv),
        ],
        out_specs=[
            pl.BlockSpec((tm, hq), lambda i: (i, 0)),
            pl.BlockSpec((tm, hq), lambda i: (i, 0)),
            pl.BlockSpec((tm, LANES), lambda i: (i, 0)),
        ],
        out_shape=[
            jax.ShapeDtypeStruct((t, hq), BF16),
            jax.ShapeDtypeStruct((t, hq), BF16),
            jax.ShapeDtypeStruct((t, LANES), BF16),
        ],
        compiler_params=_cparams(("parallel",)),
        name="mla_proj",
    )(z, cos_t, sin_t, qg, kvg, wqa, wqb, wkv)


def _mla_attn_kernel(q_ref, kv_ref, kr_ref, o_ref, kcat_ref, vpair_ref, *, seq, tq):
    for hd in range(2):
        kcat_ref[hd, :, :LANES] = kv_ref[:, hd * MLA_QK:hd * MLA_QK + LANES]
        kcat_ref[hd, :, LANES:] = kr_ref[...]
        vpair_ref[:, hd * MLA_V:(hd + 1) * MLA_V] = kv_ref[:, hd * MLA_QK + LANES:(hd + 1) * MLA_QK]
    for qi in range(seq // tq):
        kend = (qi + 1) * tq
        rows = slice(qi * tq, (qi + 1) * tq)
        row = lax.broadcasted_iota(I32, (tq, kend), 0) + qi * tq
        col = lax.broadcasted_iota(I32, (tq, kend), 1)
        probs, norms = [], []
        for hd in range(2):
            s = lax.dot_general(q_ref[rows, hd * MLA_QK:(hd + 1) * MLA_QK], kcat_ref[hd, :kend, :],
                                (((1,), (1,)), ((), ())), preferred_element_type=F32)
            s = jnp.where(col <= row, s, NEG_INF)
            p = jnp.exp(s - jnp.max(s, axis=-1, keepdims=True))
            norms.append(jnp.sum(p, axis=-1, keepdims=True))
            probs.append(p.astype(BF16))
        o = jnp.dot(jnp.concatenate(probs, axis=0), vpair_ref[:kend, :], preferred_element_type=F32)
        o_ref[rows, :MLA_V] = (o[:tq, :MLA_V] / norms[0]).astype(BF16)
        o_ref[rows, MLA_V:] = (o[tq:, MLA_V:] / norms[1]).astype(BF16)


def mla_attention(qf, kvf, kr, batch, seq, tq=256):
    tq = min(tq, seq)
    t = batch * seq
    return pl.pallas_call(
        functools.partial(_mla_attn_kernel, seq=seq, tq=tq),
        grid=(batch, MLA_HEADS // 2),
        in_specs=[
            pl.BlockSpec((seq, 2 * MLA_QK), lambda b, h: (b, h)),
            pl.BlockSpec((seq, 2 * MLA_QK), lambda b, h: (b, h)),
            pl.BlockSpec((seq, LANES), lambda b, h: (b, 0)),
        ],
        out_specs=pl.BlockSpec((seq, 2 * MLA_V), lambda b, h: (b, h)),
        out_shape=jax.ShapeDtypeStruct((t, MLA_HEADS * MLA_V), BF16),
        scratch_shapes=[pltpu.VMEM((2, seq, MLA_QK), BF16), pltpu.VMEM((seq, 2 * MLA_V), BF16)],
        compiler_params=_cparams(("parallel", "parallel")),
        name="mla_attention",
    )(qf, kvf, kr)


def _retention_kernel(q_ref, k_ref, v_ref, g_ref, cos_ref, sin_ref, logg_ref, gn_ref, o_ref,
                      *, seq, blk):
    lg = logg_ref[...]
    cos = cos_ref[...]
    sin = sin_ref[...]
    half = RET_DK // 2

    def rope(x):
        return x * cos + pltpu.roll(x, half, 1) * sin

    q_all = rope(q_ref[...].astype(F32))
    k_all = rope(k_ref[...].astype(F32)) * (RET_DK ** -0.5)

    ri = lax.broadcasted_iota(I32, (blk, blk), 0)
    ci = lax.broadcasted_iota(I32, (blk, blk), 1)
    diff = (ri - ci).astype(F32)
    decay = jnp.where(diff >= 0, jnp.exp(lg[:, :1] * jnp.maximum(diff, 0.0)), 0.0)
    pos = lax.broadcasted_iota(I32, (blk, RET_DK), 0).astype(F32)
    xi = jnp.exp(lg * (pos + 1.0))
    zeta = jnp.exp(lg * (blk - 1.0 - pos))
    blk_decay = jnp.exp(lg * float(blk))

    state = jnp.zeros((RET_DK, RET_DV), F32)
    for n in range(seq // blk):
        sl = slice(n * blk, (n + 1) * blk)
        qc = q_all[sl]
        kc = k_all[sl]
        vc = v_ref[sl, :]
        att = lax.dot_general(qc.astype(BF16), kc.astype(BF16), (((1,), (1,)), ((), ())),
                              preferred_element_type=F32) * decay
        inner = jnp.dot(att.astype(BF16), vc, preferred_element_type=F32)
        cross = jnp.dot((qc * xi).astype(BF16), state.astype(BF16), preferred_element_type=F32)
        ret = inner + cross
        kz_t = (kc * zeta).T.astype(BF16)
        dstate = jnp.dot(kz_t, vc, preferred_element_type=F32)
        state = state * blk_decay + dstate
        mu = jnp.mean(ret, axis=-1, keepdims=True)
        cen = ret - mu
        var = jnp.mean(cen * cen, axis=-1, keepdims=True)
        normed = cen * lax.rsqrt(var + EPS) * gn_ref[...]
        gate = g_ref[sl, :].astype(F32)
        o_ref[sl, :] = (gate * jax.nn.sigmoid(gate) * normed).astype(BF16)


def retention(z, cos_t, sin_t, logg, gn_g, batch, seq):
    blk = min(RET_BLOCK, seq)
    t = batch * seq
    first = MLA_SECTION // LANES
    col = lambda part: (lambda b, h: (b, first + part * RET_HEADS + h))
    return pl.pallas_call(
        functools.partial(_retention_kernel, seq=seq, blk=blk),
        grid=(batch, RET_HEADS),
        in_specs=[
            pl.BlockSpec((seq, LANES), col(0)),
            pl.BlockSpec((seq, LANES), col(1)),
            pl.BlockSpec((seq, LANES), col(2)),
            pl.BlockSpec((seq, LANES), col(3)),
            pl.BlockSpec((seq, LANES), lambda b, h: (b, 0)),
            pl.BlockSpec((seq, LANES), lambda b, h: (b, 0)),
            pl.BlockSpec((None, 1, LANES), lambda b, h: (h, 0, 0)),
            pl.BlockSpec((None, 1, LANES), lambda b, h: (h, 0, 0)),
        ],
        out_specs=pl.BlockSpec((seq, RET_DV), lambda b, h: (b, h)),
        out_shape=jax.ShapeDtypeStruct((t, RET_HEADS * RET_DV), BF16),
        compiler_params=_cparams(("parallel", "parallel")),
        name="retention",
    )(z, z, z, z, cos_t, sin_t, logg, gn_g)


def _xattn_kernel(h_ref, g_ref, wq_ref, kv_ref, wo_ref, o_ref):
    h = h_ref[...]
    hn = _rms(h, g_ref[...]).astype(BF16)
    q = jnp.dot(hn, wq_ref[...], preferred_element_type=F32) * (XA_HEAD_DIM ** -0.5)
    q = q.astype(BF16)
    width = XA_HEADS * XA_HEAD_DIM
    outs = []
    for hd in range(XA_HEADS):
        sl = slice(hd * XA_HEAD_DIM, (hd + 1) * XA_HEAD_DIM)
        k = kv_ref[:, sl]
        v = kv_ref[:, width + hd * XA_HEAD_DIM:width + (hd + 1) * XA_HEAD_DIM]
        s = lax.dot_general(q[:, sl], k, (((1,), (1,)), ((), ())), preferred_element_type=F32)
        m = jnp.max(s, axis=-1, keepdims=True)
        p = jnp.exp(s - m)
        l = jnp.sum(p, axis=-1, keepdims=True)
        o = jnp.dot(p.astype(BF16), v, preferred_element_type=F32) / l
        outs.append(o.astype(BF16))
    o_all = jnp.concatenate(outs, axis=-1)
    o_ref[...] = h + jnp.dot(o_all, wo_ref[...], preferred_element_type=F32)


def cross_attention(h, g, wq, memkv, wo, seq, tm=512):
    t, d = h.shape
    tm = min(tm, seq)
    per_b = seq // tm
    m = memkv.shape[0] // (t // seq)
    full = lambda a: pl.BlockSpec(a.shape, lambda i: (0, 0))
    return pl.pallas_call(
        _xattn_kernel,
        grid=(t // tm,),
        in_specs=[
            pl.BlockSpec((tm, d), lambda i: (i, 0)),
            pl.BlockSpec((1, d), lambda i: (0, 0)),
            full(wq),
            pl.BlockSpec((m, memkv.shape[1]), lambda i: (i // per_b, 0)),
            full(wo),
        ],
        out_specs=pl.BlockSpec((tm, d), lambda i: (i, 0)),
        out_shape=jax.ShapeDtypeStruct((t, d), F32),
        compiler_params=_cparams(("parallel",)),
        name="cross_attention",
    )(h, g.reshape(1, d), wq, memkv, wo)


SC_PITCH = PEER_NKEYS + 8
SUBLANES = 8
TOPK_TOKENS = SUBLANES * LANES


def _peer_scores_kernel(h_ref, g_ref, wqt_ref, keys_ref, hn_ref, sc_ref, *, tm):
    hn = _rms(h_ref[...], g_ref[...]).astype(BF16)
    hn_ref[...] = hn
    q_t = lax.dot_general(wqt_ref[...], hn, (((1,), (1,)), ((), ())),
                          preferred_element_type=F32).astype(BF16)
    pad = jnp.zeros((SC_PITCH - PEER_NKEYS, LANES), F32)
    for grp in range(2 * PEER_HEADS):
        s = jnp.dot(keys_ref[grp], q_t[grp * PEER_HALF:(grp + 1) * PEER_HALF, :],
                    preferred_element_type=F32)
        for c in range(tm // LANES):
            sc_ref[grp, c * SC_PITCH:c * SC_PITCH + PEER_NKEYS, :] = s[:, c * LANES:(c + 1) * LANES]
            sc_ref[grp, c * SC_PITCH + PEER_NKEYS:(c + 1) * SC_PITCH, :] = pad


def peer_scores(h, g, wq_t, keys, tm=512):
    t, d = h.shape
    tm = min(tm, t)
    rows = (tm // LANES) * SC_PITCH
    return pl.pallas_call(
        functools.partial(_peer_scores_kernel, tm=tm),
        grid=(t // tm,),
        in_specs=[
            pl.BlockSpec((tm, d), lambda i: (i, 0)),
            pl.BlockSpec((1, d), lambda i: (0, 0)),
            pl.BlockSpec(wq_t.shape, lambda i: (0, 0)),
            pl.BlockSpec(keys.shape, lambda i: (0, 0, 0)),
        ],
        out_specs=[
            pl.BlockSpec((tm, d), lambda i: (i, 0)),
            pl.BlockSpec((2 * PEER_HEADS, rows, LANES), lambda i: (0, i, 0)),
        ],
        out_shape=[
            jax.ShapeDtypeStruct((t, d), BF16),
            jax.ShapeDtypeStruct((2 * PEER_HEADS, (t // LANES) * SC_PITCH, LANES), F32),
        ],
        compiler_params=_cparams(("parallel",)),
        name="peer_scores",
    )(h, g.reshape(1, d), wq_t, keys)


def _tree(op, xs):
    xs = list(xs)
    while len(xs) > 1:
        xs = [op(xs[i], xs[i + 1]) if i + 1 < len(xs) else xs[i] for i in range(0, len(xs), 2)]
    return xs[0]


def _sorting_network(n):
    def merge(lo, hi, r):
        step = 2 * r
        if step < hi - lo:
            yield from merge(lo, hi, step)
            yield from merge(lo + r, hi, step)
            yield from ((i, i + r) for i in range(lo + r, hi - r, step))
        else:
            yield (lo, lo + r)

    def sort(lo, hi):
        if hi > lo:
            mid = lo + (hi - lo) // 2
            yield from sort(lo, mid)
            yield from sort(mid + 1, hi)
            yield from merge(lo, hi, 1)

    return list(sort(0, n - 1))


_SORT16 = _sorting_network(PEER_TOPK)


def _sort_block(vals, ids):
    vals, ids = list(vals), list(ids)
    for i, j in _SORT16:
        va, vb, ia, ib = vals[i], vals[j], ids[i], ids[j]
        swap = (vb > va) | ((vb == va) & (ib < ia))
        vals[i], vals[j] = jnp.maximum(va, vb), jnp.minimum(va, vb)
        ids[i], ids[j] = jnp.where(swap, ib, ia), jnp.where(swap, ia, ib)
    return vals, ids


def _merge_top16(lists):
    def better(a, b):
        return jnp.maximum(a[0], b[0]), jnp.where(a[0] >= b[0], a[1], b[1])

    lists = [list(entries) for entries in lists]
    vals, idxs = [], []
    for r in range(PEER_TOPK):
        m, pos = _tree(better, [entries[0] for entries in lists if entries])
        vals.append(m)
        idxs.append(pos)
        keep = PEER_TOPK - 1 - r
        owner = jnp.floor(pos * (1.0 / PEER_TOPK))
        for n, entries in enumerate(lists):
            if not entries:
                continue
            won = owner == float(n)
            nxt = entries[1:] + [(NEG_INF, float(PEER_TOPK * n))]
            lists[n] = [(jnp.where(won, b[0], a[0]), jnp.where(won, b[1], a[1]))
                        for a, b in list(zip(entries, nxt))[:keep]]
    return vals, idxs


def _top16(rows):
    blocks = []
    for n in range(len(rows) // PEER_TOPK):
        base = n * PEER_TOPK
        ids = [jnp.full(rows[0].shape, float(base + k), F32) for k in range(PEER_TOPK)]
        vals, ids = _sort_block(rows[base:base + PEER_TOPK], ids)
        blocks.append(list(zip(vals, ids)))
    return _merge_top16(blocks)


def _pick(table, sel):
    out = table[0]
    for i in range(1, PEER_TOPK):
        out = jnp.where(sel == float(i), table[i], out)
    return out


_PAIRS = [(i, j) for i in range(PEER_TOPK) for j in range(PEER_TOPK) if (i + 1) * (j + 1) <= PEER_TOPK]


def _route_head(s1, s2):
    v1, i1 = _top16(s1)
    v2, i2 = _top16(s2)
    sums = [[(v1[i] + v2[j], float(i * PEER_TOPK + j)) for j in range(PEER_TOPK) if (i, j) in _PAIRS]
            for i in range(PEER_TOPK)]
    best, flat = _merge_top16(sums)
    probs = [jnp.exp(b - best[0]) for b in best]
    inv = 1.0 / _tree(jnp.add, probs)
    experts = []
    for r in range(PEER_TOPK):
        hi = jnp.floor(flat[r] * (1.0 / PEER_TOPK))
        experts.append(_pick(i1, hi) * PEER_NKEYS + _pick(i2, flat[r] - hi * PEER_TOPK))
    return experts, [p * inv for p in probs]


def _peer_topk_kernel(sc_ref, e_ref, gate_ref, esel_ref, gsel_ref):
    head = pl.program_id(1)

    def load(grp):
        return [sc_ref[grp, pl.ds(k, SUBLANES, stride=SC_PITCH), :] for k in range(PEER_NKEYS)]

    experts, gates = _route_head(load(0), load(1))
    for r in range(PEER_TOPK):
        row = pl.multiple_of((head * PEER_TOPK + r) * SUBLANES, SUBLANES)
        esel_ref[pl.ds(row, SUBLANES), :] = experts[r]
        gsel_ref[pl.ds(row, SUBLANES), :] = gates[r]

    @pl.when(head == PEER_HEADS - 1)
    def _():
        for s in range(SUBLANES):
            e_sel = esel_ref[pl.ds(s, PEER_SEL, stride=SUBLANES), :]
            g_sel = gsel_ref[pl.ds(s, PEER_SEL, stride=SUBLANES), :]
            e_ref[s * LANES:(s + 1) * LANES, :] = e_sel.T.astype(I32)
            gate_ref[s * LANES:(s + 1) * LANES, :] = g_sel.T


def peer_topk(scores):
    t = scores.shape[1] // SC_PITCH * LANES
    rows = SUBLANES * SC_PITCH
    return pl.pallas_call(
        _peer_topk_kernel,
        grid=(t // TOPK_TOKENS, PEER_HEADS),
        in_specs=[pl.BlockSpec((2, rows, LANES), lambda i, hd: (hd, i, 0))],
        out_specs=[
            pl.BlockSpec((TOPK_TOKENS, PEER_SEL), lambda i, hd: (i, 0)),
            pl.BlockSpec((TOPK_TOKENS, PEER_SEL), lambda i, hd: (i, 0)),
        ],
        out_shape=[
            jax.ShapeDtypeStruct((t, PEER_SEL), I32),
            jax.ShapeDtypeStruct((t, PEER_SEL), F32),
        ],
        scratch_shapes=[
            pltpu.VMEM((PEER_SEL * SUBLANES, LANES), F32),
            pltpu.VMEM((PEER_SEL * SUBLANES, LANES), F32),
        ],
        compiler_params=_cparams(("parallel", "arbitrary")),
        name="peer_topk",
    )(scores)


W_UNROLL = 32
W_SLABS = PEER_NKEYS // SUBLANES
U32 = jnp.uint32
BF16_HI = np.uint32(0xFFFF0000)
HALF_WORD = np.uint32(16)


def _token_gates(e_row, g_row, sub):
    a_hot = jnp.where(sub == lax.shift_right_logical(e_row, 7), 1.0, 0.0).astype(BF16)
    b_hot = jnp.where(sub == (e_row & (PEER_NKEYS - 1)), g_row, 0.0).astype(BF16)
    return lax.dot_general(a_hot, b_hot, (((1,), (1,)), ((), ())), preferred_element_type=F32)


def _bf16_bits(x):
    return lax.bitcast_convert_type(x.astype(BF16).astype(F32), U32)


def _peer_gates_kernel(e_hi_ref, g_hi_ref, e_lo_ref, g_lo_ref, w_ref, *, tw):
    sub = lax.broadcasted_iota(I32, (PEER_NKEYS, PEER_SEL), 0)

    def group(gi, carry):
        p0 = pl.multiple_of(gi * W_UNROLL, W_UNROLL)
        for i in range(W_UNROLL):
            rows = pl.ds(p0 + i, 1)
            w_hi = _token_gates(e_hi_ref[rows, :], g_hi_ref[rows, :], sub)
            w_lo = _token_gates(e_lo_ref[rows, :], g_lo_ref[rows, :], sub)
            packed = _bf16_bits(w_hi) | lax.shift_right_logical(_bf16_bits(w_lo), HALF_WORD)
            row = pl.multiple_of((p0 + i) * SUBLANES, SUBLANES)
            for s in range(W_SLABS):
                w_ref[s, pl.ds(row, SUBLANES), :] = packed[s * SUBLANES:(s + 1) * SUBLANES, :]
        return carry

    lax.fori_loop(0, tw // W_UNROLL, group, 0)


def peer_gates(e, g, tile, tw=128):
    t = e.shape[0]
    half = tile // 2
    tw = min(tw, half)
    per_half = half // tw
    hi = lambda i, c: (i * 2 * per_half + c, 0)
    lo = lambda i, c: (i * 2 * per_half + per_half + c, 0)
    return pl.pallas_call(
        functools.partial(_peer_gates_kernel, tw=tw),
        grid=(t // tile, per_half),
        in_specs=[
            pl.BlockSpec((tw, PEER_SEL), hi),
            pl.BlockSpec((tw, PEER_SEL), hi),
            pl.BlockSpec((tw, PEER_SEL), lo),
            pl.BlockSpec((tw, PEER_SEL), lo),
        ],
        out_specs=pl.BlockSpec((W_SLABS, tw * SUBLANES, PEER_NKEYS), lambda i, c: (0, i * per_half + c, 0)),
        out_shape=jax.ShapeDtypeStruct((W_SLABS, (t // 2) * SUBLANES, PEER_NKEYS), U32),
        compiler_params=_cparams(("parallel", "parallel")),
        name="peer_gates",
    )(e, g, e, g)


DENSE_CHUNK = 512
DENSE_VMEM_LIMIT = 60 * 1024 * 1024


def _peer_dense_kernel(hn_ref, u_ref, w_ref, v_ref, o_ref):
    j = pl.program_id(1)

    @pl.when(j == 0)
    def _():
        o_ref[...] = jnp.zeros_like(o_ref)

    hn = hn_ref[...]
    tm = hn.shape[0]
    te = u_ref.shape[0]
    k1_per_chunk = DENSE_CHUNK // PEER_NKEYS
    steps_per_slab = (SUBLANES * PEER_NKEYS) // te
    k1_base = (j % steps_per_slab) * (te // PEER_NKEYS)
    parts = []
    for c in range(te // DENSE_CHUNK):
        rows = slice(c * DENSE_CHUNK, (c + 1) * DENSE_CHUNK)
        pre = lax.dot_general(hn, u_ref[rows, :], (((1,), (1,)), ((), ())),
                              preferred_element_type=F32)
        act = 0.5 * pre * (1.0 + lax.erf(pre * (2.0 ** -0.5)))
        words = [w_ref[pl.ds(k1_base + c * k1_per_chunk + k, tm // 2, stride=SUBLANES), :]
                 for k in range(k1_per_chunk)]
        first = jnp.concatenate([lax.bitcast_convert_type(x & BF16_HI, F32) for x in words], axis=1)
        second = jnp.concatenate(
            [lax.bitcast_convert_type(lax.shift_left(x, HALF_WORD), F32) for x in words], axis=1)
        gate = jnp.concatenate([first, second], axis=0)
        parts.append((act * gate).astype(BF16))
    weighted = parts[0] if len(parts) == 1 else jnp.concatenate(parts, axis=1)
    o_ref[...] += jnp.dot(weighted, v_ref[...], preferred_element_type=F32)


DENSE_TOKENS = 1024


def peer_dense(hn, u, w, v, layer, tm, te=1024):
    t, d = hn.shape
    n_exp = u.shape[1]
    slab = SUBLANES * PEER_NKEYS
    return pl.pallas_call(
        _peer_dense_kernel,
        grid=(t // tm, n_exp // te),
        in_specs=[
            pl.BlockSpec((tm, d), lambda i, j: (i, 0)),
            pl.BlockSpec((None, te, d), lambda i, j: (layer, j, 0)),
            pl.BlockSpec((None, (tm // 2) * SUBLANES, PEER_NKEYS), lambda i, j: (j * te // slab, i, 0)),
            pl.BlockSpec((None, te, d), lambda i, j: (layer, j, 0)),
        ],
        out_specs=pl.BlockSpec((tm, d), lambda i, j: (i, 0)),
        out_shape=jax.ShapeDtypeStruct((t, d), F32),
        compiler_params=_cparams(("parallel", "arbitrary"), DENSE_VMEM_LIMIT),
        name="peer_dense",
    )(hn, u, w, v)


def _add_rmsnorm_kernel(a_ref, b_ref, g_ref, o_ref):
    o_ref[...] = _rms(a_ref[...] + b_ref[...], g_ref[...])


def add_rmsnorm(a, b, g, tm=512):
    t, d = a.shape
    tm = min(tm, t)
    return pl.pallas_call(
        _add_rmsnorm_kernel,
        grid=(t // tm,),
        in_specs=[
            pl.BlockSpec((tm, d), lambda i: (i, 0)),
            pl.BlockSpec((tm, d), lambda i: (i, 0)),
            pl.BlockSpec((1, d), lambda i: (0, 0)),
        ],
        out_specs=pl.BlockSpec((tm, d), lambda i: (i, 0)),
        out_shape=jax.ShapeDtypeStruct((t, d), F32),
        compiler_params=_cparams(("parallel",)),
        name="add_rmsnorm",
    )(a, b, g.reshape(1, d))


LRU_ROWS = 8


def _lru_kernel(gate_ref, rec_ref, cw_ref, cb_ref, wa_ref, ba_ref, wx_ref, bx_ref, lam_ref, o_ref,
                tail_ref, carry_ref, a_ref, u_ref, hs_ref, *, ts):
    @pl.when(pl.program_id(1) == 0)
    def _():
        tail_ref[...] = jnp.zeros_like(tail_ref)
        carry_ref[...] = jnp.zeros_like(carry_ref)

    x = rec_ref[...].astype(F32)
    xp = jnp.concatenate([tail_ref[...], x], axis=0)
    tail_ref[...] = x[ts - LRU_ROWS:, :]
    xc = cb_ref[...] + cw_ref[CONV_WIDTH - 1:CONV_WIDTH, :] * x
    for back in range(1, CONV_WIDTH):
        shifted = pltpu.roll(xp, back, 0)[LRU_ROWS:, :]
        xc = xc + cw_ref[CONV_WIDTH - 1 - back:CONV_WIDTH - back, :] * shifted

    soft = jnp.log(1.0 + jnp.exp(-lam_ref[...]))
    for n in range(LRU_BLOCKS):
        sl = slice(n * LRU_BLOCK, (n + 1) * LRU_BLOCK)
        xb = xc[:, sl]
        xb16 = xb.astype(BF16)
        r = jax.nn.sigmoid(jnp.dot(xb16, wa_ref[n], preferred_element_type=F32) + ba_ref[:, sl])
        i = jax.nn.sigmoid(jnp.dot(xb16, wx_ref[n], preferred_element_type=F32) + bx_ref[:, sl])
        log_a = -LRU_C * r * soft[:, sl]
        a = jnp.exp(log_a)
        a_ref[:, sl] = a
        u_ref[:, sl] = jnp.sqrt(1.0 - a * a) * (i * xb)

    groups = ts // LRU_ROWS
    a = a_ref[...].reshape(groups, LRU_ROWS, -1)
    u = u_ref[...].reshape(groups, LRU_ROWS, -1)
    sub = lax.broadcasted_iota(I32, a.shape, 1)
    for shift in (1, 2, 4):
        ok = sub >= shift
        u = u + a * jnp.where(ok, pltpu.roll(u, shift, 1), 0.0)
        a = a * jnp.where(ok, pltpu.roll(a, shift, 1), 1.0)
    a_ref[...] = a.reshape(ts, -1)
    u_ref[...] = u.reshape(ts, -1)

    def step(c, h):
        r0 = pl.multiple_of(c * LRU_ROWS, LRU_ROWS)
        h8 = u_ref[pl.ds(r0, LRU_ROWS), :] + a_ref[pl.ds(r0, LRU_ROWS), :] * h
        hs_ref[pl.ds(r0, LRU_ROWS), :] = h8
        return h8[LRU_ROWS - 1:, :]

    carry_ref[...] = lax.fori_loop(0, ts // LRU_ROWS, step, carry_ref[...])
    o_ref[...] = (jax.nn.gelu(gate_ref[...].astype(F32)) * hs_ref[...]).astype(BF16)


def lru_branch(z, conv_w, conv_b, wa, ba, wx, bx, lam, batch, seq, ts=256):
    t = z.shape[0]
    w = z.shape[1] // 2
    ts = min(ts, seq)
    per_b = seq // ts
    row = lambda a: pl.BlockSpec(a.shape, lambda b, s: (0, 0))
    cube = lambda a: pl.BlockSpec(a.shape, lambda b, s: (0, 0, 0))
    return pl.pallas_call(
        functools.partial(_lru_kernel, ts=ts),
        grid=(batch, per_b),
        in_specs=[
            pl.BlockSpec((ts, w), lambda b, s: (b * per_b + s, 0)),
            pl.BlockSpec((ts, w), lambda b, s: (b * per_b + s, 1)),
            row(conv_w), row(conv_b), cube(wa), row(ba), cube(wx), row(bx), row(lam),
        ],
        out_specs=pl.BlockSpec((ts, w), lambda b, s: (b * per_b + s, 0)),
        out_shape=jax.ShapeDtypeStruct((t, w), BF16),
        scratch_shapes=[
            pltpu.VMEM((LRU_ROWS, w), F32),
            pltpu.VMEM((1, w), F32),
            pltpu.VMEM((ts, w), F32),
            pltpu.VMEM((ts, w), F32),
            pltpu.VMEM((ts, w), F32),
        ],
        compiler_params=_cparams(("parallel", "arbitrary")),
        name="lru_branch",
    )(z, z, conv_w, conv_b, wa, ba, wx, bx, lam)


def _rope_tables(positions, dim, width):
    inv = ROPE_THETA ** (-jnp.arange(0, dim, 2, dtype=F32) / dim)
    ang = positions.astype(F32).reshape(-1, 1) * inv
    pad = jnp.zeros((ang.shape[0], width - dim), F32)
    cos = jnp.concatenate([jnp.cos(ang), jnp.cos(ang), pad], axis=-1)
    sin = jnp.concatenate([jnp.sin(ang), jnp.sin(ang), pad], axis=-1)
    return cos, sin


def _rotate_half_cols(w):
    half = w.shape[-1] // 2
    return jnp.concatenate([-w[..., half:], w[..., :half]], axis=-1)


def _even_weights(w_in, w_uq):
    d = w_in.shape[0]
    latent = MLA_Q_RANK + MLA_KV_RANK
    pad = LANES - MLA_ROPE
    z_in = jnp.zeros((d, pad), F32)
    k_rope = w_in[:, latent:latent + MLA_ROPE]
    w_in_p = jnp.concatenate(
        [w_in[:, :latent], k_rope, z_in, _rotate_half_cols(k_rope), z_in, w_in[:, latent + MLA_ROPE:]],
        axis=-1)
    wq3 = w_uq.reshape(MLA_Q_RANK, MLA_HEADS, MLA_NOPE + MLA_ROPE)
    nope, rope = wq3[..., :MLA_NOPE], wq3[..., MLA_NOPE:]
    zq = jnp.zeros((MLA_Q_RANK, MLA_HEADS, pad), F32)
    wqa = jnp.concatenate([nope, rope, zq], axis=-1).reshape(MLA_Q_RANK, MLA_HEADS * MLA_QK)
    wqb = jnp.concatenate([_rotate_half_cols(rope), zq], axis=-1).reshape(MLA_Q_RANK, MLA_HEADS * LANES)
    return w_in_p.astype(BF16), wqa.astype(BF16), wqb.astype(BF16)


def _memory_and_route(h, mem2, layer, seq, xa_norm_g, mem_norm_g, xa_wq, xa_wk, xa_wv, xa_wo,
                      ffn_norm_g, peer_wq, peer_sub_keys):
    w_kv = jnp.concatenate([xa_wk[layer], xa_wv[layer]], axis=-1).astype(BF16)
    memkv = norm_matmul([mem2], mem_norm_g[layer], w_kv)
    h = cross_attention(h, xa_norm_g[layer], xa_wq[layer].astype(BF16), memkv,
                        xa_wo[layer].astype(BF16), seq)
    keys = peer_sub_keys[layer].reshape(2 * PEER_HEADS, PEER_NKEYS, PEER_HALF).astype(BF16)
    hn, scores = peer_scores(h, ffn_norm_g[layer], peer_wq[layer].T.astype(BF16), keys)
    e_sel, g_sel = peer_topk(scores)
    return h, hn, peer_gates(e_sel, g_sel, min(DENSE_TOKENS, h.shape[0]))


def kernel(x, mem, positions, mix_norm_g, xa_norm_g, ffn_norm_g, mem_norm_g, ev_w_in, ev_w_out, mla_q_norm_g, mla_w_uq, mla_kv_norm_g, mla_w_ukv, ret_gn_g, od_w_in, od_w_out, conv_w, conv_b, lru_wa, lru_ba, lru_wx, lru_bx, lru_lambda, xa_wq, xa_wk, xa_wv, xa_wo, peer_wq, peer_sub_keys, peer_u, peer_v, final_norm_g):
    batch, seq, d = x.shape
    t = batch * seq
    h = x.reshape(t, d)
    mem2 = mem.reshape(-1, d)
    tail = (xa_norm_g, mem_norm_g, xa_wq, xa_wk, xa_wv, xa_wo, ffn_norm_g, peer_wq, peer_sub_keys)
    u16 = peer_u.astype(BF16)
    v16 = peer_v.astype(BF16)

    w_in_p, wqa, wqb = _even_weights(ev_w_in[0], mla_w_uq[0])
    z = norm_matmul([h], mix_norm_g[0], w_in_p)
    cos_m, sin_m = _rope_tables(positions, MLA_ROPE, LANES)
    qf, kvf, kr = mla_proj(z, cos_m, sin_m, mla_q_norm_g[0].reshape(1, -1),
                           mla_kv_norm_g[0].reshape(1, -1), wqa, wqb, mla_w_ukv[0].astype(BF16))
    attn = mla_attention(qf, kvf, kr, batch, seq)
    cos_r, sin_r = _rope_tables(positions, RET_DK, RET_DK)
    sign = jnp.concatenate([-jnp.ones((RET_DK // 2,), F32), jnp.ones((RET_DK // 2,), F32)])
    log_g = jnp.log(1.0 - 2.0 ** (-5.0 - jnp.arange(RET_HEADS, dtype=F32)))
    logg = jnp.broadcast_to(log_g[:, None, None], (RET_HEADS, 1, LANES))
    ret = retention(z, cos_r, sin_r * sign, logg, ret_gn_g[0].reshape(RET_HEADS, 1, RET_DV), batch, seq)
    w_out = ev_w_out[0].astype(BF16)
    n_a = MLA_HEADS * MLA_V
    h = matmul_res([attn, ret], [w_out[:n_a], w_out[n_a:]], [h])
    h, hn, w = _memory_and_route(h, mem2, 0, seq, *tail)
    tile = min(DENSE_TOKENS, t)
    ffn = peer_dense(hn, u16, w, v16, 0, tile)

    z = norm_matmul([h, ffn], mix_norm_g[1], od_w_in[0].astype(BF16))
    row = lambda a: a.reshape(1, -1)
    cw = jnp.concatenate([conv_w[0], jnp.zeros((SUBLANES - CONV_WIDTH, conv_w.shape[-1]), F32)], axis=0)
    y = lru_branch(z, cw, row(conv_b[0]), lru_wa[0].astype(BF16), row(lru_ba[0]),
                   lru_wx[0].astype(BF16), row(lru_bx[0]), row(lru_lambda[0]), batch, seq)
    h = matmul_res([y], [od_w_out[0].astype(BF16)], [h, ffn])
    h, hn, w = _memory_and_route(h, mem2, 1, seq, *tail)
    out = add_rmsnorm(h, peer_dense(hn, u16, w, v16, 1, tile), final_norm_g)
    return out.reshape(batch, seq, d)
```

```python
import functools

import jax
import jax.numpy as jnp
import numpy as np
from jax import lax
from jax.experimental import pallas as pl
from jax.experimental.pallas import tpu as pltpu

F32 = jnp.float32
BF16 = jnp.bfloat16
I32 = jnp.int32

EPS = 1e-6
ROPE_THETA = 10000.0

MLA_HEADS = 8
MLA_NOPE = 128
MLA_ROPE = 64
MLA_V = 128
MLA_Q_RANK = 512
MLA_KV_RANK = 256
MLA_QK = 256
MLA_SECTION = 1024

RET_HEADS = 8
RET_DK = 128
RET_DV = 128
RET_BLOCK = 256

LRU_BLOCKS = 16
LRU_BLOCK = 128
CONV_WIDTH = 4
LRU_C = 8.0

XA_HEADS = 4
XA_HEAD_DIM = 128

PEER_HEADS = 8
PEER_NKEYS = 128
PEER_HALF = 128
PEER_TOPK = 16
PEER_SEL = PEER_HEADS * PEER_TOPK

LANES = 128
VMEM_LIMIT = 48 * 1024 * 1024

NEG_INF = float("-inf")


def _cparams(sem, vmem=VMEM_LIMIT):
    return pltpu.CompilerParams(dimension_semantics=sem, vmem_limit_bytes=vmem)


def _rms(x, g):
    ms = jnp.mean(x * x, axis=-1, keepdims=True)
    return x * lax.rsqrt(ms + EPS) * g


def _norm_matmul_kernel(*refs, n_x):
    x_refs = refs[:n_x]
    g_ref, w_ref, o_ref, xn_ref = refs[n_x:]

    @pl.when(pl.program_id(1) == 0)
    def _():
        rows_per_chunk = min(NORM_ROWS, xn_ref.shape[0])
        for c in range(xn_ref.shape[0] // rows_per_chunk):
            rows = slice(c * rows_per_chunk, (c + 1) * rows_per_chunk)
            x = x_refs[0][rows, :]
            for extra in x_refs[1:]:
                x = x + extra[rows, :]
            xn_ref[rows, :] = _rms(x, g_ref[...]).astype(BF16)

    o_ref[...] = jnp.dot(xn_ref[...], w_ref[...], preferred_element_type=F32).astype(o_ref.dtype)


NORM_ROWS = 128
NORM_MATMUL_VMEM_LIMIT = 60 * 1024 * 1024


def norm_matmul(xs, g, w, tm=1024, tn=1024):
    t, d = xs[0].shape
    n = w.shape[1]
    tm = min(tm, t)
    tn = min(tn, n)
    return pl.pallas_call(
        functools.partial(_norm_matmul_kernel, n_x=len(xs)),
        grid=(t // tm, n // tn),
        in_specs=[pl.BlockSpec((tm, d), lambda i, j: (i, 0)) for _ in xs] + [
            pl.BlockSpec((1, d), lambda i, j: (0, 0)),
            pl.BlockSpec((d, tn), lambda i, j: (0, j)),
        ],
        out_specs=pl.BlockSpec((tm, tn), lambda i, j: (i, j)),
        out_shape=jax.ShapeDtypeStruct((t, n), BF16),
        scratch_shapes=[pltpu.VMEM((tm, d), BF16)],
        compiler_params=_cparams(("parallel", "arbitrary"), NORM_MATMUL_VMEM_LIMIT),
        name="norm_matmul",
    )(*xs, g.reshape(1, d), w)


def _matmul_res_kernel(*refs, n_in):
    xs = refs[:n_in]
    ws = refs[n_in:2 * n_in]
    res_refs = refs[2 * n_in:-1]
    o_ref = refs[-1]
    acc = res_refs[0][...]
    for res_ref in res_refs[1:]:
        acc = acc + res_ref[...]
    for x_ref, w_ref in zip(xs, ws):
        acc = acc + jnp.dot(x_ref[...], w_ref[...], preferred_element_type=F32)
    o_ref[...] = acc


def matmul_res(xs, ws, ress, tm=512, tn=2048):
    t, n = ress[0].shape
    tm = min(tm, t)
    tn = min(tn, n)
    n_in = len(xs)
    in_specs = [pl.BlockSpec((tm, x.shape[1]), lambda i, j: (i, 0)) for x in xs]
    in_specs += [pl.BlockSpec((w.shape[0], tn), lambda i, j: (0, j)) for w in ws]
    in_specs += [pl.BlockSpec((tm, tn), lambda i, j: (i, j)) for _ in ress]
    return pl.pallas_call(
        functools.partial(_matmul_res_kernel, n_in=n_in),
        grid=(t // tm, n // tn),
        in_specs=in_specs,
        out_specs=pl.BlockSpec((tm, tn), lambda i, j: (i, j)),
        out_shape=jax.ShapeDtypeStruct((t, n), F32),
        compiler_params=_cparams(("parallel", "parallel")),
        name="matmul_res",
    )(*xs, *ws, *ress)


def _mla_proj_kernel(z_ref, cos_ref, sin_ref, qg_ref, kvg_ref, wqa_ref, wqb_ref, wkv_ref,
                     q_ref, kv_ref, kr_ref):
    scale = (MLA_NOPE + MLA_ROPE) ** -0.5
    z = z_ref[...]
    cq = _rms(z[:, :MLA_Q_RANK].astype(F32), qg_ref[...]).astype(BF16)
    ckv = _rms(z[:, MLA_Q_RANK:MLA_Q_RANK + MLA_KV_RANK].astype(F32), kvg_ref[...]).astype(BF16)
    cos = cos_ref[...]
    sin = sin_ref[...]
    qa = jnp.dot(cq, wqa_ref[...], preferred_element_type=F32)
    qb = jnp.dot(cq, wqb_ref[...], preferred_element_type=F32)
    for h in range(MLA_HEADS):
        base = h * MLA_QK
        q_ref[:, base:base + LANES] = (qa[:, base:base + LANES] * scale).astype(BF16)
        rot = qa[:, base + LANES:base + 2 * LANES] * cos + qb[:, h * LANES:(h + 1) * LANES] * sin
        q_ref[:, base + LANES:base + 2 * LANES] = (rot * scale).astype(BF16)
    kv_ref[...] = jnp.dot(ckv, wkv_ref[...], preferred_element_type=F32).astype(BF16)
    off = MLA_Q_RANK + MLA_KV_RANK
    kr = z[:, off:off + LANES].astype(F32) * cos + z[:, off + LANES:off + 2 * LANES].astype(F32) * sin
    kr_ref[...] = kr.astype(BF16)


def mla_proj(z, cos_t, sin_t, qg, kvg, wqa, wqb, wkv, tm=512):
    t = z.shape[0]
    tm = min(tm, t)
    hq = MLA_HEADS * MLA_QK
    full = lambda a: pl.BlockSpec(a.shape, lambda i: (0, 0))
    return pl.pallas_call(
        _mla_proj_kernel,
        grid=(t // tm,),
        in_specs=[
            pl.BlockSpec((tm, MLA_SECTION), lambda i: (i, 0)),
            pl.BlockSpec((tm, LANES), lambda i: (i, 0)),
            pl.BlockSpec((tm, LANES), lambda i: (i, 0)),
            full(qg), full(kvg), full(wqa), full(wqb), full(wkv),
        ],
        out_specs=[
            pl.BlockSpec((tm, hq), lambda i: (i, 0)),
            pl.BlockSpec((tm, hq), lambda i: (i, 0)),
            pl.BlockSpec((tm, LANES), lambda i: (i, 0)),
        ],
        out_shape=[
            jax.ShapeDtypeStruct((t, hq), BF16),
            jax.ShapeDtypeStruct((t, hq), BF16),
            jax.ShapeDtypeStruct((t, LANES), BF16),
        ],
        compiler_params=_cparams(("parallel",)),
        name="mla_proj",
    )(z, cos_t, sin_t, qg, kvg, wqa, wqb, wkv)


def _mla_attn_kernel(q_ref, kv_ref, kr_ref, o_ref, kcat_ref, vpair_ref, *, seq, tq):
    for hd in range(2):
        kcat_ref[hd, :, :LANES] = kv_ref[:, hd * MLA_QK:hd * MLA_QK + LANES]
        kcat_ref[hd, :, LANES:] = kr_ref[...]
        vpair_ref[:, hd * MLA_V:(hd + 1) * MLA_V] = kv_ref[:, hd * MLA_QK + LANES:(hd + 1) * MLA_QK]
    for qi in range(seq // tq):
        kend = (qi + 1) * tq
        rows = slice(qi * tq, (qi + 1) * tq)
        row = lax.broadcasted_iota(I32, (tq, kend), 0) + qi * tq
        col = lax.broadcasted_iota(I32, (tq, kend), 1)
        probs, norms = [], []
        for hd in range(2):
            s = lax.dot_general(q_ref[rows, hd * MLA_QK:(hd + 1) * MLA_QK], kcat_ref[hd, :kend, :],
                                (((1,), (1,)), ((), ())), preferred_element_type=F32)
            s = jnp.where(col <= row, s, NEG_INF)
            p = jnp.exp(s - jnp.max(s, axis=-1, keepdims=True))
            norms.append(jnp.sum(p, axis=-1, keepdims=True))
            probs.append(p.astype(BF16))
        o = jnp.dot(jnp.concatenate(probs, axis=0), vpair_ref[:kend, :], preferred_element_type=F32)
        o_ref[rows, :MLA_V] = (o[:tq, :MLA_V] / norms[0]).astype(BF16)
        o_ref[rows, MLA_V:] = (o[tq:, MLA_V:] / norms[1]).astype(BF16)


def mla_attention(qf, kvf, kr, batch, seq, tq=256):
    tq = min(tq, seq)
    t = batch * seq
    return pl.pallas_call(
        functools.partial(_mla_attn_kernel, seq=seq, tq=tq),
        grid=(batch, MLA_HEADS // 2),
        in_specs=[
            pl.BlockSpec((seq, 2 * MLA_QK), lambda b, h: (b, h)),
            pl.BlockSpec((seq, 2 * MLA_QK), lambda b, h: (b, h)),
            pl.BlockSpec((seq, LANES), lambda b, h: (b, 0)),
        ],
        out_specs=pl.BlockSpec((seq, 2 * MLA_V), lambda b, h: (b, h)),
        out_shape=jax.ShapeDtypeStruct((t, MLA_HEADS * MLA_V), BF16),
        scratch_shapes=[pltpu.VMEM((2, seq, MLA_QK), BF16), pltpu.VMEM((seq, 2 * MLA_V), BF16)],
        compiler_params=_cparams(("parallel", "parallel")),
        name="mla_attention",
    )(qf, kvf, kr)


def _retention_kernel(q_ref, k_ref, v_ref, g_ref, cos_ref, sin_ref, logg_ref, gn_ref, o_ref,
                      *, seq, blk):
    lg = logg_ref[...]
    cos = cos_ref[...]
    sin = sin_ref[...]
    half = RET_DK // 2

    def rope(x):
        return x * cos + pltpu.roll(x, half, 1) * sin

    q_all = rope(q_ref[...].astype(F32))
    k_all = rope(k_ref[...].astype(F32)) * (RET_DK ** -0.5)

    ri = lax.broadcasted_iota(I32, (blk, blk), 0)
    ci = lax.broadcasted_iota(I32, (blk, blk), 1)
    diff = (ri - ci).astype(F32)
    decay = jnp.where(diff >= 0, jnp.exp(lg[:, :1] * jnp.maximum(diff, 0.0)), 0.0)
    pos = lax.broadcasted_iota(I32, (blk, RET_DK), 0).astype(F32)
    xi = jnp.exp(lg * (pos + 1.0))
    zeta = jnp.exp(lg * (blk - 1.0 - pos))
    blk_decay = jnp.exp(lg * float(blk))

    state = jnp.zeros((RET_DK, RET_DV), F32)
    for n in range(seq // blk):
        sl = slice(n * blk, (n + 1) * blk)
        qc = q_all[sl]
        kc = k_all[sl]
        vc = v_ref[sl, :]
        att = lax.dot_general(qc.astype(BF16), kc.astype(BF16), (((1,), (1,)), ((), ())),
                              preferred_element_type=F32) * decay
        inner = jnp.dot(att.astype(BF16), vc, preferred_element_type=F32)
        cross = jnp.dot((qc * xi).astype(BF16), state.astype(BF16), preferred_element_type=F32)
        ret = inner + cross
        kz_t = (kc * zeta).T.astype(BF16)
        dstate = jnp.dot(kz_t, vc, preferred_element_type=F32)
        state = state * blk_decay + dstate
        mu = jnp.mean(ret, axis=-1, keepdims=True)
        cen = ret - mu
        var = jnp.mean(cen * cen, axis=-1, keepdims=True)
        normed = cen * lax.rsqrt(var + EPS) * gn_ref[...]
        gate = g_ref[sl, :].astype(F32)
        o_ref[sl, :] = (gate * jax.nn.sigmoid(gate) * normed).astype(BF16)


def retention(z, cos_t, sin_t, logg, gn_g, batch, seq):
    blk = min(RET_BLOCK, seq)
    t = batch * seq
    first = MLA_SECTION // LANES
    col = lambda part: (lambda b, h: (b, first + part * RET_HEADS + h))
    return pl.pallas_call(
        functools.partial(_retention_kernel, seq=seq, blk=blk),
        grid=(batch, RET_HEADS),
        in_specs=[
            pl.BlockSpec((seq, LANES), col(0)),
            pl.BlockSpec((seq, LANES), col(1)),
            pl.BlockSpec((seq, LANES), col(2)),
            pl.BlockSpec((seq, LANES), col(3)),
            pl.BlockSpec((seq, LANES), lambda b, h: (b, 0)),
            pl.BlockSpec((seq, LANES), lambda b, h: (b, 0)),
            pl.BlockSpec((None, 1, LANES), lambda b, h: (h, 0, 0)),
            pl.BlockSpec((None, 1, LANES), lambda b, h: (h, 0, 0)),
        ],
        out_specs=pl.BlockSpec((seq, RET_DV), lambda b, h: (b, h)),
        out_shape=jax.ShapeDtypeStruct((t, RET_HEADS * RET_DV), BF16),
        compiler_params=_cparams(("parallel", "parallel")),
        name="retention",
    )(z, z, z, z, cos_t, sin_t, logg, gn_g)


def _xattn_kernel(h_ref, g_ref, wq_ref, kv_ref, wo_ref, o_ref):
    h = h_ref[...]
    hn = _rms(h, g_ref[...]).astype(BF16)
    q = jnp.dot(hn, wq_ref[...], preferred_element_type=F32) * (XA_HEAD_DIM ** -0.5)
    q = q.astype(BF16)
    width = XA_HEADS * XA_HEAD_DIM
    outs = []
    for hd in range(XA_HEADS):
        sl = slice(hd * XA_HEAD_DIM, (hd + 1) * XA_HEAD_DIM)
        k = kv_ref[:, sl]
        v = kv_ref[:, width + hd * XA_HEAD_DIM:width + (hd + 1) * XA_HEAD_DIM]
        s = lax.dot_general(q[:, sl], k, (((1,), (1,)), ((), ())), preferred_element_type=F32)
        m = jnp.max(s, axis=-1, keepdims=True)
        p = jnp.exp(s - m)
        l = jnp.sum(p, axis=-1, keepdims=True)
        o = jnp.dot(p.astype(BF16), v, preferred_element_type=F32) / l
        outs.append(o.astype(BF16))
    o_all = jnp.concatenate(outs, axis=-1)
    o_ref[...] = h + jnp.dot(o_all, wo_ref[...], preferred_element_type=F32)


def cross_attention(h, g, wq, memkv, wo, seq, tm=512):
    t, d = h.shape
    tm = min(tm, seq)
    per_b = seq // tm
    m = memkv.shape[0] // (t // seq)
    full = lambda a: pl.BlockSpec(a.shape, lambda i: (0, 0))
    return pl.pallas_call(
        _xattn_kernel,
        grid=(t // tm,),
        in_specs=[
            pl.BlockSpec((tm, d), lambda i: (i, 0)),
            pl.BlockSpec((1, d), lambda i: (0, 0)),
            full(wq),
            pl.BlockSpec((m, memkv.shape[1]), lambda i: (i // per_b, 0)),
            full(wo),
        ],
        out_specs=pl.BlockSpec((tm, d), lambda i: (i, 0)),
        out_shape=jax.ShapeDtypeStruct((t, d), F32),
        compiler_params=_cparams(("parallel",)),
        name="cross_attention",
    )(h, g.reshape(1, d), wq, memkv, wo)


SC_PITCH = PEER_NKEYS + 8
SUBLANES = 8
TOPK_TOKENS = SUBLANES * LANES


def _peer_scores_kernel(h_ref, g_ref, wqt_ref, keys_ref, hn_ref, sc_ref, *, tm):
    hn = _rms(h_ref[...], g_ref[...]).astype(BF16)
    hn_ref[...] = hn
    q_t = lax.dot_general(wqt_ref[...], hn, (((1,), (1,)), ((), ())),
                          preferred_element_type=F32).astype(BF16)
    pad = jnp.zeros((SC_PITCH - PEER_NKEYS, LANES), F32)
    for grp in range(2 * PEER_HEADS):
        s = jnp.dot(keys_ref[grp], q_t[grp * PEER_HALF:(grp + 1) * PEER_HALF, :],
                    preferred_element_type=F32)
        for c in range(tm // LANES):
            sc_ref[grp, c * SC_PITCH:c * SC_PITCH + PEER_NKEYS, :] = s[:, c * LANES:(c + 1) * LANES]
            sc_ref[grp, c * SC_PITCH + PEER_NKEYS:(c + 1) * SC_PITCH, :] = pad


def peer_scores(h, g, wq_t, keys, tm=512):
    t, d = h.shape
    tm = min(tm, t)
    rows = (tm // LANES) * SC_PITCH
    return pl.pallas_call(
        functools.partial(_peer_scores_kernel, tm=tm),
        grid=(t // tm,),
        in_specs=[
            pl.BlockSpec((tm, d), lambda i: (i, 0)),
            pl.BlockSpec((1, d), lambda i: (0, 0)),
            pl.BlockSpec(wq_t.shape, lambda i: (0, 0)),
            pl.BlockSpec(keys.shape, lambda i: (0, 0, 0)),
        ],
        out_specs=[
            pl.BlockSpec((tm, d), lambda i: (i, 0)),
            pl.BlockSpec((2 * PEER_HEADS, rows, LANES), lambda i: (0, i, 0)),
        ],
        out_shape=[
            jax.ShapeDtypeStruct((t, d), BF16),
            jax.ShapeDtypeStruct((2 * PEER_HEADS, (t // LANES) * SC_PITCH, LANES), F32),
        ],
        compiler_params=_cparams(("parallel",)),
        name="peer_scores",
    )(h, g.reshape(1, d), wq_t, keys)


def _tree(op, xs):
    xs = list(xs)
    while len(xs) > 1:
        xs = [op(xs[i], xs[i + 1]) if i + 1 < len(xs) else xs[i] for i in range(0, len(xs), 2)]
    return xs[0]


def _sorting_network(n):
    def merge(lo, hi, r):
        step = 2 * r
        if step < hi - lo:
            yield from merge(lo, hi, step)
            yield from merge(lo + r, hi, step)
            yield from ((i, i + r) for i in range(lo + r, hi - r, step))
        else:
            yield (lo, lo + r)

    def sort(lo, hi):
        if hi > lo:
            mid = lo + (hi - lo) // 2
            yield from sort(lo, mid)
            yield from sort(mid + 1, hi)
            yield from merge(lo, hi, 1)

    return list(sort(0, n - 1))


_SORT16 = _sorting_network(PEER_TOPK)


def _sort_block(vals, ids):
    vals, ids = list(vals), list(ids)
    for i, j in _SORT16:
        va, vb, ia, ib = vals[i], vals[j], ids[i], ids[j]
        swap = (vb > va) | ((vb == va) & (ib < ia))
        vals[i], vals[j] = jnp.maximum(va, vb), jnp.minimum(va, vb)
        ids[i], ids[j] = jnp.where(swap, ib, ia), jnp.where(swap, ia, ib)
    return vals, ids


def _merge_top16(lists):
    def better(a, b):
        return jnp.maximum(a[0], b[0]), jnp.where(a[0] >= b[0], a[1], b[1])

    lists = [list(entries) for entries in lists]
    vals, idxs = [], []
    for r in range(PEER_TOPK):
        m, pos = _tree(better, [entries[0] for entries in lists if entries])
        vals.append(m)
        idxs.append(pos)
        keep = PEER_TOPK - 1 - r
        owner = jnp.floor(pos * (1.0 / PEER_TOPK))
        for n, entries in enumerate(lists):
            if not entries:
                continue
            won = owner == float(n)
            nxt = entries[1:] + [(NEG_INF, float(PEER_TOPK * n))]
            lists[n] = [(jnp.where(won, b[0], a[0]), jnp.where(won, b[1], a[1]))
                        for a, b in list(zip(entries, nxt))[:keep]]
    return vals, idxs


def _top16(rows):
    blocks = []
    for n in range(len(rows) // PEER_TOPK):
        base = n * PEER_TOPK
        ids = [jnp.full(rows[0].shape, float(base + k), F32) for k in range(PEER_TOPK)]
        vals, ids = _sort_block(rows[base:base + PEER_TOPK], ids)
        blocks.append(list(zip(vals, ids)))
    return _merge_top16(blocks)


def _pick(table, sel):
    out = table[0]
    for i in range(1, PEER_TOPK):
        out = jnp.where(sel == float(i), table[i], out)
    return out


_PAIRS = [(i, j) for i in range(PEER_TOPK) for j in range(PEER_TOPK) if (i + 1) * (j + 1) <= PEER_TOPK]


def _route_head(s1, s2):
    v1, i1 = _top16(s1)
    v2, i2 = _top16(s2)
    sums = [[(v1[i] + v2[j], float(i * PEER_TOPK + j)) for j in range(PEER_TOPK) if (i, j) in _PAIRS]
            for i in range(PEER_TOPK)]
    best, flat = _merge_top16(sums)
    probs = [jnp.exp(b - best[0]) for b in best]
    inv = 1.0 / _tree(jnp.add, probs)
    experts = []
    for r in range(PEER_TOPK):
        hi = jnp.floor(flat[r] * (1.0 / PEER_TOPK))
        experts.append(_pick(i1, hi) * PEER_NKEYS + _pick(i2, flat[r] - hi * PEER_TOPK))
    return experts, [p * inv for p in probs]


def _peer_topk_kernel(sc_ref, e_ref, gate_ref, esel_ref, gsel_ref):
    head = pl.program_id(1)

    def load(grp):
        return [sc_ref[grp, pl.ds(k, SUBLANES, stride=SC_PITCH), :] for k in range(PEER_NKEYS)]

    experts, gates = _route_head(load(0), load(1))
    for r in range(PEER_TOPK):
        row = pl.multiple_of((head * PEER_TOPK + r) * SUBLANES, SUBLANES)
        esel_ref[pl.ds(row, SUBLANES), :] = experts[r]
        gsel_ref[pl.ds(row, SUBLANES), :] = gates[r]

    @pl.when(head == PEER_HEADS - 1)
    def _():
        for s in range(SUBLANES):
            e_sel = esel_ref[pl.ds(s, PEER_SEL, stride=SUBLANES), :]
            g_sel = gsel_ref[pl.ds(s, PEER_SEL, stride=SUBLANES), :]
            e_ref[s * LANES:(s + 1) * LANES, :] = e_sel.T.astype(I32)
            gate_ref[s * LANES:(s + 1) * LANES, :] = g_sel.T


def peer_topk(scores):
    t = scores.shape[1] // SC_PITCH * LANES
    rows = SUBLANES * SC_PITCH
    return pl.pallas_call(
        _peer_topk_kernel,
        grid=(t // TOPK_TOKENS, PEER_HEADS),
        in_specs=[pl.BlockSpec((2, rows, LANES), lambda i, hd: (hd, i, 0))],
        out_specs=[
            pl.BlockSpec((TOPK_TOKENS, PEER_SEL), lambda i, hd: (i, 0)),
            pl.BlockSpec((TOPK_TOKENS, PEER_SEL), lambda i, hd: (i, 0)),
        ],
        out_shape=[
            jax.ShapeDtypeStruct((t, PEER_SEL), I32),
            jax.ShapeDtypeStruct((t, PEER_SEL), F32),
        ],
        scratch_shapes=[
            pltpu.VMEM((PEER_SEL * SUBLANES, LANES), F32),
            pltpu.VMEM((PEER_SEL * SUBLANES, LANES), F32),
        ],
        compiler_params=_cparams(("parallel", "arbitrary")),
        name="peer_topk",
    )(scores)


W_UNROLL = 32
W_SLABS = PEER_NKEYS // SUBLANES
U32 = jnp.uint32
BF16_HI = np.uint32(0xFFFF0000)
HALF_WORD = np.uint32(16)


def _token_gates(e_row, g_row, sub):
    a_hot = jnp.where(sub == lax.shift_right_logical(e_row, 7), 1.0, 0.0).astype(BF16)
    b_hot = jnp.where(sub == (e_row & (PEER_NKEYS - 1)), g_row, 0.0).astype(BF16)
    return lax.dot_general(a_hot, b_hot, (((1,), (1,)), ((), ())), preferred_element_type=F32)


def _bf16_bits(x):
    return lax.bitcast_convert_type(x.astype(BF16).astype(F32), U32)


def _peer_gates_kernel(e_hi_ref, g_hi_ref, e_lo_ref, g_lo_ref, w_ref, *, tw):
    sub = lax.broadcasted_iota(I32, (PEER_NKEYS, PEER_SEL), 0)

    def group(gi, carry):
        p0 = pl.multiple_of(gi * W_UNROLL, W_UNROLL)
        for i in range(W_UNROLL):
            rows = pl.ds(p0 + i, 1)
            w_hi = _token_gates(e_hi_ref[rows, :], g_hi_ref[rows, :], sub)
            w_lo = _token_gates(e_lo_ref[rows, :], g_lo_ref[rows, :], sub)
            packed = _bf16_bits(w_hi) | lax.shift_right_logical(_bf16_bits(w_lo), HALF_WORD)
            row = pl.multiple_of((p0 + i) * SUBLANES, SUBLANES)
            for s in range(W_SLABS):
                w_ref[s, pl.ds(row, SUBLANES), :] = packed[s * SUBLANES:(s + 1) * SUBLANES, :]
        return carry

    lax.fori_loop(0, tw // W_UNROLL, group, 0)


def peer_gates(e, g, tile, tw=128):
    t = e.shape[0]
    half = tile // 2
    tw = min(tw, half)
    per_half = half // tw
    hi = lambda i, c: (i * 2 * per_half + c, 0)
    lo = lambda i, c: (i * 2 * per_half + per_half + c, 0)
    return pl.pallas_call(
        functools.partial(_peer_gates_kernel, tw=tw),
        grid=(t // tile, per_half),
        in_specs=[
            pl.BlockSpec((tw, PEER_SEL), hi),
            pl.BlockSpec((tw, PEER_SEL), hi),
            pl.BlockSpec((tw, PEER_SEL), lo),
            pl.BlockSpec((tw, PEER_SEL), lo),
        ],
        out_specs=pl.BlockSpec((W_SLABS, tw * SUBLANES, PEER_NKEYS), lambda i, c: (0, i * per_half + c, 0)),
        out_shape=jax.ShapeDtypeStruct((W_SLABS, (t // 2) * SUBLANES, PEER_NKEYS), U32),
        compiler_params=_cparams(("parallel", "parallel")),
        name="peer_gates",
    )(e, g, e, g)


DENSE_CHUNK = 512
DENSE_VMEM_LIMIT = 60 * 1024 * 1024


def _peer_act_kernel(hn_ref, u_ref, w_ref, a_ref):
    hn = hn_ref[...]
    tm = hn.shape[0]
    te = u_ref.shape[0]
    k1_per_chunk = DENSE_CHUNK // PEER_NKEYS
    k1_base = 0
    parts = []
    for c in range(te // DENSE_CHUNK):
        rows = slice(c * DENSE_CHUNK, (c + 1) * DENSE_CHUNK)
        pre = lax.dot_general(hn, u_ref[rows, :], (((1,), (1,)), ((), ())),
                              preferred_element_type=F32)
        act = 0.5 * pre * (1.0 + lax.erf(pre * (2.0 ** -0.5)))
        words = [w_ref[pl.ds(k1_base + c * k1_per_chunk + k, tm // 2, stride=SUBLANES), :]
                 for k in range(k1_per_chunk)]
        first = jnp.concatenate([lax.bitcast_convert_type(x & BF16_HI, F32) for x in words], axis=1)
        second = jnp.concatenate(
            [lax.bitcast_convert_type(lax.shift_left(x, HALF_WORD), F32) for x in words], axis=1)
        gate = jnp.concatenate([first, second], axis=0)
        parts.append((act * gate).astype(BF16))
    a_ref[...] = parts[0] if len(parts) == 1 else jnp.concatenate(parts, axis=1)


def _peer_out_kernel(a_ref, v_ref, res_ref, fg_ref, o_ref, *, final_norm):
    j = pl.program_id(1)

    @pl.when(j == 0)
    def _():
        o_ref[...] = res_ref[...]

    o_ref[...] += jnp.dot(a_ref[...], v_ref[...], preferred_element_type=F32)

    if final_norm:
        @pl.when(j == pl.num_programs(1) - 1)
        def _():
            o_ref[...] = _rms(o_ref[...], fg_ref[...])


DENSE_TOKENS = 1024


def peer_dense(hn, u, w, v, layer, res, final_g, tm):
    t, d = hn.shape
    n_exp = u.shape[1]
    te = SUBLANES * PEER_NKEYS
    act = pl.pallas_call(
        _peer_act_kernel,
        grid=(t // tm, n_exp // te),
        in_specs=[
            pl.BlockSpec((tm, d), lambda i, j: (i, 0)),
            pl.BlockSpec((None, te, d), lambda i, j: (layer, j, 0)),
            pl.BlockSpec((None, (tm // 2) * SUBLANES, PEER_NKEYS), lambda i, j: (j, i, 0)),
        ],
        out_specs=pl.BlockSpec((tm, te), lambda i, j: (i, j)),
        out_shape=jax.ShapeDtypeStruct((t, n_exp), BF16),
        compiler_params=_cparams(("parallel", "parallel")),
        name="peer_act",
    )(hn, u, w)
    final_norm = final_g is not None
    fg = final_g if final_norm else jnp.ones((d,), F32)
    return pl.pallas_call(
        functools.partial(_peer_out_kernel, final_norm=final_norm),
        grid=(t // tm, n_exp // te),
        in_specs=[
            pl.BlockSpec((tm, te), lambda i, j: (i, j)),
            pl.BlockSpec((None, te, d), lambda i, j: (layer, j, 0)),
            pl.BlockSpec((tm, d), lambda i, j: (i, 0)),
            pl.BlockSpec((1, d), lambda i, j: (0, 0)),
        ],
        out_specs=pl.BlockSpec((tm, d), lambda i, j: (i, 0)),
        out_shape=jax.ShapeDtypeStruct((t, d), F32),
        compiler_params=_cparams(("parallel", "arbitrary"), DENSE_VMEM_LIMIT),
        name="peer_out",
    )(act, v, res, fg.reshape(1, d))


def _add_rmsnorm_kernel(a_ref, b_ref, g_ref, o_ref):
    o_ref[...] = _rms(a_ref[...] + b_ref[...], g_ref[...])


def add_rmsnorm(a, b, g, tm=512):
    t, d = a.shape
    tm = min(tm, t)
    return pl.pallas_call(
        _add_rmsnorm_kernel,
        grid=(t // tm,),
        in_specs=[
            pl.BlockSpec((tm, d), lambda i: (i, 0)),
            pl.BlockSpec((tm, d), lambda i: (i, 0)),
            pl.BlockSpec((1, d), lambda i: (0, 0)),
        ],
        out_specs=pl.BlockSpec((tm, d), lambda i: (i, 0)),
        out_shape=jax.ShapeDtypeStruct((t, d), F32),
        compiler_params=_cparams(("parallel",)),
        name="add_rmsnorm",
    )(a, b, g.reshape(1, d))


LRU_ROWS = 8


def _lru_kernel(gate_ref, rec_ref, cw_ref, cb_ref, wa_ref, ba_ref, wx_ref, bx_ref, lam_ref, o_ref,
                tail_ref, carry_ref, a_ref, u_ref, hs_ref, *, ts):
    @pl.when(pl.program_id(1) == 0)
    def _():
        tail_ref[...] = jnp.zeros_like(tail_ref)
        carry_ref[...] = jnp.zeros_like(carry_ref)

    x = rec_ref[...].astype(F32)
    xp = jnp.concatenate([tail_ref[...], x], axis=0)
    tail_ref[...] = x[ts - LRU_ROWS:, :]
    xc = cb_ref[...] + cw_ref[CONV_WIDTH - 1:CONV_WIDTH, :] * x
    for back in range(1, CONV_WIDTH):
        shifted = pltpu.roll(xp, back, 0)[LRU_ROWS:, :]
        xc = xc + cw_ref[CONV_WIDTH - 1 - back:CONV_WIDTH - back, :] * shifted

    soft = jnp.log(1.0 + jnp.exp(-lam_ref[...]))
    for n in range(LRU_BLOCKS):
        sl = slice(n * LRU_BLOCK, (n + 1) * LRU_BLOCK)
        xb = xc[:, sl]
        xb16 = xb.astype(BF16)
        r = jax.nn.sigmoid(jnp.dot(xb16, wa_ref[n], preferred_element_type=F32) + ba_ref[:, sl])
        i = jax.nn.sigmoid(jnp.dot(xb16, wx_ref[n], preferred_element_type=F32) + bx_ref[:, sl])
        log_a = -LRU_C * r * soft[:, sl]
        a = jnp.exp(log_a)
        a_ref[:, sl] = a
        u_ref[:, sl] = jnp.sqrt(1.0 - a * a) * (i * xb)

    groups = ts // LRU_ROWS
    a = a_ref[...].reshape(groups, LRU_ROWS, -1)
    u = u_ref[...].reshape(groups, LRU_ROWS, -1)
    sub = lax.broadcasted_iota(I32, a.shape, 1)
    for shift in (1, 2, 4):
        ok = sub >= shift
        u = u + a * jnp.where(ok, pltpu.roll(u, shift, 1), 0.0)
        a = a * jnp.where(ok, pltpu.roll(a, shift, 1), 1.0)
    a_ref[...] = a.reshape(ts, -1)
    u_ref[...] = u.reshape(ts, -1)

    def step(c, h):
        r0 = pl.multiple_of(c * LRU_ROWS, LRU_ROWS)
        h8 = u_ref[pl.ds(r0, LRU_ROWS), :] + a_ref[pl.ds(r0, LRU_ROWS), :] * h
        hs_ref[pl.ds(r0, LRU_ROWS), :] = h8
        return h8[LRU_ROWS - 1:, :]

    carry_ref[...] = lax.fori_loop(0, ts // LRU_ROWS, step, carry_ref[...])
    o_ref[...] = (jax.nn.gelu(gate_ref[...].astype(F32)) * hs_ref[...]).astype(BF16)


def lru_branch(z, conv_w, conv_b, wa, ba, wx, bx, lam, batch, seq, ts=256):
    t = z.shape[0]
    w = z.shape[1] // 2
    ts = min(ts, seq)
    per_b = seq // ts
    row = lambda a: pl.BlockSpec(a.shape, lambda b, s: (0, 0))
    cube = lambda a: pl.BlockSpec(a.shape, lambda b, s: (0, 0, 0))
    return pl.pallas_call(
        functools.partial(_lru_kernel, ts=ts),
        grid=(batch, per_b),
        in_specs=[
            pl.BlockSpec((ts, w), lambda b, s: (b * per_b + s, 0)),
            pl.BlockSpec((ts, w), lambda b, s: (b * per_b + s, 1)),
            row(conv_w), row(conv_b), cube(wa), row(ba), cube(wx), row(bx), row(lam),
        ],
        out_specs=pl.BlockSpec((ts, w), lambda b, s: (b * per_b + s, 0)),
        out_shape=jax.ShapeDtypeStruct((t, w), BF16),
        scratch_shapes=[
            pltpu.VMEM((LRU_ROWS, w), F32),
            pltpu.VMEM((1, w), F32),
            pltpu.VMEM((ts, w), F32),
            pltpu.VMEM((ts, w), F32),
            pltpu.VMEM((ts, w), F32),
        ],
        compiler_params=_cparams(("parallel", "arbitrary")),
        name="lru_branch",
    )(z, z, conv_w, conv_b, wa, ba, wx, bx, lam)


def _rope_tables(positions, dim, width):
    inv = ROPE_THETA ** (-jnp.arange(0, dim, 2, dtype=F32) / dim)
    ang = positions.astype(F32).reshape(-1, 1) * inv
    pad = jnp.zeros((ang.shape[0], width - dim), F32)
    cos = jnp.concatenate([jnp.cos(ang), jnp.cos(ang), pad], axis=-1)
    sin = jnp.concatenate([jnp.sin(ang), jnp.sin(ang), pad], axis=-1)
    return cos, sin


def _rotate_half_cols(w):
    half = w.shape[-1] // 2
    return jnp.concatenate([-w[..., half:], w[..., :half]], axis=-1)


def _even_weights(w_in, w_uq):
    d = w_in.shape[0]
    latent = MLA_Q_RANK + MLA_KV_RANK
    pad = LANES - MLA_ROPE
    z_in = jnp.zeros((d, pad), F32)
    k_rope = w_in[:, latent:latent + MLA_ROPE]
    w_in_p = jnp.concatenate(
        [w_in[:, :latent], k_rope, z_in, _rotate_half_cols(k_rope), z_in, w_in[:, latent + MLA_ROPE:]],
        axis=-1)
    wq3 = w_uq.reshape(MLA_Q_RANK, MLA_HEADS, MLA_NOPE + MLA_ROPE)
    nope, rope = wq3[..., :MLA_NOPE], wq3[..., MLA_NOPE:]
    zq = jnp.zeros((MLA_Q_RANK, MLA_HEADS, pad), F32)
    wqa = jnp.concatenate([nope, rope, zq], axis=-1).reshape(MLA_Q_RANK, MLA_HEADS * MLA_QK)
    wqb = jnp.concatenate([_rotate_half_cols(rope), zq], axis=-1).reshape(MLA_Q_RANK, MLA_HEADS * LANES)
    return w_in_p.astype(BF16), wqa.astype(BF16), wqb.astype(BF16)


def _memory_and_route(h, mem2, layer, seq, xa_norm_g, mem_norm_g, xa_wq, xa_wk, xa_wv, xa_wo,
                      ffn_norm_g, peer_wq, peer_sub_keys):
    w_kv = jnp.concatenate([xa_wk[layer], xa_wv[layer]], axis=-1).astype(BF16)
    memkv = norm_matmul([mem2], mem_norm_g[layer], w_kv)
    h = cross_attention(h, xa_norm_g[layer], xa_wq[layer].astype(BF16), memkv,
                        xa_wo[layer].astype(BF16), seq)
    keys = peer_sub_keys[layer].reshape(2 * PEER_HEADS, PEER_NKEYS, PEER_HALF).astype(BF16)
    hn, scores = peer_scores(h, ffn_norm_g[layer], peer_wq[layer].T.astype(BF16), keys)
    e_sel, g_sel = peer_topk(scores)
    return h, hn, peer_gates(e_sel, g_sel, min(DENSE_TOKENS, h.shape[0]))


def kernel(x, mem, positions, mix_norm_g, xa_norm_g, ffn_norm_g, mem_norm_g, ev_w_in, ev_w_out, mla_q_norm_g, mla_w_uq, mla_kv_norm_g, mla_w_ukv, ret_gn_g, od_w_in, od_w_out, conv_w, conv_b, lru_wa, lru_ba, lru_wx, lru_bx, lru_lambda, xa_wq, xa_wk, xa_wv, xa_wo, peer_wq, peer_sub_keys, peer_u, peer_v, final_norm_g):
    batch, seq, d = x.shape
    t = batch * seq
    h = x.reshape(t, d)
    mem2 = mem.reshape(-1, d)
    tail = (xa_norm_g, mem_norm_g, xa_wq, xa_wk, xa_wv, xa_wo, ffn_norm_g, peer_wq, peer_sub_keys)
    u16 = peer_u.astype(BF16)
    v16 = peer_v.astype(BF16)

    w_in_p, wqa, wqb = _even_weights(ev_w_in[0], mla_w_uq[0])
    z = norm_matmul([h], mix_norm_g[0], w_in_p)
    cos_m, sin_m = _rope_tables(positions, MLA_ROPE, LANES)
    qf, kvf, kr = mla_proj(z, cos_m, sin_m, mla_q_norm_g[0].reshape(1, -1),
                           mla_kv_norm_g[0].reshape(1, -1), wqa, wqb, mla_w_ukv[0].astype(BF16))
    attn = mla_attention(qf, kvf, kr, batch, seq)
    cos_r, sin_r = _rope_tables(positions, RET_DK, RET_DK)
    sign = jnp.concatenate([-jnp.ones((RET_DK // 2,), F32), jnp.ones((RET_DK // 2,), F32)])
    log_g = jnp.log(1.0 - 2.0 ** (-5.0 - jnp.arange(RET_HEADS, dtype=F32)))
    logg = jnp.broadcast_to(log_g[:, None, None], (RET_HEADS, 1, LANES))
    ret = retention(z, cos_r, sin_r * sign, logg, ret_gn_g[0].reshape(RET_HEADS, 1, RET_DV), batch, seq)
    w_out = ev_w_out[0].astype(BF16)
    n_a = MLA_HEADS * MLA_V
    h = matmul_res([attn, ret], [w_out[:n_a], w_out[n_a:]], [h])
    h, hn, w = _memory_and_route(h, mem2, 0, seq, *tail)
    tile = min(DENSE_TOKENS, t)
    h = peer_dense(hn, u16, w, v16, 0, h, None, tile)

    z = norm_matmul([h], mix_norm_g[1], od_w_in[0].astype(BF16))
    row = lambda a: a.reshape(1, -1)
    cw = jnp.concatenate([conv_w[0], jnp.zeros((SUBLANES - CONV_WIDTH, conv_w.shape[-1]), F32)], axis=0)
    y = lru_branch(z, cw, row(conv_b[0]), lru_wa[0].astype(BF16), row(lru_ba[0]),
                   lru_wx[0].astype(BF16), row(lru_bx[0]), row(lru_lambda[0]), batch, seq)
    h = matmul_res([y], [od_w_out[0].astype(BF16)], [h])
    h, hn, w = _memory_and_route(h, mem2, 1, seq, *tail)
    out = peer_dense(hn, u16, w, v16, 1, h, final_norm_g, tile)
    return out.reshape(batch, seq, d)
```

```python
import functools

import jax
import jax.numpy as jnp
import numpy as np
from jax import lax
from jax.experimental import pallas as pl
from jax.experimental.pallas import tpu as pltpu

F32 = jnp.float32
BF16 = jnp.bfloat16
I32 = jnp.int32

EPS = 1e-6
ROPE_THETA = 10000.0

MLA_HEADS = 8
MLA_NOPE = 128
MLA_ROPE = 64
MLA_V = 128
MLA_Q_RANK = 512
MLA_KV_RANK = 256
MLA_QK = 256
MLA_SECTION = 1024

RET_HEADS = 8
RET_DK = 128
RET_DV = 128
RET_BLOCK = 256

LRU_BLOCKS = 16
LRU_BLOCK = 128
CONV_WIDTH = 4
LRU_C = 8.0

XA_HEADS = 4
XA_HEAD_DIM = 128

PEER_HEADS = 8
PEER_NKEYS = 128
PEER_HALF = 128
PEER_TOPK = 16
PEER_SEL = PEER_HEADS * PEER_TOPK

LANES = 128
VMEM_LIMIT = 48 * 1024 * 1024

NEG_INF = float("-inf")


def _cparams(sem, vmem=VMEM_LIMIT):
    return pltpu.CompilerParams(dimension_semantics=sem, vmem_limit_bytes=vmem)


def _rms(x, g):
    ms = jnp.mean(x * x, axis=-1, keepdims=True)
    return x * lax.rsqrt(ms + EPS) * g


def _norm_matmul_kernel(*refs, n_x):
    x_refs = refs[:n_x]
    g_ref, w_ref, o_ref, xn_ref = refs[n_x:]

    @pl.when(pl.program_id(1) == 0)
    def _():
        rows_per_chunk = min(NORM_ROWS, xn_ref.shape[0])
        for c in range(xn_ref.shape[0] // rows_per_chunk):
            rows = slice(c * rows_per_chunk, (c + 1) * rows_per_chunk)
            x = x_refs[0][rows, :]
            for extra in x_refs[1:]:
                x = x + extra[rows, :]
            xn_ref[rows, :] = _rms(x, g_ref[...]).astype(BF16)

    o_ref[...] = jnp.dot(xn_ref[...], w_ref[...], preferred_element_type=F32).astype(o_ref.dtype)


NORM_ROWS = 128
NORM_MATMUL_VMEM_LIMIT = 60 * 1024 * 1024


def norm_matmul(xs, g, w, tm=1024, tn=1024):
    t, d = xs[0].shape
    n = w.shape[1]
    tm = min(tm, t)
    tn = min(tn, n)
    return pl.pallas_call(
        functools.partial(_norm_matmul_kernel, n_x=len(xs)),
        grid=(t // tm, n // tn),
        in_specs=[pl.BlockSpec((tm, d), lambda i, j: (i, 0)) for _ in xs] + [
            pl.BlockSpec((1, d), lambda i, j: (0, 0)),
            pl.BlockSpec((d, tn), lambda i, j: (0, j)),
        ],
        out_specs=pl.BlockSpec((tm, tn), lambda i, j: (i, j)),
        out_shape=jax.ShapeDtypeStruct((t, n), BF16),
        scratch_shapes=[pltpu.VMEM((tm, d), BF16)],
        compiler_params=_cparams(("parallel", "arbitrary"), NORM_MATMUL_VMEM_LIMIT),
        name="norm_matmul",
    )(*xs, g.reshape(1, d), w)


def _matmul_res_kernel(*refs, n_in):
    xs = refs[:n_in]
    ws = refs[n_in:2 * n_in]
    res_refs = refs[2 * n_in:-1]
    o_ref = refs[-1]
    acc = res_refs[0][...]
    for res_ref in res_refs[1:]:
        acc = acc + res_ref[...]
    for x_ref, w_ref in zip(xs, ws):
        acc = acc + jnp.dot(x_ref[...], w_ref[...], preferred_element_type=F32)
    o_ref[...] = acc


def matmul_res(xs, ws, ress, tm=512, tn=2048):
    t, n = ress[0].shape
    tm = min(tm, t)
    tn = min(tn, n)
    n_in = len(xs)
    in_specs = [pl.BlockSpec((tm, x.shape[1]), lambda i, j: (i, 0)) for x in xs]
    in_specs += [pl.BlockSpec((w.shape[0], tn), lambda i, j: (0, j)) for w in ws]
    in_specs += [pl.BlockSpec((tm, tn), lambda i, j: (i, j)) for _ in ress]
    return pl.pallas_call(
        functools.partial(_matmul_res_kernel, n_in=n_in),
        grid=(t // tm, n // tn),
        in_specs=in_specs,
        out_specs=pl.BlockSpec((tm, tn), lambda i, j: (i, j)),
        out_shape=jax.ShapeDtypeStruct((t, n), F32),
        compiler_params=_cparams(("parallel", "parallel")),
        name="matmul_res",
    )(*xs, *ws, *ress)


def _mla_proj_kernel(z_ref, cos_ref, sin_ref, qg_ref, kvg_ref, wqa_ref, wqb_ref, wkv_ref,
                     q_ref, kv_ref, kr_ref):
    scale = (MLA_NOPE + MLA_ROPE) ** -0.5
    z = z_ref[...]
    cq = _rms(z[:, :MLA_Q_RANK].astype(F32), qg_ref[...]).astype(BF16)
    ckv = _rms(z[:, MLA_Q_RANK:MLA_Q_RANK + MLA_KV_RANK].astype(F32), kvg_ref[...]).astype(BF16)
    cos = cos_ref[...]
    sin = sin_ref[...]
    qa = jnp.dot(cq, wqa_ref[...], preferred_element_type=F32)
    qb = jnp.dot(cq, wqb_ref[...], preferred_element_type=F32)
    for h in range(MLA_HEADS):
        base = h * MLA_QK
        q_ref[:, base:base + LANES] = (qa[:, base:base + LANES] * scale).astype(BF16)
        rot = qa[:, base + LANES:base + 2 * LANES] * cos + qb[:, h * LANES:(h + 1) * LANES] * sin
        q_ref[:, base + LANES:base + 2 * LANES] = (rot * scale).astype(BF16)
    kv_ref[...] = jnp.dot(ckv, wkv_ref[...], preferred_element_type=F32).astype(BF16)
    off = MLA_Q_RANK + MLA_KV_RANK
    kr = z[:, off:off + LANES].astype(F32) * cos + z[:, off + LANES:off + 2 * LANES].astype(F32) * sin
    kr_ref[...] = kr.astype(BF16)


def mla_proj(z, cos_t, sin_t, qg, kvg, wqa, wqb, wkv, tm=512):
    t = z.shape[0]
    tm = min(tm, t)
    hq = MLA_HEADS * MLA_QK
    full = lambda a: pl.BlockSpec(a.shape, lambda i: (0, 0))
    return pl.pallas_call(
        _mla_proj_kernel,
        grid=(t // tm,),
        in_specs=[
            pl.BlockSpec((tm, MLA_SECTION), lambda i: (i, 0)),
            pl.BlockSpec((tm, LANES), lambda i: (i, 0)),
            pl.BlockSpec((tm, LANES), lambda i: (i, 0)),
            full(qg), full(kvg), full(wqa), full(wqb), full(wkv),
        ],
        out_specs=[
            pl.BlockSpec((tm, hq), lambda i: (i, 0)),
            pl.BlockSpec((tm, hq), lambda i: (i, 0)),
            pl.BlockSpec((tm, LANES), lambda i: (i, 0)),
        ],
        out_shape=[
            jax.ShapeDtypeStruct((t, hq), BF16),
            jax.ShapeDtypeStruct((t, hq), BF16),
            jax.ShapeDtypeStruct((t, LANES), BF16),
        ],
        compiler_params=_cparams(("parallel",)),
        name="mla_proj",
    )(z, cos_t, sin_t, qg, kvg, wqa, wqb, wkv)


def _mla_attn_kernel(q_ref, kv_ref, kr_ref, o_ref, kcat_ref, vpair_ref, *, seq, tq):
    for hd in range(2):
        kcat_ref[hd, :, :LANES] = kv_ref[:, hd * MLA_QK:hd * MLA_QK + LANES]
        kcat_ref[hd, :, LANES:] = kr_ref[...]
        vpair_ref[:, hd * MLA_V:(hd + 1) * MLA_V] = kv_ref[:, hd * MLA_QK + LANES:(hd + 1) * MLA_QK]
    for qi in range(seq // tq):
        kend = (qi + 1) * tq
        rows = slice(qi * tq, (qi + 1) * tq)
        row = lax.broadcasted_iota(I32, (tq, kend), 0) + qi * tq
        col = lax.broadcasted_iota(I32, (tq, kend), 1)
        probs, norms = [], []
        for hd in range(2):
            s = lax.dot_general(q_ref[rows, hd * MLA_QK:(hd + 1) * MLA_QK], kcat_ref[hd, :kend, :],
                                (((1,), (1,)), ((), ())), preferred_element_type=F32)
            s = jnp.where(col <= row, s, NEG_INF)
            p = jnp.exp(s - jnp.max(s, axis=-1, keepdims=True))
            norms.append(jnp.sum(p, axis=-1, keepdims=True))
            probs.append(p.astype(BF16))
        o = jnp.dot(jnp.concatenate(probs, axis=0), vpair_ref[:kend, :], preferred_element_type=F32)
        o_ref[rows, :MLA_V] = (o[:tq, :MLA_V] / norms[0]).astype(BF16)
        o_ref[rows, MLA_V:] = (o[tq:, MLA_V:] / norms[1]).astype(BF16)


def mla_attention(qf, kvf, kr, batch, seq, tq=256):
    tq = min(tq, seq)
    t = batch * seq
    return pl.pallas_call(
        functools.partial(_mla_attn_kernel, seq=seq, tq=tq),
        grid=(batch, MLA_HEADS // 2),
        in_specs=[
            pl.BlockSpec((seq, 2 * MLA_QK), lambda b, h: (b, h)),
            pl.BlockSpec((seq, 2 * MLA_QK), lambda b, h: (b, h)),
            pl.BlockSpec((seq, LANES), lambda b, h: (b, 0)),
        ],
        out_specs=pl.BlockSpec((seq, 2 * MLA_V), lambda b, h: (b, h)),
        out_shape=jax.ShapeDtypeStruct((t, MLA_HEADS * MLA_V), BF16),
        scratch_shapes=[pltpu.VMEM((2, seq, MLA_QK), BF16), pltpu.VMEM((seq, 2 * MLA_V), BF16)],
        compiler_params=_cparams(("parallel", "parallel")),
        name="mla_attention",
    )(qf, kvf, kr)


def _retention_kernel(q_ref, k_ref, v_ref, g_ref, cos_ref, sin_ref, logg_ref, gn_ref, o_ref,
                      *, seq, blk):
    lg = logg_ref[...]
    cos = cos_ref[...]
    sin = sin_ref[...]
    half = RET_DK // 2

    def rope(x):
        return x * cos + pltpu.roll(x, half, 1) * sin

    q_all = rope(q_ref[...].astype(F32))
    k_all = rope(k_ref[...].astype(F32)) * (RET_DK ** -0.5)

    ri = lax.broadcasted_iota(I32, (blk, blk), 0)
    ci = lax.broadcasted_iota(I32, (blk, blk), 1)
    diff = (ri - ci).astype(F32)
    decay = jnp.where(diff >= 0, jnp.exp(lg[:, :1] * jnp.maximum(diff, 0.0)), 0.0)
    pos = lax.broadcasted_iota(I32, (blk, RET_DK), 0).astype(F32)
    xi = jnp.exp(lg * (pos + 1.0))
    zeta = jnp.exp(lg * (blk - 1.0 - pos))
    blk_decay = jnp.exp(lg * float(blk))

    state = jnp.zeros((RET_DK, RET_DV), F32)
    for n in range(seq // blk):
        sl = slice(n * blk, (n + 1) * blk)
        qc = q_all[sl]
        kc = k_all[sl]
        vc = v_ref[sl, :]
        att = lax.dot_general(qc.astype(BF16), kc.astype(BF16), (((1,), (1,)), ((), ())),
                              preferred_element_type=F32) * decay
        inner = jnp.dot(att.astype(BF16), vc, preferred_element_type=F32)
        cross = jnp.dot((qc * xi).astype(BF16), state.astype(BF16), preferred_element_type=F32)
        ret = inner + cross
        kz_t = (kc * zeta).T.astype(BF16)
        dstate = jnp.dot(kz_t, vc, preferred_element_type=F32)
        state = state * blk_decay + dstate
        mu = jnp.mean(ret, axis=-1, keepdims=True)
        cen = ret - mu
        var = jnp.mean(cen * cen, axis=-1, keepdims=True)
        normed = cen * lax.rsqrt(var + EPS) * gn_ref[...]
        gate = g_ref[sl, :].astype(F32)
        o_ref[sl, :] = (gate * jax.nn.sigmoid(gate) * normed).astype(BF16)


def retention(z, cos_t, sin_t, logg, gn_g, batch, seq):
    blk = min(RET_BLOCK, seq)
    t = batch * seq
    first = MLA_SECTION // LANES
    col = lambda part: (lambda b, h: (b, first + part * RET_HEADS + h))
    return pl.pallas_call(
        functools.partial(_retention_kernel, seq=seq, blk=blk),
        grid=(batch, RET_HEADS),
        in_specs=[
            pl.BlockSpec((seq, LANES), col(0)),
            pl.BlockSpec((seq, LANES), col(1)),
            pl.BlockSpec((seq, LANES), col(2)),
            pl.BlockSpec((seq, LANES), col(3)),
            pl.BlockSpec((seq, LANES), lambda b, h: (b, 0)),
            pl.BlockSpec((seq, LANES), lambda b, h: (b, 0)),
            pl.BlockSpec((None, 1, LANES), lambda b, h: (h, 0, 0)),
            pl.BlockSpec((None, 1, LANES), lambda b, h: (h, 0, 0)),
        ],
        out_specs=pl.BlockSpec((seq, RET_DV), lambda b, h: (b, h)),
        out_shape=jax.ShapeDtypeStruct((t, RET_HEADS * RET_DV), BF16),
        compiler_params=_cparams(("parallel", "parallel")),
        name="retention",
    )(z, z, z, z, cos_t, sin_t, logg, gn_g)


def _xattn_kernel(h_ref, g_ref, wq_ref, kv_ref, wo_ref, o_ref):
    h = h_ref[...]
    hn = _rms(h, g_ref[...]).astype(BF16)
    q = jnp.dot(hn, wq_ref[...], preferred_element_type=F32) * (XA_HEAD_DIM ** -0.5)
    q = q.astype(BF16)
    width = XA_HEADS * XA_HEAD_DIM
    outs = []
    for hd in range(XA_HEADS):
        sl = slice(hd * XA_HEAD_DIM, (hd + 1) * XA_HEAD_DIM)
        k = kv_ref[:, sl]
        v = kv_ref[:, width + hd * XA_HEAD_DIM:width + (hd + 1) * XA_HEAD_DIM]
        s = lax.dot_general(q[:, sl], k, (((1,), (1,)), ((), ())), preferred_element_type=F32)
        m = jnp.max(s, axis=-1, keepdims=True)
        p = jnp.exp(s - m)
        l = jnp.sum(p, axis=-1, keepdims=True)
        o = jnp.dot(p.astype(BF16), v, preferred_element_type=F32) / l
        outs.append(o.astype(BF16))
    o_all = jnp.concatenate(outs, axis=-1)
    o_ref[...] = h + jnp.dot(o_all, wo_ref[...], preferred_element_type=F32)


def cross_attention(h, g, wq, memkv, wo, seq, tm=512):
    t, d = h.shape
    tm = min(tm, seq)
    per_b = seq // tm
    m = memkv.shape[0] // (t // seq)
    full = lambda a: pl.BlockSpec(a.shape, lambda i: (0, 0))
    return pl.pallas_call(
        _xattn_kernel,
        grid=(t // tm,),
        in_specs=[
            pl.BlockSpec((tm, d), lambda i: (i, 0)),
            pl.BlockSpec((1, d), lambda i: (0, 0)),
            full(wq),
            pl.BlockSpec((m, memkv.shape[1]), lambda i: (i // per_b, 0)),
            full(wo),
        ],
        out_specs=pl.BlockSpec((tm, d), lambda i: (i, 0)),
        out_shape=jax.ShapeDtypeStruct((t, d), F32),
        compiler_params=_cparams(("parallel",)),
        name="cross_attention",
    )(h, g.reshape(1, d), wq, memkv, wo)


SC_PITCH = PEER_NKEYS + 8
SUBLANES = 8
TOPK_TOKENS = SUBLANES * LANES


def _peer_scores_kernel(h_ref, g_ref, wqt_ref, keys_ref, hn_ref, sc_ref, *, tm):
    hn = _rms(h_ref[...], g_ref[...]).astype(BF16)
    hn_ref[...] = hn
    q_t = lax.dot_general(wqt_ref[...], hn, (((1,), (1,)), ((), ())),
                          preferred_element_type=F32).astype(BF16)
    pad = jnp.zeros((SC_PITCH - PEER_NKEYS, LANES), F32)
    for grp in range(2 * PEER_HEADS):
        s = jnp.dot(keys_ref[grp], q_t[grp * PEER_HALF:(grp + 1) * PEER_HALF, :],
                    preferred_element_type=F32)
        for c in range(tm // LANES):
            sc_ref[grp, c * SC_PITCH:c * SC_PITCH + PEER_NKEYS, :] = s[:, c * LANES:(c + 1) * LANES]
            sc_ref[grp, c * SC_PITCH + PEER_NKEYS:(c + 1) * SC_PITCH, :] = pad


def peer_scores(h, g, wq_t, keys, tm=512):
    t, d = h.shape
    tm = min(tm, t)
    rows = (tm // LANES) * SC_PITCH
    return pl.pallas_call(
        functools.partial(_peer_scores_kernel, tm=tm),
        grid=(t // tm,),
        in_specs=[
            pl.BlockSpec((tm, d), lambda i: (i, 0)),
            pl.BlockSpec((1, d), lambda i: (0, 0)),
            pl.BlockSpec(wq_t.shape, lambda i: (0, 0)),
            pl.BlockSpec(keys.shape, lambda i: (0, 0, 0)),
        ],
        out_specs=[
            pl.BlockSpec((tm, d), lambda i: (i, 0)),
            pl.BlockSpec((2 * PEER_HEADS, rows, LANES), lambda i: (0, i, 0)),
        ],
        out_shape=[
            jax.ShapeDtypeStruct((t, d), BF16),
            jax.ShapeDtypeStruct((2 * PEER_HEADS, (t // LANES) * SC_PITCH, LANES), F32),
        ],
        compiler_params=_cparams(("parallel",)),
        name="peer_scores",
    )(h, g.reshape(1, d), wq_t, keys)


def _tree(op, xs):
    xs = list(xs)
    while len(xs) > 1:
        xs = [op(xs[i], xs[i + 1]) if i + 1 < len(xs) else xs[i] for i in range(0, len(xs), 2)]
    return xs[0]


def _sorting_network(n):
    def merge(lo, hi, r):
        step = 2 * r
        if step < hi - lo:
            yield from merge(lo, hi, step)
            yield from merge(lo + r, hi, step)
            yield from ((i, i + r) for i in range(lo + r, hi - r, step))
        else:
            yield (lo, lo + r)

    def sort(lo, hi):
        if hi > lo:
            mid = lo + (hi - lo) // 2
            yield from sort(lo, mid)
            yield from sort(mid + 1, hi)
            yield from merge(lo, hi, 1)

    return list(sort(0, n - 1))


_SORT16 = _sorting_network(PEER_TOPK)


def _sort_block(vals, ids):
    vals, ids = list(vals), list(ids)
    for i, j in _SORT16:
        va, vb, ia, ib = vals[i], vals[j], ids[i], ids[j]
        swap = (vb > va) | ((vb == va) & (ib < ia))
        vals[i], vals[j] = jnp.maximum(va, vb), jnp.minimum(va, vb)
        ids[i], ids[j] = jnp.where(swap, ib, ia), jnp.where(swap, ia, ib)
    return vals, ids


def _merge_top16(lists):
    def better(a, b):
        return jnp.maximum(a[0], b[0]), jnp.where(a[0] >= b[0], a[1], b[1])

    lists = [list(entries) for entries in lists]
    vals, idxs = [], []
    for r in range(PEER_TOPK):
        m, pos = _tree(better, [entries[0] for entries in lists if entries])
        vals.append(m)
        idxs.append(pos)
        keep = PEER_TOPK - 1 - r
        owner = jnp.floor(pos * (1.0 / PEER_TOPK))
        for n, entries in enumerate(lists):
            if not entries:
                continue
            won = owner == float(n)
            nxt = entries[1:] + [(NEG_INF, float(PEER_TOPK * n))]
            lists[n] = [(jnp.where(won, b[0], a[0]), jnp.where(won, b[1], a[1]))
                        for a, b in list(zip(entries, nxt))[:keep]]
    return vals, idxs


def _top16(rows):
    blocks = []
    for n in range(len(rows) // PEER_TOPK):
        base = n * PEER_TOPK
        ids = [jnp.full(rows[0].shape, float(base + k), F32) for k in range(PEER_TOPK)]
        vals, ids = _sort_block(rows[base:base + PEER_TOPK], ids)
        blocks.append(list(zip(vals, ids)))
    return _merge_top16(blocks)


def _pick(table, sel):
    out = table[0]
    for i in range(1, PEER_TOPK):
        out = jnp.where(sel == float(i), table[i], out)
    return out


_PAIRS = [(i, j) for i in range(PEER_TOPK) for j in range(PEER_TOPK) if (i + 1) * (j + 1) <= PEER_TOPK]


def _route_head(s1, s2):
    v1, i1 = _top16(s1)
    v2, i2 = _top16(s2)
    sums = [[(v1[i] + v2[j], float(i * PEER_TOPK + j)) for j in range(PEER_TOPK) if (i, j) in _PAIRS]
            for i in range(PEER_TOPK)]
    best, flat = _merge_top16(sums)
    probs = [jnp.exp(b - best[0]) for b in best]
    inv = 1.0 / _tree(jnp.add, probs)
    experts = []
    for r in range(PEER_TOPK):
        hi = jnp.floor(flat[r] * (1.0 / PEER_TOPK))
        experts.append(_pick(i1, hi) * PEER_NKEYS + _pick(i2, flat[r] - hi * PEER_TOPK))
    return experts, [p * inv for p in probs]


def _peer_topk_kernel(sc_ref, e_ref, gate_ref, esel_ref, gsel_ref):
    head = pl.program_id(1)

    def load(grp):
        return [sc_ref[grp, pl.ds(k, SUBLANES, stride=SC_PITCH), :] for k in range(PEER_NKEYS)]

    experts, gates = _route_head(load(0), load(1))
    for r in range(PEER_TOPK):
        row = pl.multiple_of((head * PEER_TOPK + r) * SUBLANES, SUBLANES)
        esel_ref[pl.ds(row, SUBLANES), :] = experts[r]
        gsel_ref[pl.ds(row, SUBLANES), :] = gates[r]

    @pl.when(head == PEER_HEADS - 1)
    def _():
        for s in range(SUBLANES):
            e_sel = esel_ref[pl.ds(s, PEER_SEL, stride=SUBLANES), :]
            g_sel = gsel_ref[pl.ds(s, PEER_SEL, stride=SUBLANES), :]
            e_ref[s * LANES:(s + 1) * LANES, :] = e_sel.T.astype(I32)
            gate_ref[s * LANES:(s + 1) * LANES, :] = g_sel.T


def peer_topk(scores):
    t = scores.shape[1] // SC_PITCH * LANES
    rows = SUBLANES * SC_PITCH
    return pl.pallas_call(
        _peer_topk_kernel,
        grid=(t // TOPK_TOKENS, PEER_HEADS),
        in_specs=[pl.BlockSpec((2, rows, LANES), lambda i, hd: (hd, i, 0))],
        out_specs=[
            pl.BlockSpec((TOPK_TOKENS, PEER_SEL), lambda i, hd: (i, 0)),
            pl.BlockSpec((TOPK_TOKENS, PEER_SEL), lambda i, hd: (i, 0)),
        ],
        out_shape=[
            jax.ShapeDtypeStruct((t, PEER_SEL), I32),
            jax.ShapeDtypeStruct((t, PEER_SEL), F32),
        ],
        scratch_shapes=[
            pltpu.VMEM((PEER_SEL * SUBLANES, LANES), F32),
            pltpu.VMEM((PEER_SEL * SUBLANES, LANES), F32),
        ],
        compiler_params=_cparams(("parallel", "arbitrary")),
        name="peer_topk",
    )(scores)


W_UNROLL = 32
W_SLABS = PEER_NKEYS // SUBLANES
U32 = jnp.uint32
BF16_HI = np.uint32(0xFFFF0000)
HALF_WORD = np.uint32(16)


def _token_gates(e_row, g_row, sub):
    a_hot = jnp.where(sub == lax.shift_right_logical(e_row, 7), 1.0, 0.0).astype(BF16)
    b_hot = jnp.where(sub == (e_row & (PEER_NKEYS - 1)), g_row, 0.0).astype(BF16)
    return lax.dot_general(a_hot, b_hot, (((1,), (1,)), ((), ())), preferred_element_type=F32)


def _bf16_bits(x):
    return lax.bitcast_convert_type(x.astype(BF16).astype(F32), U32)


def _peer_gates_kernel(e_hi_ref, g_hi_ref, e_lo_ref, g_lo_ref, w_ref, *, tw):
    sub = lax.broadcasted_iota(I32, (PEER_NKEYS, PEER_SEL), 0)

    def group(gi, carry):
        p0 = pl.multiple_of(gi * W_UNROLL, W_UNROLL)
        for i in range(W_UNROLL):
            rows = pl.ds(p0 + i, 1)
            w_hi = _token_gates(e_hi_ref[rows, :], g_hi_ref[rows, :], sub)
            w_lo = _token_gates(e_lo_ref[rows, :], g_lo_ref[rows, :], sub)
            packed = _bf16_bits(w_hi) | lax.shift_right_logical(_bf16_bits(w_lo), HALF_WORD)
            row = pl.multiple_of((p0 + i) * SUBLANES, SUBLANES)
            for s in range(W_SLABS):
                w_ref[s, pl.ds(row, SUBLANES), :] = packed[s * SUBLANES:(s + 1) * SUBLANES, :]
        return carry

    lax.fori_loop(0, tw // W_UNROLL, group, 0)


def peer_gates(e, g, tile, tw=128):
    t = e.shape[0]
    half = tile // 2
    tw = min(tw, half)
    per_half = half // tw
    hi = lambda i, c: (i * 2 * per_half + c, 0)
    lo = lambda i, c: (i * 2 * per_half + per_half + c, 0)
    return pl.pallas_call(
        functools.partial(_peer_gates_kernel, tw=tw),
        grid=(t // tile, per_half),
        in_specs=[
            pl.BlockSpec((tw, PEER_SEL), hi),
            pl.BlockSpec((tw, PEER_SEL), hi),
            pl.BlockSpec((tw, PEER_SEL), lo),
            pl.BlockSpec((tw, PEER_SEL), lo),
        ],
        out_specs=pl.BlockSpec((W_SLABS, tw * SUBLANES, PEER_NKEYS), lambda i, c: (0, i * per_half + c, 0)),
        out_shape=jax.ShapeDtypeStruct((W_SLABS, (t // 2) * SUBLANES, PEER_NKEYS), U32),
        compiler_params=_cparams(("parallel", "parallel")),
        name="peer_gates",
    )(e, g, e, g)


DENSE_CHUNK = 512
DENSE_VMEM_LIMIT = 60 * 1024 * 1024


def _peer_act_kernel(hn_ref, u_ref, w_ref, a_ref):
    hn = hn_ref[...]
    tm = hn.shape[0]
    te = u_ref.shape[0]
    k1_per_chunk = DENSE_CHUNK // PEER_NKEYS
    chunks_per_slab = SUBLANES // k1_per_chunk
    parts = []
    for c in range(te // DENSE_CHUNK):
        slab_ref = w_ref.at[c // chunks_per_slab]
        k1_base = (c % chunks_per_slab) * k1_per_chunk
        rows = slice(c * DENSE_CHUNK, (c + 1) * DENSE_CHUNK)
        pre = lax.dot_general(hn, u_ref[rows, :], (((1,), (1,)), ((), ())),
                              preferred_element_type=F32)
        act = 0.5 * pre * (1.0 + lax.erf(pre * (2.0 ** -0.5)))
        words = [slab_ref[pl.ds(k1_base + k, tm // 2, stride=SUBLANES), :] for k in range(k1_per_chunk)]
        first = jnp.concatenate([lax.bitcast_convert_type(x & BF16_HI, F32) for x in words], axis=1)
        second = jnp.concatenate(
            [lax.bitcast_convert_type(lax.shift_left(x, HALF_WORD), F32) for x in words], axis=1)
        gate = jnp.concatenate([first, second], axis=0)
        parts.append((act * gate).astype(BF16))
    a_ref[...] = parts[0] if len(parts) == 1 else jnp.concatenate(parts, axis=1)


def _peer_out_kernel(a_ref, v_ref, res_ref, fg_ref, o_ref, *, final_norm):
    j = pl.program_id(1)

    @pl.when(j == 0)
    def _():
        o_ref[...] = res_ref[...]

    o_ref[...] += jnp.dot(a_ref[...], v_ref[...], preferred_element_type=F32)

    if final_norm:
        @pl.when(j == pl.num_programs(1) - 1)
        def _():
            o_ref[...] = _rms(o_ref[...], fg_ref[...])


ACT_SLABS = 2
DENSE_TOKENS = 1024


def peer_dense(hn, u, w, v, layer, res, final_g, tm):
    t, d = hn.shape
    n_exp = u.shape[1]
    te = SUBLANES * PEER_NKEYS
    act_te = ACT_SLABS * te
    act = pl.pallas_call(
        _peer_act_kernel,
        grid=(t // tm, n_exp // act_te),
        in_specs=[
            pl.BlockSpec((tm, d), lambda i, j: (i, 0)),
            pl.BlockSpec((None, act_te, d), lambda i, j: (layer, j, 0)),
            pl.BlockSpec((ACT_SLABS, (tm // 2) * SUBLANES, PEER_NKEYS), lambda i, j: (j, i, 0)),
        ],
        out_specs=pl.BlockSpec((tm, act_te), lambda i, j: (i, j)),
        out_shape=jax.ShapeDtypeStruct((t, n_exp), BF16),
        compiler_params=_cparams(("parallel", "parallel"), DENSE_VMEM_LIMIT),
        name="peer_act",
    )(hn, u, w)
    final_norm = final_g is not None
    fg = final_g if final_norm else jnp.ones((d,), F32)
    return pl.pallas_call(
        functools.partial(_peer_out_kernel, final_norm=final_norm),
        grid=(t // tm, n_exp // te),
        in_specs=[
            pl.BlockSpec((tm, te), lambda i, j: (i, j)),
            pl.BlockSpec((None, te, d), lambda i, j: (layer, j, 0)),
            pl.BlockSpec((tm, d), lambda i, j: (i, 0)),
            pl.BlockSpec((1, d), lambda i, j: (0, 0)),
        ],
        out_specs=pl.BlockSpec((tm, d), lambda i, j: (i, 0)),
        out_shape=jax.ShapeDtypeStruct((t, d), F32),
        compiler_params=_cparams(("parallel", "arbitrary"), DENSE_VMEM_LIMIT),
        name="peer_out",
    )(act, v, res, fg.reshape(1, d))


def _add_rmsnorm_kernel(a_ref, b_ref, g_ref, o_ref):
    o_ref[...] = _rms(a_ref[...] + b_ref[...], g_ref[...])


def add_rmsnorm(a, b, g, tm=512):
    t, d = a.shape
    tm = min(tm, t)
    return pl.pallas_call(
        _add_rmsnorm_kernel,
        grid=(t // tm,),
        in_specs=[
            pl.BlockSpec((tm, d), lambda i: (i, 0)),
            pl.BlockSpec((tm, d), lambda i: (i, 0)),
            pl.BlockSpec((1, d), lambda i: (0, 0)),
        ],
        out_specs=pl.BlockSpec((tm, d), lambda i: (i, 0)),
        out_shape=jax.ShapeDtypeStruct((t, d), F32),
        compiler_params=_cparams(("parallel",)),
        name="add_rmsnorm",
    )(a, b, g.reshape(1, d))


LRU_ROWS = 8


def _lru_kernel(gate_ref, rec_ref, cw_ref, cb_ref, wa_ref, ba_ref, wx_ref, bx_ref, lam_ref, o_ref,
                tail_ref, carry_ref, a_ref, u_ref, hs_ref, *, ts):
    @pl.when(pl.program_id(1) == 0)
    def _():
        tail_ref[...] = jnp.zeros_like(tail_ref)
        carry_ref[...] = jnp.zeros_like(carry_ref)

    x = rec_ref[...].astype(F32)
    xp = jnp.concatenate([tail_ref[...], x], axis=0)
    tail_ref[...] = x[ts - LRU_ROWS:, :]
    xc = cb_ref[...] + cw_ref[CONV_WIDTH - 1:CONV_WIDTH, :] * x
    for back in range(1, CONV_WIDTH):
        shifted = pltpu.roll(xp, back, 0)[LRU_ROWS:, :]
        xc = xc + cw_ref[CONV_WIDTH - 1 - back:CONV_WIDTH - back, :] * shifted

    soft = jnp.log(1.0 + jnp.exp(-lam_ref[...]))
    for n in range(LRU_BLOCKS):
        sl = slice(n * LRU_BLOCK, (n + 1) * LRU_BLOCK)
        xb = xc[:, sl]
        xb16 = xb.astype(BF16)
        r = jax.nn.sigmoid(jnp.dot(xb16, wa_ref[n], preferred_element_type=F32) + ba_ref[:, sl])
        i = jax.nn.sigmoid(jnp.dot(xb16, wx_ref[n], preferred_element_type=F32) + bx_ref[:, sl])
        log_a = -LRU_C * r * soft[:, sl]
        a = jnp.exp(log_a)
        a_ref[:, sl] = a
        u_ref[:, sl] = jnp.sqrt(1.0 - a * a) * (i * xb)

    groups = ts // LRU_ROWS
    a = a_ref[...].reshape(groups, LRU_ROWS, -1)
    u = u_ref[...].reshape(groups, LRU_ROWS, -1)
    sub = lax.broadcasted_iota(I32, a.shape, 1)
    for shift in (1, 2, 4):
        ok = sub >= shift
        u = u + a * jnp.where(ok, pltpu.roll(u, shift, 1), 0.0)
        a = a * jnp.where(ok, pltpu.roll(a, shift, 1), 1.0)
    a_ref[...] = a.reshape(ts, -1)
    u_ref[...] = u.reshape(ts, -1)

    def step(c, h):
        r0 = pl.multiple_of(c * LRU_ROWS, LRU_ROWS)
        h8 = u_ref[pl.ds(r0, LRU_ROWS), :] + a_ref[pl.ds(r0, LRU_ROWS), :] * h
        hs_ref[pl.ds(r0, LRU_ROWS), :] = h8
        return h8[LRU_ROWS - 1:, :]

    carry_ref[...] = lax.fori_loop(0, ts // LRU_ROWS, step, carry_ref[...])
    o_ref[...] = (jax.nn.gelu(gate_ref[...].astype(F32)) * hs_ref[...]).astype(BF16)


def lru_branch(z, conv_w, conv_b, wa, ba, wx, bx, lam, batch, seq, ts=256):
    t = z.shape[0]
    w = z.shape[1] // 2
    ts = min(ts, seq)
    per_b = seq // ts
    row = lambda a: pl.BlockSpec(a.shape, lambda b, s: (0, 0))
    cube = lambda a: pl.BlockSpec(a.shape, lambda b, s: (0, 0, 0))
    return pl.pallas_call(
        functools.partial(_lru_kernel, ts=ts),
        grid=(batch, per_b),
        in_specs=[
            pl.BlockSpec((ts, w), lambda b, s: (b * per_b + s, 0)),
            pl.BlockSpec((ts, w), lambda b, s: (b * per_b + s, 1)),
            row(conv_w), row(conv_b), cube(wa), row(ba), cube(wx), row(bx), row(lam),
        ],
        out_specs=pl.BlockSpec((ts, w), lambda b, s: (b * per_b + s, 0)),
        out_shape=jax.ShapeDtypeStruct((t, w), BF16),
        scratch_shapes=[
            pltpu.VMEM((LRU_ROWS, w), F32),
            pltpu.VMEM((1, w), F32),
            pltpu.VMEM((ts, w), F32),
            pltpu.VMEM((ts, w), F32),
            pltpu.VMEM((ts, w), F32),
        ],
        compiler_params=_cparams(("parallel", "arbitrary")),
        name="lru_branch",
    )(z, z, conv_w, conv_b, wa, ba, wx, bx, lam)


def _rope_tables(positions, dim, width):
    inv = ROPE_THETA ** (-jnp.arange(0, dim, 2, dtype=F32) / dim)
    ang = positions.astype(F32).reshape(-1, 1) * inv
    pad = jnp.zeros((ang.shape[0], width - dim), F32)
    cos = jnp.concatenate([jnp.cos(ang), jnp.cos(ang), pad], axis=-1)
    sin = jnp.concatenate([jnp.sin(ang), jnp.sin(ang), pad], axis=-1)
    return cos, sin


def _rotate_half_cols(w):
    half = w.shape[-1] // 2
    return jnp.concatenate([-w[..., half:], w[..., :half]], axis=-1)


def _even_weights(w_in, w_uq):
    d = w_in.shape[0]
    latent = MLA_Q_RANK + MLA_KV_RANK
    pad = LANES - MLA_ROPE
    z_in = jnp.zeros((d, pad), F32)
    k_rope = w_in[:, latent:latent + MLA_ROPE]
    w_in_p = jnp.concatenate(
        [w_in[:, :latent], k_rope, z_in, _rotate_half_cols(k_rope), z_in, w_in[:, latent + MLA_ROPE:]],
        axis=-1)
    wq3 = w_uq.reshape(MLA_Q_RANK, MLA_HEADS, MLA_NOPE + MLA_ROPE)
    nope, rope = wq3[..., :MLA_NOPE], wq3[..., MLA_NOPE:]
    zq = jnp.zeros((MLA_Q_RANK, MLA_HEADS, pad), F32)
    wqa = jnp.concatenate([nope, rope, zq], axis=-1).reshape(MLA_Q_RANK, MLA_HEADS * MLA_QK)
    wqb = jnp.concatenate([_rotate_half_cols(rope), zq], axis=-1).reshape(MLA_Q_RANK, MLA_HEADS * LANES)
    return w_in_p.astype(BF16), wqa.astype(BF16), wqb.astype(BF16)


def _memory_and_route(h, mem2, layer, seq, xa_norm_g, mem_norm_g, xa_wq, xa_wk, xa_wv, xa_wo,
                      ffn_norm_g, peer_wq, peer_sub_keys):
    w_kv = jnp.concatenate([xa_wk[layer], xa_wv[layer]], axis=-1).astype(BF16)
    memkv = norm_matmul([mem2], mem_norm_g[layer], w_kv)
    h = cross_attention(h, xa_norm_g[layer], xa_wq[layer].astype(BF16), memkv,
                        xa_wo[layer].astype(BF16), seq)
    keys = peer_sub_keys[layer].reshape(2 * PEER_HEADS, PEER_NKEYS, PEER_HALF).astype(BF16)
    hn, scores = peer_scores(h, ffn_norm_g[layer], peer_wq[layer].T.astype(BF16), keys)
    e_sel, g_sel = peer_topk(scores)
    return h, hn, peer_gates(e_sel, g_sel, min(DENSE_TOKENS, h.shape[0]))


def kernel(x, mem, positions, mix_norm_g, xa_norm_g, ffn_norm_g, mem_norm_g, ev_w_in, ev_w_out, mla_q_norm_g, mla_w_uq, mla_kv_norm_g, mla_w_ukv, ret_gn_g, od_w_in, od_w_out, conv_w, conv_b, lru_wa, lru_ba, lru_wx, lru_bx, lru_lambda, xa_wq, xa_wk, xa_wv, xa_wo, peer_wq, peer_sub_keys, peer_u, peer_v, final_norm_g):
    batch, seq, d = x.shape
    t = batch * seq
    h = x.reshape(t, d)
    mem2 = mem.reshape(-1, d)
    tail = (xa_norm_g, mem_norm_g, xa_wq, xa_wk, xa_wv, xa_wo, ffn_norm_g, peer_wq, peer_sub_keys)
    u16 = peer_u.astype(BF16)
    v16 = peer_v.astype(BF16)

    w_in_p, wqa, wqb = _even_weights(ev_w_in[0], mla_w_uq[0])
    z = norm_matmul([h], mix_norm_g[0], w_in_p)
    cos_m, sin_m = _rope_tables(positions, MLA_ROPE, LANES)
    qf, kvf, kr = mla_proj(z, cos_m, sin_m, mla_q_norm_g[0].reshape(1, -1),
                           mla_kv_norm_g[0].reshape(1, -1), wqa, wqb, mla_w_ukv[0].astype(BF16))
    attn = mla_attention(qf, kvf, kr, batch, seq)
    cos_r, sin_r = _rope_tables(positions, RET_DK, RET_DK)
    sign = jnp.concatenate([-jnp.ones((RET_DK // 2,), F32), jnp.ones((RET_DK // 2,), F32)])
    log_g = jnp.log(1.0 - 2.0 ** (-5.0 - jnp.arange(RET_HEADS, dtype=F32)))
    logg = jnp.broadcast_to(log_g[:, None, None], (RET_HEADS, 1, LANES))
    ret = retention(z, cos_r, sin_r * sign, logg, ret_gn_g[0].reshape(RET_HEADS, 1, RET_DV), batch, seq)
    w_out = ev_w_out[0].astype(BF16)
    n_a = MLA_HEADS * MLA_V
    h = matmul_res([attn, ret], [w_out[:n_a], w_out[n_a:]], [h])
    h, hn, w = _memory_and_route(h, mem2, 0, seq, *tail)
    tile = min(DENSE_TOKENS, t)
    h = peer_dense(hn, u16, w, v16, 0, h, None, tile)

    z = norm_matmul([h], mix_norm_g[1], od_w_in[0].astype(BF16))
    row = lambda a: a.reshape(1, -1)
    cw = jnp.concatenate([conv_w[0], jnp.zeros((SUBLANES - CONV_WIDTH, conv_w.shape[-1]), F32)], axis=0)
    y = lru_branch(z, cw, row(conv_b[0]), lru_wa[0].astype(BF16), row(lru_ba[0]),
                   lru_wx[0].astype(BF16), row(lru_bx[0]), row(lru_lambda[0]), batch, seq)
    h = matmul_res([y], [od_w_out[0].astype(BF16)], [h])
    h, hn, w = _memory_and_route(h, mem2, 1, seq, *tail)
    out = peer_dense(hn, u16, w, v16, 1, h, final_norm_g, tile)
    return out.reshape(batch, seq, d)
```
